```python
import jax, jax.numpy as jnp
from jax import lax
import numpy as np

D_MODEL = 2048
BATCH = 4
SEQ = 2048
DEPTH = 2

GRID_W = 64
CTX_LEN = 256

NA_HEADS = 16
NA_HEAD_DIM = 64
NA_WIN_ROWS = 8
NA_WIN_COLS = 16
GLA_HEADS = 4
GLA_DK = 128
GLA_DV = 256
GLA_GATE_RANK = 16
GLA_TAU = 16.0
GLA_CHUNK = 64
GQA_HEADS = 8
GQA_KV_HEADS = 2
GQA_HEAD_DIM = 128
ROPE_THETA = 10000.0
Q_BLOCK = 128
D_FF = 5632
N_EXPERTS = 8
TOP_K = 2
MOE_BLOCK = 256
NORM_EPS = 1e-6

NA_W = NA_HEADS * NA_HEAD_DIM
GLA_KW = GLA_HEADS * GLA_DK
GLA_VW = GLA_HEADS * GLA_DV
GQA_QW = GQA_HEADS * GQA_HEAD_DIM
GQA_KVW = GQA_KV_HEADS * GQA_HEAD_DIM
BRANCH_W = 1024

IN_SPLITS = (
    ("na_q", NA_W), ("na_k", NA_W), ("na_v", NA_W),
    ("gla_q", GLA_KW), ("gla_k", GLA_KW), ("gla_v", GLA_VW), ("gla_og", GLA_VW),
    ("gla_af", GLA_GATE_RANK), ("gla_ab", GLA_GATE_RANK),
    ("gqa_q", GQA_QW), ("gqa_k", GQA_KVW), ("gqa_v", GQA_KVW),
    ("gate_a", D_MODEL), ("gate_b", D_MODEL), ("gate_c", D_MODEL),
)
D_IN = 3 * NA_W + 2 * GLA_KW + 2 * GLA_VW + 2 * GLA_GATE_RANK + GQA_QW + 2 * GQA_KVW + 3 * D_MODEL
CTX_KV_SPLITS = ("na_k", "na_v", "gla_k", "gla_v", "gla_af", "gla_ab", "gqa_k", "gqa_v")

kernel_name = "hybrid_diffusion_parallel_mixer_block"


def rmsnorm(x, g):
    xf = x.astype(jnp.float32)
    y = xf * lax.rsqrt(jnp.mean(xf * xf, axis=-1, keepdims=True) + NORM_EPS)
    return (y * g.astype(jnp.float32)).astype(x.dtype)


def swiglu(h, w_gate, w_up, w_down):
    return (jax.nn.silu(h @ w_gate) * (h @ w_up)) @ w_down


def in_proj(h, w_in, names=None):
    out, o = {}, 0
    p = h @ w_in if names is None else None
    for name, n in IN_SPLITS:
        if names is None:
            out[name] = p[..., o:o + n]
        elif name in names:
            out[name] = h @ w_in[:, o:o + n]
        o += n
    return out


def axial_rope_tables(n_tok):
    half = GQA_HEAD_DIM // 2
    freqs = ROPE_THETA ** (-jnp.arange(0, half, 2, dtype=jnp.float32) / half)
    t = jnp.arange(n_tok)
    row = (t // GRID_W).astype(jnp.float32)
    col = (t % GRID_W).astype(jnp.float32)
    ang = jnp.concatenate([row[:, None] * freqs, col[:, None] * freqs], axis=-1)
    return jnp.cos(ang), jnp.sin(ang)


def apply_rope(x, cos, sin):
    xf = x.astype(jnp.float32)
    x1, x2 = xf[..., 0::2], xf[..., 1::2]
    c, s = cos[None, :, None, :], sin[None, :, None, :]
    out = jnp.stack([x1 * c - x2 * s, x1 * s + x2 * c], axis=-1).reshape(x.shape)
    return out.astype(x.dtype)


def _attend(q5, k, v):
    s = jnp.einsum('bqkgd,bnkd->bkgqn', q5, k).astype(jnp.float32) * (q5.shape[-1] ** -0.5)
    p = jax.nn.softmax(s, axis=-1).astype(v.dtype)
    return jnp.einsum('bkgqn,bnkd->bqkgd', p, v)


def gqa_dense(q, k, v):
    b, t, hq, dh = q.shape
    hkv = k.shape[2]
    return _attend(q.reshape(b, t, hkv, hq // hkv, dh), k, v).reshape(b, t, hq * dh)


def gqa_blocked(q, k, v, kc, vc):
    b, s, hq, dh = q.shape
    hkv = k.shape[2]
    k_all = jnp.concatenate([k, kc], axis=1)
    v_all = jnp.concatenate([v, vc], axis=1)
    qb = q.reshape(b, s // Q_BLOCK, Q_BLOCK, hkv, hq // hkv, dh).swapaxes(0, 1)
    o = lax.map(lambda qi: _attend(qi, k_all, v_all), qb)
    return o.swapaxes(0, 1).reshape(b, s, hq * dh)


def na_latent(q, k, v, kc, vc, rpb):
    b, s, h, dh = q.shape
    rows_n = s // GRID_W
    kr = min(NA_WIN_ROWS, rows_n)
    rows = jnp.arange(rows_n)
    row_idx = jnp.clip(rows - kr // 2, 0, rows_n - kr)[:, None] + jnp.arange(kr)[None, :]
    cols = jnp.arange(GRID_W)
    col_start = jnp.clip(cols - NA_WIN_COLS // 2, 0, GRID_W - NA_WIN_COLS)
    in_win = (cols[None, :] >= col_start[:, None]) & (cols[None, :] < col_start[:, None] + NA_WIN_COLS)
    dr = row_idx - rows[:, None]
    dc = jnp.clip(cols[None, :] - cols[:, None] + NA_WIN_COLS - 1, 0, 2 * NA_WIN_COLS - 2)
    bias = rpb[:, (dr + NA_WIN_ROWS - 1)[:, :, None, None], dc[None, None]]
    bias = bias.astype(jnp.float32).transpose(0, 1, 3, 2, 4)
    bias = jnp.where(in_win[:, None, :], bias, -jnp.inf)
    qg = q.reshape(b, rows_n, GRID_W, h, dh) * (dh ** -0.5)
    kg = k.reshape(b, rows_n, GRID_W, h, dh)[:, row_idx]
    vg = v.reshape(b, rows_n, GRID_W, h, dh)[:, row_idx]
    s_loc = jnp.einsum('brqhd,brkwhd->bhrqkw', qg, kg).astype(jnp.float32) + bias[None]
    s_ctx = jnp.einsum('brqhd,blhd->bhrql', qg, kc).astype(jnp.float32)
    n_loc = kr * GRID_W
    p = jax.nn.softmax(jnp.concatenate([s_loc.reshape(b, h, rows_n, GRID_W, n_loc), s_ctx], axis=-1), axis=-1)
    p = p.astype(v.dtype)
    p_loc = p[..., :n_loc].reshape(b, h, rows_n, GRID_W, kr, GRID_W)
    o = jnp.einsum('bhrqkw,brkwhd->brqhd', p_loc, vg) + jnp.einsum('bhrql,blhd->brqhd', p[..., n_loc:], vc)
    return o.reshape(b, s, h * dh)


def gla_scan(q, k, v, log_a, s0):
    b, t, h, dk = k.shape
    n = t // GLA_CHUNK

    def chunks(a):
        return a.astype(jnp.float32).reshape(b, n, GLA_CHUNK, h, a.shape[-1]).transpose(1, 0, 3, 2, 4)

    mask = jnp.tril(jnp.ones((GLA_CHUNK, GLA_CHUNK), bool))[..., None]
    with_out = q is not None
    xs = (chunks(k), chunks(v), chunks(log_a)) + ((chunks(q),) if with_out else ())

    def step(state, inp):
        kc, vc, gc = inp[0], inp[1], inp[2]
        bcum = jnp.cumsum(gc, axis=-2)
        b_last = bcum[..., -1:, :]
        new_state = jnp.exp(b_last)[..., 0, :, None] * state + jnp.einsum(
            'bhjd,bhje->bhde', kc * jnp.exp(b_last - bcum), vc)
        if not with_out:
            return new_state, None
        qc = inp[3]
        diff = bcum[..., :, None, :] - bcum[..., None, :, :]
        decay = jnp.exp(jnp.where(mask, diff, -jnp.inf))
        attn = jnp.einsum('bhid,bhjd,bhijd->bhij', qc, kc, decay)
        o = attn @ vc + jnp.einsum('bhid,bhde->bhie', qc * jnp.exp(bcum), state)
        return new_state, o

    s_fin, o = lax.scan(step, s0, xs)
    if with_out:
        o = o.transpose(1, 0, 3, 2, 4).reshape(b, t, h, v.shape[-1])
    return o, s_fin


def na_branch(pl, pc, rpb, with_ctx):
    def heads(a):
        return a.reshape(a.shape[0], a.shape[1], NA_HEADS, NA_HEAD_DIM)
    kc, vc = heads(pc['na_k']), heads(pc['na_v'])
    yl = na_latent(heads(pl['na_q']), heads(pl['na_k']), heads(pl['na_v']), kc, vc, rpb)
    yc = gqa_dense(heads(pc['na_q']), kc, vc) if with_ctx else None
    return yl, yc


def gla_branch(pl, pc, w_a2, b_a, norm_g, with_ctx):
    def heads(a, d):
        return a.reshape(a.shape[0], a.shape[1], GLA_HEADS, d)

    def log_decay(low, dirn):
        z = (low @ w_a2[dirn]).astype(jnp.float32) + b_a[dirn].astype(jnp.float32)
        return heads(jax.nn.log_sigmoid(z) / GLA_TAU, GLA_DK)

    def flip(a):
        return None if a is None else a[:, ::-1]

    def finish(o, og):
        o = rmsnorm(o.astype(og.dtype), norm_g)
        return o.reshape(og.shape) * jax.nn.silu(og)

    scale = GLA_DK ** -0.5
    bsz = pl['gla_k'].shape[0]
    s0 = jnp.zeros((bsz, GLA_HEADS, GLA_DK, GLA_DV), jnp.float32)
    qc = heads(pc['gla_q'], GLA_DK) * scale if with_ctx else None
    kc, vc = heads(pc['gla_k'], GLA_DK), heads(pc['gla_v'], GLA_DV)
    oc_f, s_f = gla_scan(qc, kc, vc, log_decay(pc['gla_af'], 0), s0)
    oc_b, s_b = gla_scan(flip(qc), flip(kc), flip(vc), flip(log_decay(pc['gla_ab'], 1)), s0)
    ql = heads(pl['gla_q'], GLA_DK) * scale
    kl, vl = heads(pl['gla_k'], GLA_DK), heads(pl['gla_v'], GLA_DV)
    ol_f, _ = gla_scan(ql, kl, vl, log_decay(pl['gla_af'], 0), s_f)
    ol_b, _ = gla_scan(flip(ql), flip(kl), flip(vl), flip(log_decay(pl['gla_ab'], 1)), s_b)
    yl = finish(ol_f + flip(ol_b), pl['gla_og'])
    yc = finish(oc_f + flip(oc_b), pc['gla_og']) if with_ctx else None
    return yl, yc


def gqa_branch(pl, pc, qn_g, kn_g, cos, sin, with_ctx):
    def heads(a, n):
        return a.reshape(a.shape[0], a.shape[1], n, GQA_HEAD_DIM)
    kc = rmsnorm(heads(pc['gqa_k'], GQA_KV_HEADS), kn_g)
    vc = heads(pc['gqa_v'], GQA_KV_HEADS)
    ql = apply_rope(rmsnorm(heads(pl['gqa_q'], GQA_HEADS), qn_g), cos, sin)
    kl = apply_rope(rmsnorm(heads(pl['gqa_k'], GQA_KV_HEADS), kn_g), cos, sin)
    vl = heads(pl['gqa_v'], GQA_KV_HEADS)
    yl = gqa_blocked(ql, kl, vl, kc, vc)
    yc = gqa_dense(rmsnorm(heads(pc['gqa_q'], GQA_HEADS), qn_g), kc, vc) if with_ctx else None
    return yl, yc


def mixer(hl, hc, w_in, rpb, gla_w_a2, gla_b_a, gla_norm_g, qn_g, kn_g,
          w_pa, w_pb, w_pc, w_out, cos, sin, with_ctx):
    pl = in_proj(hl, w_in)
    pc = in_proj(hc, w_in, None if with_ctx else CTX_KV_SPLITS)
    a_l, a_c = na_branch(pl, pc, rpb, with_ctx)
    b_l, b_c = gla_branch(pl, pc, gla_w_a2, gla_b_a, gla_norm_g, with_ctx)
    c_l, c_c = gqa_branch(pl, pc, qn_g, kn_g, cos, sin, with_ctx)

    def merge(p, a, b, c):
        m = (jax.nn.sigmoid(p['gate_a']) * (a @ w_pa) + jax.nn.sigmoid(p['gate_b']) * (b @ w_pb)
             + jax.nn.sigmoid(p['gate_c']) * (c @ w_pc))
        return m @ w_out

    yl = merge(pl, a_l, b_l, c_l)
    yc = merge(pc, a_c, b_c, c_c) if with_ctx else None
    return yl, yc


def moe_ffn(h, router_w, router_b, w_gate, w_up, w_down):
    x = h.reshape(-1, D_MODEL)
    t = x.shape[0]
    n_assign = t * TOP_K
    logits = (x @ router_w).astype(jnp.float32) + router_b.astype(jnp.float32)
    top_v, top_i = lax.top_k(logits, TOP_K)
    wts = jax.nn.softmax(top_v, axis=-1)
    flat_e = top_i.reshape(-1)
    flat_t = jnp.repeat(jnp.arange(t, dtype=jnp.int32), TOP_K)
    flat_w = wts.reshape(-1)
    order = jnp.argsort(flat_e)
    se = flat_e[order]
    counts = jnp.bincount(flat_e, length=N_EXPERTS)
    starts = jnp.cumsum(counts) - counts
    padded = (counts + MOE_BLOCK - 1) // MOE_BLOCK * MOE_BLOCK
    pends = jnp.cumsum(padded)
    pstarts = pends - padded
    dest = pstarts[se] + jnp.arange(n_assign) - starts[se]
    n_blk = -(-n_assign // MOE_BLOCK) + N_EXPERTS
    slot_t = jnp.full((n_blk * MOE_BLOCK,), t, jnp.int32).at[dest].set(flat_t[order])
    slot_w = jnp.zeros((n_blk * MOE_BLOCK,), jnp.float32).at[dest].set(flat_w[order])
    block_e = jnp.minimum(jnp.searchsorted(pends, jnp.arange(n_blk) * MOE_BLOCK, side='right'), N_EXPERTS - 1)
    xp = jnp.concatenate([x, jnp.zeros((1, D_MODEL), x.dtype)], axis=0)
    xb = xp[slot_t].reshape(n_blk, MOE_BLOCK, D_MODEL)
    yb = lax.map(lambda a: swiglu(a[0], w_gate[a[1]], w_up[a[1]], w_down[a[1]]), (xb, block_e))
    y = jnp.zeros((t + 1, D_MODEL), x.dtype).at[slot_t].add(
        yb.reshape(-1, D_MODEL) * slot_w[:, None].astype(x.dtype))
    return y[:t].reshape(h.shape)


def setup_inputs(seed: int = 0) -> dict:
    key = jax.random.key(seed)
    k = jax.random.split(key, 28)
    n_dense = (DEPTH + 1) // 2
    n_moe = DEPTH // 2
    d = D_MODEL

    def nrm(i, shape, scale):
        return jax.random.normal(k[i], shape, jnp.float32) * scale

    return {
        "x": nrm(0, (BATCH, SEQ, d), 1.0),
        "c": nrm(1, (BATCH, d), 1.0),
        "ctx": nrm(2, (BATCH, CTX_LEN, d), 1.0),
        "c_ctx": nrm(3, (d,), 1.0),
        "w_ada": nrm(4, (DEPTH, d, 6 * d), 0.5 * d ** -0.5),
        "b_ada": nrm(5, (DEPTH, 6 * d), 0.01),
        "norm1_g": 1.0 + nrm(6, (DEPTH, d), 0.05),
        "norm2_g": 1.0 + nrm(7, (DEPTH, d), 0.05),
        "w_in": nrm(8, (DEPTH, d, D_IN), d ** -0.5),
        "na_rpb": nrm(9, (DEPTH, NA_HEADS, 2 * NA_WIN_ROWS - 1, 2 * NA_WIN_COLS - 1), 0.1),
        "gla_w_a2": nrm(10, (DEPTH, 2, GLA_GATE_RANK, GLA_KW), GLA_GATE_RANK ** -0.5),
        "gla_b_a": nrm(11, (DEPTH, 2, GLA_KW), 0.1),
        "gla_norm_g": 1.0 + nrm(12, (DEPTH, GLA_DV), 0.05),
        "gqa_qn_g": 1.0 + nrm(13, (DEPTH, GQA_HEAD_DIM), 0.05),
        "gqa_kn_g": 1.0 + nrm(14, (DEPTH, GQA_HEAD_DIM), 0.05),
        "w_pa": nrm(15, (DEPTH, BRANCH_W, d), BRANCH_W ** -0.5),
        "w_pb": nrm(16, (DEPTH, BRANCH_W, d), BRANCH_W ** -0.5),
        "w_pc": nrm(17, (DEPTH, BRANCH_W, d), BRANCH_W ** -0.5),
        "w_out": nrm(18, (DEPTH, d, d), d ** -0.5),
        "dense_w_gate": nrm(19, (n_dense, d, D_FF), d ** -0.5),
        "dense_w_up": nrm(20, (n_dense, d, D_FF), d ** -0.5),
        "dense_w_down": nrm(21, (n_dense, D_FF, d), D_FF ** -0.5),
        "router_w": nrm(22, (n_moe, d, N_EXPERTS), d ** -0.5),
        "router_b": nrm(23, (n_moe, N_EXPERTS), 0.01),
        "moe_w_gate": nrm(24, (n_moe, N_EXPERTS, d, D_FF), d ** -0.5),
        "moe_w_up": nrm(25, (n_moe, N_EXPERTS, d, D_FF), d ** -0.5),
        "moe_w_down": nrm(26, (n_moe, N_EXPERTS, D_FF, d), D_FF ** -0.5),
        "final_norm_g": 1.0 + nrm(27, (d,), 0.05),
    }


def reference(x, c, ctx, c_ctx, w_ada, b_ada, norm1_g, norm2_g, w_in, na_rpb, gla_w_a2, gla_b_a,
              gla_norm_g, gqa_qn_g, gqa_kn_g, w_pa, w_pb, w_pc, w_out, dense_w_gate, dense_w_up,
              dense_w_down, router_w, router_b, moe_w_gate, moe_w_up, moe_w_down, final_norm_g):
    cos, sin = axial_rope_tables(x.shape[1])
    xl, xc = x, ctx
    c_act = jax.nn.silu(c)
    cc_act = jax.nn.silu(c_ctx)
    for i in range(DEPTH):
        last = i == DEPTH - 1
        mod_l = (c_act @ w_ada[i] + b_ada[i])[:, None, :]
        mod_c = cc_act @ w_ada[i] + b_ada[i]
        sh1, sc1, g1, sh2, sc2, g2 = jnp.split(mod_l, 6, axis=-1)
        csh1, csc1, cg1, csh2, csc2, cg2 = jnp.split(mod_c, 6, axis=-1)
        hl = rmsnorm(xl, norm1_g[i]) * (1.0 + sc1) + sh1
        hc = rmsnorm(xc, norm1_g[i]) * (1.0 + csc1) + csh1
        yl, yc = mixer(hl, hc, w_in[i], na_rpb[i], gla_w_a2[i], gla_b_a[i], gla_norm_g[i],
                       gqa_qn_g[i], gqa_kn_g[i], w_pa[i], w_pb[i], w_pc[i], w_out[i], cos, sin,
                       not last)
        xl = xl + g1 * yl
        j = i // 2
        if i % 2 == 0:
            ffn = lambda h: swiglu(h, dense_w_gate[j], dense_w_up[j], dense_w_down[j])
        else:
            ffn = lambda h: moe_ffn(h, router_w[j], router_b[j], moe_w_gate[j], moe_w_up[j], moe_w_down[j])
        xl = xl + g2 * ffn(rmsnorm(xl, norm2_g[i]) * (1.0 + sc2) + sh2)
        if not last:
            xc = xc + cg1 * yc
            xc = xc + cg2 * ffn(rmsnorm(xc, norm2_g[i]) * (1.0 + csc2) + csh2)
    return rmsnorm(xl, final_norm_g)
```

```python
import functools

import jax
import jax.numpy as jnp
from jax import lax
from jax.experimental import pallas as pl
from jax.experimental.pallas import tpu as pltpu

BF = jnp.bfloat16
F32 = jnp.float32

D_MODEL = 2048
SEQ = 2048
CTX_LEN = 256
DEPTH = 2
GRID_W = 64
GRID_ROWS = SEQ // GRID_W
NA_HEADS = 16
NA_HEAD_DIM = 64
NA_WIN_ROWS = 8
NA_WIN_COLS = 16
GLA_HEADS = 4
GLA_DK = 128
GLA_DV = 256
GLA_GATE_RANK = 16
GLA_TAU = 16.0
GQA_HEADS = 8
GQA_KV_HEADS = 2
GQA_HEAD_DIM = 128
GQA_GROUP = GQA_HEADS // GQA_KV_HEADS
ROPE_THETA = 10000.0
D_FF = 5632
N_EXPERTS = 8
TOP_K = 2
NORM_EPS = 1e-6
BRANCH_W = 1024

C_NA_Q, C_NA_K, C_NA_V = 0, 1024, 2048
C_GLA_Q, C_GLA_K, C_GLA_V, C_GLA_OG = 3072, 3584, 4096, 5120
C_GQA_Q, C_GQA_K, C_GQA_V = 6144, 7168, 7424
C_GATE_A, C_GATE_B, C_GATE_C = 7680, 9728, 11776
C_LOW = 13824
D_IN_PACKED = 14336
R_LOW = 6144
R_GQA_Q = 6176
R_END = 13856

V7X_LANES = 128
V7X_VMEM_LIMIT_BYTES = 56 * 1024 * 1024

GLA_STEP = 128
GLA_SUB = 16
MOE_ROWS = 512
MOE_FF_TILE = 512


def _cparams(*sem):
    return pltpu.CompilerParams(dimension_semantics=sem,
                                vmem_limit_bytes=V7X_VMEM_LIMIT_BYTES)


def _dot(a, b):
    return jnp.dot(a, b, preferred_element_type=F32)


def _dot_nt(a, b):
    return lax.dot_general(a, b, (((1,), (1,)), ((), ())), preferred_element_type=F32)


def _silu(x):
    return x * jax.nn.sigmoid(x)


def _latent_mod_row(tm):
    return lambda t: (t * tm) // SEQ


def _ctx_mod_row(n_batch):
    return lambda t: n_batch


def _ada_kernel(c_ref, w_ref, b_ref, o_ref):
    a = _silu(c_ref[...])
    o_ref[...] = _dot(a.astype(BF), w_ref[...].astype(BF)) + b_ref[...]


def ada_modulation(cvec, w_ada, b_ada):
    tn = 1024
    n6 = 6 * D_MODEL
    out = pl.pallas_call(
        _ada_kernel,
        grid=(DEPTH, n6 // tn),
        in_specs=[
            pl.BlockSpec((8, D_MODEL), lambda l, j: (0, 0)),
            pl.BlockSpec((None, D_MODEL, tn), lambda l, j: (l, 0, j)),
            pl.BlockSpec((None, 1, tn), lambda l, j: (l, 0, j)),
        ],
        out_specs=pl.BlockSpec((None, 8, tn), lambda l, j: (l, 0, j)),
        out_shape=jax.ShapeDtypeStruct((DEPTH, 8, n6), F32),
        compiler_params=_cparams("parallel", "parallel"),
        name="ada_modulation",
    )(cvec, w_ada, b_ada.reshape(DEPTH, 1, n6))
    return out.reshape(DEPTH, 8, 6, D_MODEL)


def _modulated_norm(x, g, mod, shift_idx, scale_idx):
    ms = jnp.mean(x * x, axis=-1, keepdims=True)
    y = x * lax.rsqrt(ms + NORM_EPS) * g
    return y * (1.0 + mod[scale_idx:scale_idx + 1]) + mod[shift_idx:shift_idx + 1]


def _norm_mod_kernel(x_ref, g_ref, mod_ref, o_ref, *, shift_idx, scale_idx):
    o_ref[...] = _modulated_norm(x_ref[...], g_ref[...], mod_ref[...],
                                 shift_idx, scale_idx).astype(o_ref.dtype)


def norm_mod(x, g, mod, mod_row, shift_idx, scale_idx, tm=512):
    m = x.shape[0]
    tm = min(tm, m)
    return pl.pallas_call(
        functools.partial(_norm_mod_kernel, shift_idx=shift_idx, scale_idx=scale_idx),
        grid=(m // tm,),
        in_specs=[
            pl.BlockSpec((tm, D_MODEL), lambda t: (t, 0)),
            pl.BlockSpec((1, D_MODEL), lambda t: (0, 0)),
            pl.BlockSpec((None, 6, D_MODEL), lambda t: (mod_row(t), 0, 0)),
        ],
        out_specs=pl.BlockSpec((tm, D_MODEL), lambda t: (t, 0)),
        out_shape=jax.ShapeDtypeStruct((m, D_MODEL), BF),
        compiler_params=_cparams("parallel"),
        name="norm_mod",
    )(x, g.reshape(1, D_MODEL), mod)


def _rmsnorm_kernel(x_ref, g_ref, o_ref):
    x = x_ref[...]
    ms = jnp.mean(x * x, axis=-1, keepdims=True)
    o_ref[...] = x * lax.rsqrt(ms + NORM_EPS) * g_ref[...]


def final_rmsnorm(x, g, tm=512):
    m = x.shape[0]
    return pl.pallas_call(
        _rmsnorm_kernel,
        grid=(m // tm,),
        in_specs=[pl.BlockSpec((tm, D_MODEL), lambda t: (t, 0)),
                  pl.BlockSpec((1, D_MODEL), lambda t: (0, 0))],
        out_specs=pl.BlockSpec((tm, D_MODEL), lambda t: (t, 0)),
        out_shape=jax.ShapeDtypeStruct((m, D_MODEL), F32),
        compiler_params=_cparams("parallel"),
        name="final_rmsnorm",
    )(x, g.reshape(1, D_MODEL))


def _mm_kernel(a_ref, w_ref, o_ref):
    o_ref[...] = _dot(a_ref[...], w_ref[...].astype(BF)).astype(o_ref.dtype)


def matmul(a, w, out_dtype, tm=1024, tn=512):
    m, k = a.shape
    n = w.shape[1]
    tm = min(tm, m)
    return pl.pallas_call(
        _mm_kernel,
        grid=(m // tm, n // tn),
        in_specs=[pl.BlockSpec((tm, k), lambda i, j: (i, 0)),
                  pl.BlockSpec((k, tn), lambda i, j: (0, j))],
        out_specs=pl.BlockSpec((tm, tn), lambda i, j: (i, j)),
        out_shape=jax.ShapeDtypeStruct((m, n), out_dtype),
        compiler_params=_cparams("parallel", "parallel"),
        name="matmul",
    )(a, w)


def _mm_res_kernel(a_ref, w_ref, x_ref, mod_ref, o_ref, *, gate_idx):
    y = _dot(a_ref[...], w_ref[...].astype(BF))
    o_ref[...] = x_ref[...] + mod_ref[gate_idx:gate_idx + 1, :] * y


def matmul_residual(a, w, x, mod, mod_row, gate_idx, tm=1024, tn=256):
    m, k = a.shape
    n = w.shape[1]
    tm = min(tm, m)
    return pl.pallas_call(
        functools.partial(_mm_res_kernel, gate_idx=gate_idx),
        grid=(m // tm, n // tn),
        in_specs=[pl.BlockSpec((tm, k), lambda i, j: (i, 0)),
                  pl.BlockSpec((k, tn), lambda i, j: (0, j)),
                  pl.BlockSpec((tm, tn), lambda i, j: (i, j)),
                  pl.BlockSpec((None, 6, tn), lambda i, j: (mod_row(i), 0, j))],
        out_specs=pl.BlockSpec((tm, tn), lambda i, j: (i, j)),
        out_shape=jax.ShapeDtypeStruct((m, n), F32),
        compiler_params=_cparams("parallel", "parallel"),
        name="matmul_residual",
    )(a, w, x, mod)


def _merge_kernel(a_ref, b_ref, c_ref, ga_ref, gb_ref, gc_ref, wa_ref, wb_ref, wc_ref, o_ref):
    def branch(x_ref, g_ref, w_ref):
        return jax.nn.sigmoid(g_ref[...].astype(F32)) * _dot(x_ref[...], w_ref[...].astype(BF))

    o_ref[...] = (branch(a_ref, ga_ref, wa_ref) + branch(b_ref, gb_ref, wb_ref)
                  + branch(c_ref, gc_ref, wc_ref)).astype(o_ref.dtype)


def merge_branches(a, b, c, proj, w_pa, w_pb, w_pc, tm=1024, tn=512):
    m = a.shape[0]
    tm = min(tm, m)
    x_spec = pl.BlockSpec((tm, BRANCH_W), lambda i, j: (i, 0))
    w_spec = pl.BlockSpec((BRANCH_W, tn), lambda i, j: (0, j))

    def gate_spec(col):
        return pl.BlockSpec((tm, tn), lambda i, j: (i, col // tn + j))

    return pl.pallas_call(
        _merge_kernel,
        grid=(m // tm, D_MODEL // tn),
        in_specs=[x_spec, x_spec, x_spec,
                  gate_spec(C_GATE_A), gate_spec(C_GATE_B), gate_spec(C_GATE_C),
                  w_spec, w_spec, w_spec],
        out_specs=pl.BlockSpec((tm, tn), lambda i, j: (i, j)),
        out_shape=jax.ShapeDtypeStruct((m, D_MODEL), BF),
        compiler_params=_cparams("parallel", "parallel"),
        name="merge_branches",
    )(a, b, c, proj, proj, proj, w_pa, w_pb, w_pc)


def _ffn_up_kernel(h_ref, wg_ref, wu_ref, o_ref):
    h = h_ref[...]
    g = _dot(h, wg_ref[...].astype(BF))
    u = _dot(h, wu_ref[...].astype(BF))
    o_ref[...] = (_silu(g) * u).astype(o_ref.dtype)


def ffn_up(h, w_gate, w_up, tm=1024, tn=512):
    m = h.shape[0]
    tm = min(tm, m)
    return pl.pallas_call(
        _ffn_up_kernel,
        grid=(m // tm, D_FF // tn),
        in_specs=[pl.BlockSpec((tm, D_MODEL), lambda i, j: (i, 0)),
                  pl.BlockSpec((D_MODEL, tn), lambda i, j: (0, j)),
                  pl.BlockSpec((D_MODEL, tn), lambda i, j: (0, j))],
        out_specs=pl.BlockSpec((tm, tn), lambda i, j: (i, j)),
        out_shape=jax.ShapeDtypeStruct((m, D_FF), BF),
        compiler_params=_cparams("parallel", "parallel"),
        name="ffn_up",
    )(h, w_gate, w_up)


def _gqa_prep_kernel(q_ref, k_ref, cos_ref, sin_ref, qg_ref, kg_ref, qo_ref, ko_ref):
    cos = cos_ref[...]
    sin = sin_ref[...]
    lane = lax.broadcasted_iota(jnp.int32, cos.shape, 1)
    even = (lane & 1) == 0

    def prep(x_ref, g_ref, o_ref, heads):
        for h in range(heads):
            sl = slice(h * GQA_HEAD_DIM, (h + 1) * GQA_HEAD_DIM)
            x = x_ref[:, sl].astype(F32)
            ms = jnp.mean(x * x, axis=-1, keepdims=True)
            y = x * lax.rsqrt(ms + NORM_EPS) * g_ref[...]
            swapped = jnp.where(even, pltpu.roll(y, GQA_HEAD_DIM - 1, 1), pltpu.roll(y, 1, 1))
            o_ref[:, sl] = (y * cos + swapped * sin).astype(o_ref.dtype)

    prep(q_ref, qg_ref, qo_ref, GQA_HEADS)
    prep(k_ref, kg_ref, ko_ref, GQA_KV_HEADS)


def gqa_prep(proj, cos_t, sin_t, qn_g, kn_g, table_block, tm=256):
    m = proj.shape[0]
    qw = GQA_HEADS * GQA_HEAD_DIM
    kw = GQA_KV_HEADS * GQA_HEAD_DIM
    return pl.pallas_call(
        _gqa_prep_kernel,
        grid=(m // tm,),
        in_specs=[pl.BlockSpec((tm, qw), lambda t: (t, C_GQA_Q // qw)),
                  pl.BlockSpec((tm, kw), lambda t: (t, C_GQA_K // kw)),
                  pl.BlockSpec((tm, GQA_HEAD_DIM), lambda t: (table_block(t), 0)),
                  pl.BlockSpec((tm, GQA_HEAD_DIM), lambda t: (table_block(t), 0)),
                  pl.BlockSpec((1, GQA_HEAD_DIM), lambda t: (0, 0)),
                  pl.BlockSpec((1, GQA_HEAD_DIM), lambda t: (0, 0))],
        out_specs=[pl.BlockSpec((tm, qw), lambda t: (t, 0)),
                   pl.BlockSpec((tm, kw), lambda t: (t, 0))],
        out_shape=[jax.ShapeDtypeStruct((m, qw), BF), jax.ShapeDtypeStruct((m, kw), BF)],
        compiler_params=_cparams("parallel"),
        name="gqa_prep",
    )(proj, proj, cos_t, sin_t, qn_g.reshape(1, -1), kn_g.reshape(1, -1))


def rope_tables():
    half = GQA_HEAD_DIM // 2
    freqs = ROPE_THETA ** (-jnp.arange(0, half, 2, dtype=F32) / half)
    t = jnp.arange(SEQ)
    row = (t // GRID_W).astype(F32)
    col = (t % GRID_W).astype(F32)
    ang = jnp.concatenate([row[:, None] * freqs, col[:, None] * freqs], axis=-1)
    cos = jnp.repeat(jnp.cos(ang), 2, axis=-1)
    sin = jnp.repeat(jnp.sin(ang), 2, axis=-1)
    sign = jnp.tile(jnp.array([-1.0, 1.0], F32), half)
    cos = jnp.concatenate([cos, jnp.ones((256, GQA_HEAD_DIM), F32)], axis=0)
    sin = jnp.concatenate([sin * sign, jnp.zeros((256, GQA_HEAD_DIM), F32)], axis=0)
    return cos, sin


def _attend(q, kv_list, scale):
    scores = []
    for k, _, bias in kv_list:
        s = _dot_nt(q, k) * scale
        if bias is not None:
            s = s + bias
        scores.append(s)
    m = functools.reduce(jnp.maximum, [jnp.max(s, axis=-1, keepdims=True) for s in scores])
    ps = [jnp.exp(s - m) for s in scores]
    den = functools.reduce(jnp.add, [jnp.sum(p, axis=-1, keepdims=True) for p in ps])
    o = functools.reduce(jnp.add, [_dot(p.astype(BF), v) for p, (_, v, _) in zip(ps, kv_list)])
    return o / den


def _head_pair_rows(q2):
    lane = lax.broadcasted_iota(jnp.int32, q2.shape, 1)
    first = lane < NA_HEAD_DIM
    zero = jnp.zeros_like(q2)
    return jnp.concatenate([jnp.where(first, q2, zero), jnp.where(first, zero, q2)], axis=0)


def _head_pair_merge(o, m):
    lane = lax.broadcasted_iota(jnp.int32, (m, 2 * NA_HEAD_DIM), 1)
    return jnp.where(lane < NA_HEAD_DIM, o[:m], o[m:])


def _na_kernel(q_ref, kl_ref, vl_ref, kc_ref, vc_ref, bias_ref, o_ref):
    r = pl.program_id(1)
    start = jnp.clip(r - NA_WIN_ROWS // 2, 0, GRID_ROWS - NA_WIN_ROWS) * GRID_W
    start = pl.multiple_of(start, GRID_W)
    n_loc = NA_WIN_ROWS * GRID_W
    scale = NA_HEAD_DIM ** -0.5
    for hp in range(NA_HEADS // 2):
        sl = slice(hp * 128, (hp + 1) * 128)
        qq = _head_pair_rows(q_ref[:, sl])
        bias = jnp.concatenate([bias_ref[2 * hp], bias_ref[2 * hp + 1]], axis=0)
        o = _attend(qq, [(kl_ref[pl.ds(start, n_loc), sl], vl_ref[pl.ds(start, n_loc), sl], bias),
                         (kc_ref[:, sl], vc_ref[:, sl], None)], scale)
        o_ref[:, sl] = _head_pair_merge(o, GRID_W).astype(o_ref.dtype)


def na_latent(proj_l, proj_c, bias_tab, n_batch):
    w = NA_HEADS * NA_HEAD_DIM

    def pattern(r):
        return jnp.where(r < 4, r, jnp.where(r > GRID_ROWS - 4, r - (GRID_ROWS - NA_WIN_ROWS), 4))

    return pl.pallas_call(
        _na_kernel,
        grid=(n_batch, GRID_ROWS),
        in_specs=[pl.BlockSpec((GRID_W, w), lambda b, r: (b * GRID_ROWS + r, C_NA_Q // w)),
                  pl.BlockSpec((SEQ, w), lambda b, r: (b, C_NA_K // w)),
                  pl.BlockSpec((SEQ, w), lambda b, r: (b, C_NA_V // w)),
                  pl.BlockSpec((CTX_LEN, w), lambda b, r: (b, C_NA_K // w)),
                  pl.BlockSpec((CTX_LEN, w), lambda b, r: (b, C_NA_V // w)),
                  pl.BlockSpec((None, NA_HEADS, GRID_W, NA_WIN_ROWS * GRID_W),
                               lambda b, r: (pattern(r), 0, 0, 0))],
        out_specs=pl.BlockSpec((GRID_W, w), lambda b, r: (b * GRID_ROWS + r, 0)),
        out_shape=jax.ShapeDtypeStruct((n_batch * SEQ, w), BF),
        compiler_params=_cparams("parallel", "arbitrary"),
        name="na_latent",
    )(proj_l, proj_l, proj_l, proj_c, proj_c, bias_tab)


def na_bias_table(rpb):
    cols = jnp.arange(GRID_W)
    col_start = jnp.clip(cols - NA_WIN_COLS // 2, 0, GRID_W - NA_WIN_COLS)
    in_win = (cols[None, :] >= col_start[:, None]) & (cols[None, :] < col_start[:, None] + NA_WIN_COLS)
    dc = jnp.clip(cols[None, :] - cols[:, None] + NA_WIN_COLS - 1, 0, 2 * NA_WIN_COLS - 2)
    dr = jnp.arange(NA_WIN_ROWS)[None, :] - jnp.arange(NA_WIN_ROWS)[:, None]
    bias = rpb[:, (dr + NA_WIN_ROWS - 1)[:, :, None, None], dc[None, None]]
    bias = bias.astype(F32).transpose(1, 0, 3, 2, 4)
    bias = jnp.where(in_win[:, None, :], bias, -jnp.inf)
    return bias.reshape(NA_WIN_ROWS, NA_HEADS, GRID_W, NA_WIN_ROWS * GRID_W)


def _na_ctx_kernel(q_ref, k_ref, v_ref, o_ref):
    scale = NA_HEAD_DIM ** -0.5
    for hp in range(NA_HEADS // 2):
        sl = slice(hp * 128, (hp + 1) * 128)
        o = _attend(_head_pair_rows(q_ref[:, sl]), [(k_ref[:, sl], v_ref[:, sl], None)], scale)
        o_ref[:, sl] = _head_pair_merge(o, CTX_LEN).astype(o_ref.dtype)


def na_context(proj_c, n_batch):
    w = NA_HEADS * NA_HEAD_DIM
    return pl.pallas_call(
        _na_ctx_kernel,
        grid=(n_batch,),
        in_specs=[pl.BlockSpec((CTX_LEN, w), lambda b: (b, C_NA_Q // w)),
                  pl.BlockSpec((CTX_LEN, w), lambda b: (b, C_NA_K // w)),
                  pl.BlockSpec((CTX_LEN, w), lambda b: (b, C_NA_V // w))],
        out_specs=pl.BlockSpec((CTX_LEN, w), lambda b: (b, 0)),
        out_shape=jax.ShapeDtypeStruct((n_batch * CTX_LEN, w), BF),
        compiler_params=_cparams("parallel"),
        name="na_context",
    )(proj_c, proj_c, proj_c)


def _gqa_rows(q_ref):
    return jnp.concatenate([q_ref[:, g * GQA_HEAD_DIM:(g + 1) * GQA_HEAD_DIM]
                            for g in range(GQA_GROUP)], axis=0)


def _gqa_store(o, o_ref):
    tq = o_ref.shape[0]
    for g in range(GQA_GROUP):
        o_ref[:, g * GQA_HEAD_DIM:(g + 1) * GQA_HEAD_DIM] = o[g * tq:(g + 1) * tq].astype(o_ref.dtype)


def _gqa_kernel(q_ref, kl_ref, vl_ref, kc_ref, vc_ref, o_ref):
    o = _attend(_gqa_rows(q_ref), [(kl_ref[...], vl_ref[...], None), (kc_ref[...], vc_ref[...], None)],
                GQA_HEAD_DIM ** -0.5)
    _gqa_store(o, o_ref)


def _gqa_ctx_kernel(q_ref, kc_ref, vc_ref, o_ref):
    o = _attend(_gqa_rows(q_ref), [(kc_ref[...], vc_ref[...], None)], GQA_HEAD_DIM ** -0.5)
    _gqa_store(o, o_ref)


def gqa_latent(q_l, k_l, k_c, proj_l, proj_c, n_batch, tq=256):
    gw = GQA_GROUP * GQA_HEAD_DIM
    nq = SEQ // tq
    dh = GQA_HEAD_DIM
    return pl.pallas_call(
        _gqa_kernel,
        grid=(n_batch, GQA_KV_HEADS, nq),
        in_specs=[pl.BlockSpec((tq, gw), lambda b, h, i: (b * nq + i, h)),
                  pl.BlockSpec((SEQ, dh), lambda b, h, i: (b, h)),
                  pl.BlockSpec((SEQ, dh), lambda b, h, i: (b, C_GQA_V // dh + h)),
                  pl.BlockSpec((CTX_LEN, dh), lambda b, h, i: (b, h)),
                  pl.BlockSpec((CTX_LEN, dh), lambda b, h, i: (b, C_GQA_V // dh + h))],
        out_specs=pl.BlockSpec((tq, gw), lambda b, h, i: (b * nq + i, h)),
        out_shape=jax.ShapeDtypeStruct((n_batch * SEQ, GQA_HEADS * dh), BF),
        compiler_params=_cparams("parallel", "parallel", "arbitrary"),
        name="gqa_latent",
    )(q_l, k_l, proj_l, k_c, proj_c)


def gqa_context(q_c, k_c, proj_c, n_batch):
    gw = GQA_GROUP * GQA_HEAD_DIM
    dh = GQA_HEAD_DIM
    return pl.pallas_call(
        _gqa_ctx_kernel,
        grid=(n_batch, GQA_KV_HEADS),
        in_specs=[pl.BlockSpec((CTX_LEN, gw), lambda b, h: (b, h)),
                  pl.BlockSpec((CTX_LEN, dh), lambda b, h: (b, h)),
                  pl.BlockSpec((CTX_LEN, dh), lambda b, h: (b, C_GQA_V // dh + h))],
        out_specs=pl.BlockSpec((CTX_LEN, gw), lambda b, h: (b, h)),
        out_shape=jax.ShapeDtypeStruct((n_batch * CTX_LEN, GQA_HEADS * dh), BF),
        compiler_params=_cparams("parallel", "parallel"),
        name="gqa_context",
    )(q_c, k_c, proj_c)


def _log_sigmoid(z):
    return jnp.minimum(z, 0.0) - jnp.log1p(jnp.exp(-jnp.abs(z)))


def _gla_step(c, direction, q_s, k_s, v_s, low_s, wa_ref, ba_ref, st_ref, o_s):
    n = GLA_STEP
    r0 = pl.multiple_of(c * n, n)
    rows = pl.ds(r0, n)
    qf = q_s[rows, :].astype(F32) * (GLA_DK ** -0.5)
    kf = k_s[rows, :].astype(F32)
    v = v_s[rows, :]
    z = _dot(low_s[rows, :], wa_ref[direction].astype(BF)) + ba_ref[direction]
    g = _log_sigmoid(z) * (1.0 / GLA_TAU)

    ri = lax.broadcasted_iota(jnp.int32, (n, n), 0)
    ci = lax.broadcasted_iota(jnp.int32, (n, n), 1)
    tri = (ci <= ri) if direction == 0 else (ci >= ri)
    tri = jnp.where(tri, 1.0, 0.0).astype(BF)
    g_hi = g.astype(BF)
    g_lo = (g - g_hi.astype(F32)).astype(BF)
    bc = _dot(tri, g_hi) + _dot(tri, g_lo)
    btot = bc[n - 1:n] if direction == 0 else bc[0:1]

    state = st_ref[...]
    q_in = (qf * jnp.exp(bc)).astype(BF)
    o = _dot_nt(q_in, state.astype(BF))
    k_out = (kf * jnp.exp(btot - bc)).astype(BF)
    v_t = v.astype(F32).T.astype(BF)
    st_ref[...] = jnp.exp(btot) * state + _dot(v_t, k_out)

    row_id = lax.broadcasted_iota(jnp.int32, (n, 1), 0)
    lane = lax.broadcasted_iota(jnp.int32, (GLA_SUB, n), 1)
    sub_row = lax.broadcasted_iota(jnp.int32, (GLA_SUB, 1), 0)
    neg_inf = jnp.float32(-jnp.inf)
    blocks = []
    for blk in range(n // GLA_SUB):
        lo, hi = blk * GLA_SUB, (blk + 1) * GLA_SUB
        q_b, b_b = qf[lo:hi], bc[lo:hi]
        if direction == 0 and blk > 0:
            ref, earlier = bc[lo - 1:lo], row_id < lo
        elif direction == 1 and hi < n:
            ref, earlier = bc[hi:hi + 1], row_id >= hi
        else:
            ref = None
        if ref is None:
            a_b = jnp.zeros((GLA_SUB, n), F32)
        else:
            k_dec = (kf * jnp.exp(jnp.where(earlier, ref - bc, neg_inf))).astype(BF)
            a_b = _dot_nt((q_b * jnp.exp(b_b - ref)).astype(BF), k_dec)
        for jl in range(GLA_SUB):
            j = lo + jl
            keep = (sub_row >= jl) if direction == 0 else (sub_row <= jl)
            dec = jnp.exp(jnp.where(keep, b_b - bc[j:j + 1], neg_inf))
            col = jnp.sum(q_b * kf[j:j + 1] * dec, axis=-1, keepdims=True)
            a_b = jnp.where(lane == j, col, a_b)
        blocks.append(a_b)
    attn = jnp.concatenate(blocks, axis=0).astype(BF)
    o_s[rows, :] = o + _dot(attn, v)


def _gla_kernel(ql, kl, vl, ogl, lowl, qc, kc, vc, ogc, lowc, wa_ref, ba_ref, ng_ref,
                yl_ref, yc_ref, q_s, k_s, v_s, low_s, of_s, ob_s, stf, stb):
    nc = CTX_LEN
    q_s[0:nc, :] = qc[...]
    q_s[nc:, :] = ql[...]
    k_s[0:nc, :] = kc[...]
    k_s[nc:, :] = kl[...]
    v_s[0:nc, :] = vc[...]
    v_s[nc:, :] = vl[...]
    low_s[0:nc, :] = lowc[...]
    low_s[nc:, :] = lowl[...]
    stf[...] = jnp.zeros_like(stf)
    stb[...] = jnp.zeros_like(stb)
    n_ctx = CTX_LEN // GLA_STEP
    n_all = (CTX_LEN + SEQ) // GLA_STEP

    def body(i, carry):
        cb = jnp.where(i < n_ctx, n_ctx - 1 - i, n_all + n_ctx - 1 - i)
        _gla_step(i, 0, q_s, k_s, v_s, low_s, wa_ref, ba_ref, stf, of_s)
        _gla_step(cb, 1, q_s, k_s, v_s, low_s, wa_ref, ba_ref, stb, ob_s)
        return carry

    lax.fori_loop(0, n_all, body, 0)

    def finish(lo, hi, og_ref, y_ref):
        o = of_s[lo:hi, :] + ob_s[lo:hi, :]
        ms = jnp.mean(o * o, axis=-1, keepdims=True)
        y = o * lax.rsqrt(ms + NORM_EPS) * ng_ref[...]
        y_ref[...] = (y * _silu(og_ref[...].astype(F32))).astype(y_ref.dtype)

    finish(0, nc, ogc, yc_ref)
    finish(nc, nc + SEQ, ogl, yl_ref)


def gla_bidirectional(proj_l, proj_c, w_a2, b_a, norm_g, n_batch):
    dk, dv = GLA_DK, GLA_DV
    wa = jnp.zeros((2, GLA_HEADS, V7X_LANES, dk), F32)
    for d in range(2):
        wa = wa.at[d, :, d * GLA_GATE_RANK:(d + 1) * GLA_GATE_RANK, :].set(
            w_a2[d].reshape(GLA_GATE_RANK, GLA_HEADS, dk).transpose(1, 0, 2))
    ba = b_a.reshape(2, GLA_HEADS, 1, dk)

    def specs(rows):
        return [pl.BlockSpec((rows, dk), lambda b, h: (b, C_GLA_Q // dk + h)),
                pl.BlockSpec((rows, dk), lambda b, h: (b, C_GLA_K // dk + h)),
                pl.BlockSpec((rows, dv), lambda b, h: (b, C_GLA_V // dv + h)),
                pl.BlockSpec((rows, dv), lambda b, h: (b, C_GLA_OG // dv + h)),
                pl.BlockSpec((rows, V7X_LANES), lambda b, h: (b, C_LOW // V7X_LANES))]

    n_rows = CTX_LEN + SEQ
    return pl.pallas_call(
        _gla_kernel,
        grid=(n_batch, GLA_HEADS),
        in_specs=specs(SEQ) + specs(CTX_LEN) + [
            pl.BlockSpec((2, None, V7X_LANES, dk), lambda b, h: (0, h, 0, 0)),
            pl.BlockSpec((2, None, 1, dk), lambda b, h: (0, h, 0, 0)),
            pl.BlockSpec((1, dv), lambda b, h: (0, 0))],
        out_specs=[pl.BlockSpec((SEQ, dv), lambda b, h: (b, h)),
                   pl.BlockSpec((CTX_LEN, dv), lambda b, h: (b, h))],
        out_shape=[jax.ShapeDtypeStruct((n_batch * SEQ, GLA_HEADS * dv), BF),
                   jax.ShapeDtypeStruct((n_batch * CTX_LEN, GLA_HEADS * dv), BF)],
        scratch_shapes=[pltpu.VMEM((n_rows, dk), BF), pltpu.VMEM((n_rows, dk), BF),
                        pltpu.VMEM((n_rows, dv), BF), pltpu.VMEM((n_rows, V7X_LANES), BF),
                        pltpu.VMEM((n_rows, dv), F32), pltpu.VMEM((n_rows, dv), F32),
                        pltpu.VMEM((dv, dk), F32), pltpu.VMEM((dv, dk), F32)],
        compiler_params=_cparams("parallel", "parallel"),
        name="gla_bidirectional",
    )(*([proj_l] * 5 + [proj_c] * 5), wa, ba, norm_g.reshape(1, dv))


def _router_kernel(x_ref, g_ref, mod_ref, rw_ref, rb_ref, h_ref, idx_ref, wt_ref):
    h = _modulated_norm(x_ref[...], g_ref[...], mod_ref[...], 3, 4)
    h_ref[...] = h.astype(h_ref.dtype)
    logits = jnp.dot(h, rw_ref[...], preferred_element_type=F32,
                     precision=lax.Precision.HIGHEST) + rb_ref[...]
    lane = lax.broadcasted_iota(jnp.int32, logits.shape, 1)
    neg_inf = jnp.float32(-jnp.inf)
    logits = jnp.where(lane < N_EXPERTS, logits, neg_inf)
    m1 = jnp.max(logits, axis=-1, keepdims=True)
    lane_f = lane.astype(F32)
    i1 = jnp.min(jnp.where(logits == m1, lane_f, float(V7X_LANES)), axis=-1, keepdims=True)
    rest = jnp.where(lane_f == i1, neg_inf, logits)
    m2 = jnp.max(rest, axis=-1, keepdims=True)
    i2 = jnp.min(jnp.where(rest == m2, lane_f, float(V7X_LANES)), axis=-1, keepdims=True)
    e = jnp.exp(m2 - m1)
    w1 = 1.0 / (1.0 + e)
    idx_ref[...] = jnp.where(lane == 0, i1, jnp.where(lane == 1, i2, 0.0)).astype(jnp.int32)
    wt_ref[...] = jnp.where(lane == 0, w1, jnp.where(lane == 1, e * w1, 0.0))


def moe_router(x, g, mod, mod_row, router_w, router_b, tm=512):
    m = x.shape[0]
    rw = jnp.zeros((D_MODEL, V7X_LANES), F32).at[:, :N_EXPERTS].set(router_w)
    rb = jnp.zeros((1, V7X_LANES), F32).at[0, :N_EXPERTS].set(router_b)
    return pl.pallas_call(
        _router_kernel,
        grid=(m // tm,),
        in_specs=[pl.BlockSpec((tm, D_MODEL), lambda t: (t, 0)),
                  pl.BlockSpec((1, D_MODEL), lambda t: (0, 0)),
                  pl.BlockSpec((None, 6, D_MODEL), lambda t: (mod_row(t), 0, 0)),
                  pl.BlockSpec((D_MODEL, V7X_LANES), lambda t: (0, 0)),
                  pl.BlockSpec((1, V7X_LANES), lambda t: (0, 0))],
        out_specs=[pl.BlockSpec((tm, D_MODEL), lambda t: (t, 0)),
                   pl.BlockSpec((tm, V7X_LANES), lambda t: (t, 0)),
                   pl.BlockSpec((tm, V7X_LANES), lambda t: (t, 0))],
        out_shape=[jax.ShapeDtypeStruct((m, D_MODEL), BF),
                   jax.ShapeDtypeStruct((m, V7X_LANES), jnp.int32),
                   jax.ShapeDtypeStruct((m, V7X_LANES), F32)],
        compiler_params=_cparams("parallel"),
        name="moe_router",
    )(x, g.reshape(1, D_MODEL), mod, rw, rb)


def _moe_ffn_kernel(be_ref, nu_ref, x_ref, wg_ref, wu_ref, wd_ref, o_ref):
    b = pl.program_id(0)
    f = pl.program_id(1)

    @pl.when(f == 0)
    def _():
        o_ref[...] = jnp.zeros_like(o_ref)

    @pl.when(b < nu_ref[0])
    def _():
        x = x_ref[...]
        g = _dot(x, wg_ref[...].astype(BF))
        u = _dot(x, wu_ref[...].astype(BF))
        a = (_silu(g) * u).astype(BF)
        o_ref[...] += _dot(a, wd_ref[...].astype(BF))


def moe_expert_ffn(xb, block_e, n_used, w_gate, w_up, w_down):
    n_blk = xb.shape[0] // MOE_ROWS
    n_f = D_FF // MOE_FF_TILE

    def last_used(b, nu):
        return jnp.minimum(b, nu[0] - 1)

    def f_idx(b, f, nu):
        return jnp.where(b < nu[0], f, n_f - 1)

    grid_spec = pltpu.PrefetchScalarGridSpec(
        num_scalar_prefetch=2,
        grid=(n_blk, n_f),
        in_specs=[
            pl.BlockSpec((MOE_ROWS, D_MODEL), lambda b, f, be, nu: (last_used(b, nu), 0)),
            pl.BlockSpec((None, D_MODEL, MOE_FF_TILE),
                         lambda b, f, be, nu: (be[last_used(b, nu)], 0, f_idx(b, f, nu))),
            pl.BlockSpec((None, D_MODEL, MOE_FF_TILE),
                         lambda b, f, be, nu: (be[last_used(b, nu)], 0, f_idx(b, f, nu))),
            pl.BlockSpec((None, MOE_FF_TILE, D_MODEL),
                         lambda b, f, be, nu: (be[last_used(b, nu)], f_idx(b, f, nu), 0)),
        ],
        out_specs=pl.BlockSpec((MOE_ROWS, D_MODEL), lambda b, f, be, nu: (b, 0)),
    )
    return pl.pallas_call(
        _moe_ffn_kernel,
        grid_spec=grid_spec,
        out_shape=jax.ShapeDtypeStruct((n_blk * MOE_ROWS, D_MODEL), F32),
        compiler_params=_cparams("arbitrary", "arbitrary"),
        name="moe_expert_ffn",
    )(block_e, n_used, xb, w_gate, w_up, w_down)


def _moe_combine_kernel(x_ref, y1_ref, y2_ref, wt_ref, mod_ref, o_ref):
    w = wt_ref[...]
    y = w[:, 0:1] * y1_ref[...] + w[:, 1:2] * y2_ref[...]
    o_ref[...] = x_ref[...] + mod_ref[5:6, :] * y


def moe_combine(x, y1, y2, wt, mod, mod_row, tm=512):
    m = x.shape[0]
    row = pl.BlockSpec((tm, D_MODEL), lambda t: (t, 0))
    return pl.pallas_call(
        _moe_combine_kernel,
        grid=(m // tm,),
        in_specs=[row, row, row,
                  pl.BlockSpec((tm, V7X_LANES), lambda t: (t, 0)),
                  pl.BlockSpec((None, 6, D_MODEL), lambda t: (mod_row(t), 0, 0))],
        out_specs=row,
        out_shape=jax.ShapeDtypeStruct((m, D_MODEL), F32),
        compiler_params=_cparams("parallel"),
        name="moe_combine",
    )(x, y1, y2, wt, mod)


def moe_layout(top_i):
    t = top_i.shape[0]
    n_assign = t * TOP_K
    flat_e = top_i.reshape(-1)
    onehot = (flat_e[:, None] == jnp.arange(N_EXPERTS, dtype=jnp.int32)[None, :]).astype(jnp.int32)
    csum = jnp.cumsum(onehot, axis=0)
    rank = jnp.take_along_axis(csum, flat_e[:, None], axis=1)[:, 0] - 1
    counts = csum[-1]
    padded = (counts + MOE_ROWS - 1) // MOE_ROWS * MOE_ROWS
    pends = jnp.cumsum(padded)
    pstarts = pends - padded
    dest = pstarts[flat_e] + rank
    n_blk = n_assign // MOE_ROWS + N_EXPERTS
    flat_t = jnp.repeat(jnp.arange(t, dtype=jnp.int32), TOP_K)
    slot_t = jnp.zeros((n_blk * MOE_ROWS,), jnp.int32).at[dest].set(flat_t)
    block_e = jnp.minimum(
        jnp.searchsorted(pends, jnp.arange(n_blk, dtype=jnp.int32) * MOE_ROWS, side='right'),
        N_EXPERTS - 1).astype(jnp.int32)
    n_used = (pends[-1] // MOE_ROWS).astype(jnp.int32).reshape(1)
    return slot_t, dest.reshape(t, TOP_K), block_e, n_used


def moe_layer(x, norm_g, mod, mod_row, router_w, router_b, w_gate, w_up, w_down):
    h, idx, wt = moe_router(x, norm_g, mod, mod_row, router_w, router_b)
    slot_t, dest, block_e, n_used = moe_layout(idx[:, :TOP_K])
    xb = jnp.take(h, slot_t, axis=0)
    yb = moe_expert_ffn(xb, block_e, n_used, w_gate, w_up, w_down)
    y1 = jnp.take(yb, dest[:, 0], axis=0)
    y2 = jnp.take(yb, dest[:, 1], axis=0)
    return moe_combine(x, y1, y2, wt, mod, mod_row)


def pack_w_in(w):
    pad = jnp.zeros((D_MODEL, D_IN_PACKED - (R_END - R_GQA_Q) - R_LOW - 2 * GLA_GATE_RANK), w.dtype)
    return jnp.concatenate([w[:, :R_LOW], w[:, R_GQA_Q:R_END], w[:, R_LOW:R_GQA_Q], pad],
                           axis=1).astype(BF)


def kernel(x, c, ctx, c_ctx, w_ada, b_ada, norm1_g, norm2_g, w_in, na_rpb, gla_w_a2, gla_b_a, gla_norm_g, gqa_qn_g, gqa_kn_g, w_pa, w_pb, w_pc, w_out, dense_w_gate, dense_w_up, dense_w_down, router_w, router_b, moe_w_gate, moe_w_up, moe_w_down, final_norm_g):
    n_batch = x.shape[0]
    xl = x.reshape(n_batch * SEQ, D_MODEL)
    xc = ctx.reshape(n_batch * CTX_LEN, D_MODEL)
    cvec = jnp.zeros((8, D_MODEL), F32).at[:n_batch].set(c).at[n_batch].set(c_ctx)
    mods = ada_modulation(cvec, w_ada, b_ada)
    cos_t, sin_t = rope_tables()
    lat_row = _latent_mod_row
    ctx_row = _ctx_mod_row(n_batch)
    prep_tm = 256
    lat_table = lambda t: t % (SEQ // prep_tm)
    ctx_table = lambda t: SEQ // prep_tm

    for i in range(DEPTH):
        last = i == DEPTH - 1
        mod = mods[i]
        w_in_p = pack_w_in(w_in[i])
        hl = norm_mod(xl, norm1_g[i], mod, lat_row(512), 0, 1)
        hc = norm_mod(xc, norm1_g[i], mod, ctx_row, 0, 1)
        pl_ = matmul(hl, w_in_p, BF)
        pc_ = matmul(hc, w_in_p, BF)

        a_l = na_latent(pl_, pc_, na_bias_table(na_rpb[i]), n_batch)
        b_l, b_c = gla_bidirectional(pl_, pc_, gla_w_a2[i], gla_b_a[i], gla_norm_g[i], n_batch)
        q_l, k_l = gqa_prep(pl_, cos_t, sin_t, gqa_qn_g[i], gqa_kn_g[i], lat_table, prep_tm)
        q_c, k_c = gqa_prep(pc_, cos_t, sin_t, gqa_qn_g[i], gqa_kn_g[i], ctx_table, prep_tm)
        c_l = gqa_latent(q_l, k_l, k_c, pl_, pc_, n_batch)

        m_l = merge_branches(a_l, b_l, c_l, pl_, w_pa[i], w_pb[i], w_pc[i])
        xl = matmul_residual(m_l, w_out[i], xl, mod, lat_row(1024), 2)
        if not last:
            a_c = na_context(pc_, n_batch)
            c_c = gqa_context(q_c, k_c, pc_, n_batch)
            m_c = merge_branches(a_c, b_c, c_c, pc_, w_pa[i], w_pb[i], w_pc[i])
            xc = matmul_residual(m_c, w_out[i], xc, mod, ctx_row, 2)

        j = i // 2
        if i % 2 == 0:
            def ffn(xs, mod_row_tm):
                h2 = norm_mod(xs, norm2_g[i], mod, mod_row_tm(512), 3, 4)
                u = ffn_up(h2, dense_w_gate[j], dense_w_up[j])
                return matmul_residual(u, dense_w_down[j], xs, mod, mod_row_tm(1024), 5)
            xl = ffn(xl, lat_row)
            if not last:
                xc = ffn(xc, lambda tm: ctx_row)
        else:
            xl = moe_layer(xl, norm2_g[i], mod, lat_row(512), router_w[j], router_b[j],
                           moe_w_gate[j], moe_w_up[j], moe_w_down[j])
            if not last:
                xc = moe_layer(xc, norm2_g[i], mod, ctx_row, router_w[j], router_b[j],
                               moe_w_gate[j], moe_w_up[j], moe_w_down[j])
    return final_rmsnorm(xl, final_norm_g).reshape(n_batch, SEQ, D_MODEL)
```

```python
import functools

import jax
import jax.numpy as jnp
from jax import lax
from jax.experimental import pallas as pl
from jax.experimental.pallas import tpu as pltpu

BF = jnp.bfloat16
F32 = jnp.float32

D_MODEL = 2048
SEQ = 2048
CTX_LEN = 256
DEPTH = 2
GRID_W = 64
GRID_ROWS = SEQ // GRID_W
NA_HEADS = 16
NA_HEAD_DIM = 64
NA_WIN_ROWS = 8
NA_WIN_COLS = 16
GLA_HEADS = 4
GLA_DK = 128
GLA_DV = 256
GLA_GATE_RANK = 16
GLA_TAU = 16.0
GQA_HEADS = 8
GQA_KV_HEADS = 2
GQA_HEAD_DIM = 128
GQA_GROUP = GQA_HEADS // GQA_KV_HEADS
ROPE_THETA = 10000.0
D_FF = 5632
N_EXPERTS = 8
TOP_K = 2
NORM_EPS = 1e-6
BRANCH_W = 1024

C_NA_Q, C_NA_K, C_NA_V = 0, 1024, 2048
C_GLA_Q, C_GLA_K, C_GLA_V, C_GLA_OG = 3072, 3584, 4096, 5120
C_GQA_Q, C_GQA_K, C_GQA_V = 6144, 7168, 7424
C_GATE_A, C_GATE_B, C_GATE_C = 7680, 9728, 11776
C_LOW = 13824
D_IN_PACKED = 14336
R_LOW = 6144
R_GQA_Q = 6176
R_END = 13856

V7X_LANES = 128
V7X_VMEM_LIMIT_BYTES = 56 * 1024 * 1024

GLA_STEP = 128
GLA_SUB = 16
MOE_ROWS = 1024
MOE_SUB = 512
MOE_FF_TILE = 256
MOE_GATHER_ROWS = 256


def _cparams(*sem):
    return pltpu.CompilerParams(dimension_semantics=sem,
                                vmem_limit_bytes=V7X_VMEM_LIMIT_BYTES)


def _dot(a, b):
    return jnp.dot(a, b, preferred_element_type=F32)


def _dot_nt(a, b):
    return lax.dot_general(a, b, (((1,), (1,)), ((), ())), preferred_element_type=F32)


def _silu(x):
    return x * jax.nn.sigmoid(x)


def _latent_mod_row(tm):
    return lambda t: (t * tm) // SEQ


def _ctx_mod_row(n_batch):
    return lambda t: n_batch


def _ada_kernel(c_ref, w_ref, b_ref, o_ref):
    a = _silu(c_ref[...])
    o_ref[...] = _dot(a.astype(BF), w_ref[...].astype(BF)) + b_ref[...]


def ada_modulation(cvec, w_ada, b_ada):
    tn = 1024
    n6 = 6 * D_MODEL
    out = pl.pallas_call(
        _ada_kernel,
        grid=(DEPTH, n6 // tn),
        in_specs=[
            pl.BlockSpec((8, D_MODEL), lambda l, j: (0, 0)),
            pl.BlockSpec((None, D_MODEL, tn), lambda l, j: (l, 0, j)),
            pl.BlockSpec((None, 1, tn), lambda l, j: (l, 0, j)),
        ],
        out_specs=pl.BlockSpec((None, 8, tn), lambda l, j: (l, 0, j)),
        out_shape=jax.ShapeDtypeStruct((DEPTH, 8, n6), F32),
        compiler_params=_cparams("parallel", "parallel"),
        name="ada_modulation",
    )(cvec, w_ada, b_ada.reshape(DEPTH, 1, n6))
    return out.reshape(DEPTH, 8, 6, D_MODEL)


def _modulated_norm(x, g, mod, shift_idx, scale_idx):
    ms = jnp.mean(x * x, axis=-1, keepdims=True)
    y = x * lax.rsqrt(ms + NORM_EPS) * g
    return y * (1.0 + mod[scale_idx:scale_idx + 1]) + mod[shift_idx:shift_idx + 1]


def _norm_mod_kernel(x_ref, g_ref, mod_ref, o_ref, *, shift_idx, scale_idx):
    o_ref[...] = _modulated_norm(x_ref[...], g_ref[...], mod_ref[...],
                                 shift_idx, scale_idx).astype(o_ref.dtype)


def norm_mod(x, g, mod, mod_row, shift_idx, scale_idx, tm=512):
    m = x.shape[0]
    tm = min(tm, m)
    return pl.pallas_call(
        functools.partial(_norm_mod_kernel, shift_idx=shift_idx, scale_idx=scale_idx),
        grid=(m // tm,),
        in_specs=[
            pl.BlockSpec((tm, D_MODEL), lambda t: (t, 0)),
            pl.BlockSpec((1, D_MODEL), lambda t: (0, 0)),
            pl.BlockSpec((None, 6, D_MODEL), lambda t: (mod_row(t), 0, 0)),
        ],
        out_specs=pl.BlockSpec((tm, D_MODEL), lambda t: (t, 0)),
        out_shape=jax.ShapeDtypeStruct((m, D_MODEL), BF),
        compiler_params=_cparams("parallel"),
        name="norm_mod",
    )(x, g.reshape(1, D_MODEL), mod)


def _rmsnorm_kernel(x_ref, g_ref, o_ref):
    x = x_ref[...]
    ms = jnp.mean(x * x, axis=-1, keepdims=True)
    o_ref[...] = x * lax.rsqrt(ms + NORM_EPS) * g_ref[...]


def final_rmsnorm(x, g, tm=512):
    m = x.shape[0]
    return pl.pallas_call(
        _rmsnorm_kernel,
        grid=(m // tm,),
        in_specs=[pl.BlockSpec((tm, D_MODEL), lambda t: (t, 0)),
                  pl.BlockSpec((1, D_MODEL), lambda t: (0, 0))],
        out_specs=pl.BlockSpec((tm, D_MODEL), lambda t: (t, 0)),
        out_shape=jax.ShapeDtypeStruct((m, D_MODEL), F32),
        compiler_params=_cparams("parallel"),
        name="final_rmsnorm",
    )(x, g.reshape(1, D_MODEL))


def _mm_kernel(a_ref, w_ref, o_ref):
    o_ref[...] = _dot(a_ref[...], w_ref[...].astype(BF)).astype(o_ref.dtype)


def matmul(a, w, out_dtype, tm=1024, tn=512):
    m, k = a.shape
    n = w.shape[1]
    tm = min(tm, m)
    return pl.pallas_call(
        _mm_kernel,
        grid=(m // tm, n // tn),
        in_specs=[pl.BlockSpec((tm, k), lambda i, j: (i, 0)),
                  pl.BlockSpec((k, tn), lambda i, j: (0, j))],
        out_specs=pl.BlockSpec((tm, tn), lambda i, j: (i, j)),
        out_shape=jax.ShapeDtypeStruct((m, n), out_dtype),
        compiler_params=_cparams("parallel", "parallel"),
        name="matmul",
    )(a, w)


def _mm_res_kernel(a_ref, w_ref, x_ref, mod_ref, o_ref, *, gate_idx):
    y = _dot(a_ref[...], w_ref[...].astype(BF))
    o_ref[...] = x_ref[...] + mod_ref[gate_idx:gate_idx + 1, :] * y


def matmul_residual(a, w, layer, x, mod, mod_row, gate_idx, tm=1024, tn=256):
    m, k = a.shape
    n = w.shape[2]
    tm = min(tm, m)
    return pl.pallas_call(
        functools.partial(_mm_res_kernel, gate_idx=gate_idx),
        grid=(m // tm, n // tn),
        in_specs=[pl.BlockSpec((tm, k), lambda i, j: (i, 0)),
                  pl.BlockSpec((None, k, tn), lambda i, j: (layer, 0, j)),
                  pl.BlockSpec((tm, tn), lambda i, j: (i, j)),
                  pl.BlockSpec((None, 6, tn), lambda i, j: (mod_row(i), 0, j))],
        out_specs=pl.BlockSpec((tm, tn), lambda i, j: (i, j)),
        out_shape=jax.ShapeDtypeStruct((m, n), F32),
        compiler_params=_cparams("parallel", "parallel"),
        name="matmul_residual",
    )(a, w, x, mod)


def _merge_kernel(a_ref, b_ref, c_ref, ga_ref, gb_ref, gc_ref, wa_ref, wb_ref, wc_ref, o_ref):
    def branch(x_ref, g_ref, w_ref):
        return jax.nn.sigmoid(g_ref[...].astype(F32)) * _dot(x_ref[...], w_ref[...].astype(BF))

    o_ref[...] = (branch(a_ref, ga_ref, wa_ref) + branch(b_ref, gb_ref, wb_ref)
                  + branch(c_ref, gc_ref, wc_ref)).astype(o_ref.dtype)


def merge_branches(a, b, c, proj, w_pa, w_pb, w_pc, layer, tm=1024, tn=512):
    m = a.shape[0]
    tm = min(tm, m)
    x_spec = pl.BlockSpec((tm, BRANCH_W), lambda i, j: (i, 0))
    w_spec = pl.BlockSpec((None, BRANCH_W, tn), lambda i, j: (layer, 0, j))

    def gate_spec(col):
        return pl.BlockSpec((tm, tn), lambda i, j: (i, col // tn + j))

    return pl.pallas_call(
        _merge_kernel,
        grid=(m // tm, D_MODEL // tn),
        in_specs=[x_spec, x_spec, x_spec,
                  gate_spec(C_GATE_A), gate_spec(C_GATE_B), gate_spec(C_GATE_C),
                  w_spec, w_spec, w_spec],
        out_specs=pl.BlockSpec((tm, tn), lambda i, j: (i, j)),
        out_shape=jax.ShapeDtypeStruct((m, D_MODEL), BF),
        compiler_params=_cparams("parallel", "parallel"),
        name="merge_branches",
    )(a, b, c, proj, proj, proj, w_pa, w_pb, w_pc)


def _ffn_up_kernel(h_ref, wg_ref, wu_ref, o_ref):
    h = h_ref[...]
    g = _dot(h, wg_ref[...].astype(BF))
    u = _dot(h, wu_ref[...].astype(BF))
    o_ref[...] = (_silu(g) * u).astype(o_ref.dtype)


def ffn_up(h, w_gate, w_up, tm=1024, tn=512):
    m = h.shape[0]
    tm = min(tm, m)
    return pl.pallas_call(
        _ffn_up_kernel,
        grid=(m // tm, D_FF // tn),
        in_specs=[pl.BlockSpec((tm, D_MODEL), lambda i, j: (i, 0)),
                  pl.BlockSpec((D_MODEL, tn), lambda i, j: (0, j)),
                  pl.BlockSpec((D_MODEL, tn), lambda i, j: (0, j))],
        out_specs=pl.BlockSpec((tm, tn), lambda i, j: (i, j)),
        out_shape=jax.ShapeDtypeStruct((m, D_FF), BF),
        compiler_params=_cparams("parallel", "parallel"),
        name="ffn_up",
    )(h, w_gate, w_up)


def _gqa_prep_kernel(q_ref, k_ref, cos_ref, sin_ref, qg_ref, kg_ref, qo_ref, ko_ref):
    cos = cos_ref[...]
    sin = sin_ref[...]
    lane = lax.broadcasted_iota(jnp.int32, cos.shape, 1)
    even = (lane & 1) == 0

    def prep(x_ref, g_ref, o_ref, heads):
        for h in range(heads):
            sl = slice(h * GQA_HEAD_DIM, (h + 1) * GQA_HEAD_DIM)
            x = x_ref[:, sl].astype(F32)
            ms = jnp.mean(x * x, axis=-1, keepdims=True)
            y = x * lax.rsqrt(ms + NORM_EPS) * g_ref[...]
            swapped = jnp.where(even, pltpu.roll(y, GQA_HEAD_DIM - 1, 1), pltpu.roll(y, 1, 1))
            o_ref[:, sl] = (y * cos + swapped * sin).astype(o_ref.dtype)

    prep(q_ref, qg_ref, qo_ref, GQA_HEADS)
    prep(k_ref, kg_ref, ko_ref, GQA_KV_HEADS)


def gqa_prep(proj, cos_t, sin_t, qn_g, kn_g, table_block, tm=256):
    m = proj.shape[0]
    qw = GQA_HEADS * GQA_HEAD_DIM
    kw = GQA_KV_HEADS * GQA_HEAD_DIM
    return pl.pallas_call(
        _gqa_prep_kernel,
        grid=(m // tm,),
        in_specs=[pl.BlockSpec((tm, qw), lambda t: (t, C_GQA_Q // qw)),
                  pl.BlockSpec((tm, kw), lambda t: (t, C_GQA_K // kw)),
                  pl.BlockSpec((tm, GQA_HEAD_DIM), lambda t: (table_block(t), 0)),
                  pl.BlockSpec((tm, GQA_HEAD_DIM), lambda t: (table_block(t), 0)),
                  pl.BlockSpec((1, GQA_HEAD_DIM), lambda t: (0, 0)),
                  pl.BlockSpec((1, GQA_HEAD_DIM), lambda t: (0, 0))],
        out_specs=[pl.BlockSpec((tm, qw), lambda t: (t, 0)),
                   pl.BlockSpec((tm, kw), lambda t: (t, 0))],
        out_shape=[jax.ShapeDtypeStruct((m, qw), BF), jax.ShapeDtypeStruct((m, kw), BF)],
        compiler_params=_cparams("parallel"),
        name="gqa_prep",
    )(proj, proj, cos_t, sin_t, qn_g.reshape(1, -1), kn_g.reshape(1, -1))


def rope_tables():
    half = GQA_HEAD_DIM // 2
    freqs = ROPE_THETA ** (-jnp.arange(0, half, 2, dtype=F32) / half)
    t = jnp.arange(SEQ)
    row = (t // GRID_W).astype(F32)
    col = (t % GRID_W).astype(F32)
    ang = jnp.concatenate([row[:, None] * freqs, col[:, None] * freqs], axis=-1)
    cos = jnp.repeat(jnp.cos(ang), 2, axis=-1)
    sin = jnp.repeat(jnp.sin(ang), 2, axis=-1)
    sign = jnp.tile(jnp.array([-1.0, 1.0], F32), half)
    cos = jnp.concatenate([cos, jnp.ones((256, GQA_HEAD_DIM), F32)], axis=0)
    sin = jnp.concatenate([sin * sign, jnp.zeros((256, GQA_HEAD_DIM), F32)], axis=0)
    return cos, sin


def _attend(q, kv_list, scale):
    scores = []
    for k, _, bias in kv_list:
        s = _dot_nt(q, k) * scale
        if bias is not None:
            s = s + bias
        scores.append(s)
    m = functools.reduce(jnp.maximum, [jnp.max(s, axis=-1, keepdims=True) for s in scores])
    ps = [jnp.exp(s - m) for s in scores]
    den = functools.reduce(jnp.add, [jnp.sum(p, axis=-1, keepdims=True) for p in ps])
    o = functools.reduce(jnp.add, [_dot(p.astype(BF), v) for p, (_, v, _) in zip(ps, kv_list)])
    return o / den


def _head_pair_rows(q2):
    lane = lax.broadcasted_iota(jnp.int32, q2.shape, 1)
    first = lane < NA_HEAD_DIM
    zero = jnp.zeros_like(q2)
    return jnp.concatenate([jnp.where(first, q2, zero), jnp.where(first, zero, q2)], axis=0)


def _head_pair_merge(o, m):
    lane = lax.broadcasted_iota(jnp.int32, (m, 2 * NA_HEAD_DIM), 1)
    return jnp.where(lane < NA_HEAD_DIM, o[:m], o[m:])


def _na_kernel(q_ref, kl_ref, vl_ref, kc_ref, vc_ref, bias_ref, o_ref):
    r = pl.program_id(1)
    start = jnp.clip(r - NA_WIN_ROWS // 2, 0, GRID_ROWS - NA_WIN_ROWS) * GRID_W
    start = pl.multiple_of(start, GRID_W)
    n_loc = NA_WIN_ROWS * GRID_W
    scale = NA_HEAD_DIM ** -0.5
    for hp in range(NA_HEADS // 2):
        sl = slice(hp * 128, (hp + 1) * 128)
        qq = _head_pair_rows(q_ref[:, sl])
        bias = jnp.concatenate([bias_ref[2 * hp], bias_ref[2 * hp + 1]], axis=0)
        o = _attend(qq, [(kl_ref[pl.ds(start, n_loc), sl], vl_ref[pl.ds(start, n_loc), sl], bias),
                         (kc_ref[:, sl], vc_ref[:, sl], None)], scale)
        o_ref[:, sl] = _head_pair_merge(o, GRID_W).astype(o_ref.dtype)


def na_latent(proj_l, proj_c, bias_tab, n_batch):
    w = NA_HEADS * NA_HEAD_DIM

    def pattern(r):
        return jnp.where(r < 4, r, jnp.where(r > GRID_ROWS - 4, r - (GRID_ROWS - NA_WIN_ROWS), 4))

    return pl.pallas_call(
        _na_kernel,
        grid=(n_batch, GRID_ROWS),
        in_specs=[pl.BlockSpec((GRID_W, w), lambda b, r: (b * GRID_ROWS + r, C_NA_Q // w)),
                  pl.BlockSpec((SEQ, w), lambda b, r: (b, C_NA_K // w)),
                  pl.BlockSpec((SEQ, w), lambda b, r: (b, C_NA_V // w)),
                  pl.BlockSpec((CTX_LEN, w), lambda b, r: (b, C_NA_K // w)),
                  pl.BlockSpec((CTX_LEN, w), lambda b, r: (b, C_NA_V // w)),
                  pl.BlockSpec((None, NA_HEADS, GRID_W, NA_WIN_ROWS * GRID_W),
                               lambda b, r: (pattern(r), 0, 0, 0))],
        out_specs=pl.BlockSpec((GRID_W, w), lambda b, r: (b * GRID_ROWS + r, 0)),
        out_shape=jax.ShapeDtypeStruct((n_batch * SEQ, w), BF),
        compiler_params=_cparams("parallel", "arbitrary"),
        name="na_latent",
    )(proj_l, proj_l, proj_l, proj_c, proj_c, bias_tab)


def na_bias_table(rpb):
    cols = jnp.arange(GRID_W)
    col_start = jnp.clip(cols - NA_WIN_COLS // 2, 0, GRID_W - NA_WIN_COLS)
    in_win = (cols[None, :] >= col_start[:, None]) & (cols[None, :] < col_start[:, None] + NA_WIN_COLS)
    dc = jnp.clip(cols[None, :] - cols[:, None] + NA_WIN_COLS - 1, 0, 2 * NA_WIN_COLS - 2)
    dr = jnp.arange(NA_WIN_ROWS)[None, :] - jnp.arange(NA_WIN_ROWS)[:, None]
    bias = rpb[:, (dr + NA_WIN_ROWS - 1)[:, :, None, None], dc[None, None]]
    bias = bias.astype(F32).transpose(1, 0, 3, 2, 4)
    bias = jnp.where(in_win[:, None, :], bias, -jnp.inf)
    return bias.reshape(NA_WIN_ROWS, NA_HEADS, GRID_W, NA_WIN_ROWS * GRID_W)


def _na_ctx_kernel(q_ref, k_ref, v_ref, o_ref):
    scale = NA_HEAD_DIM ** -0.5
    for hp in range(NA_HEADS // 2):
        sl = slice(hp * 128, (hp + 1) * 128)
        o = _attend(_head_pair_rows(q_ref[:, sl]), [(k_ref[:, sl], v_ref[:, sl], None)], scale)
        o_ref[:, sl] = _head_pair_merge(o, CTX_LEN).astype(o_ref.dtype)


def na_context(proj_c, n_batch):
    w = NA_HEADS * NA_HEAD_DIM
    return pl.pallas_call(
        _na_ctx_kernel,
        grid=(n_batch,),
        in_specs=[pl.BlockSpec((CTX_LEN, w), lambda b: (b, C_NA_Q // w)),
                  pl.BlockSpec((CTX_LEN, w), lambda b: (b, C_NA_K // w)),
                  pl.BlockSpec((CTX_LEN, w), lambda b: (b, C_NA_V // w))],
        out_specs=pl.BlockSpec((CTX_LEN, w), lambda b: (b, 0)),
        out_shape=jax.ShapeDtypeStruct((n_batch * CTX_LEN, w), BF),
        compiler_params=_cparams("parallel"),
        name="na_context",
    )(proj_c, proj_c, proj_c)


def _gqa_rows(q_ref):
    return jnp.concatenate([q_ref[:, g * GQA_HEAD_DIM:(g + 1) * GQA_HEAD_DIM]
                            for g in range(GQA_GROUP)], axis=0)


def _gqa_store(o, o_ref):
    tq = o_ref.shape[0]
    for g in range(GQA_GROUP):
        o_ref[:, g * GQA_HEAD_DIM:(g + 1) * GQA_HEAD_DIM] = o[g * tq:(g + 1) * tq].astype(o_ref.dtype)


def _gqa_kernel(q_ref, kl_ref, vl_ref, kc_ref, vc_ref, o_ref):
    for g in range(GQA_GROUP):
        sl = slice(g * GQA_HEAD_DIM, (g + 1) * GQA_HEAD_DIM)
        o = _attend(q_ref[:, sl], [(kl_ref[...], vl_ref[...], None), (kc_ref[...], vc_ref[...], None)],
                    GQA_HEAD_DIM ** -0.5)
        o_ref[:, sl] = o.astype(o_ref.dtype)


def _gqa_ctx_kernel(q_ref, kc_ref, vc_ref, o_ref):
    o = _attend(_gqa_rows(q_ref), [(kc_ref[...], vc_ref[...], None)], GQA_HEAD_DIM ** -0.5)
    _gqa_store(o, o_ref)


def gqa_latent(q_l, k_l, k_c, proj_l, proj_c, n_batch, tq=256):
    gw = GQA_GROUP * GQA_HEAD_DIM
    nq = SEQ // tq
    dh = GQA_HEAD_DIM
    return pl.pallas_call(
        _gqa_kernel,
        grid=(n_batch, GQA_KV_HEADS, nq),
        in_specs=[pl.BlockSpec((tq, gw), lambda b, h, i: (b * nq + i, h)),
                  pl.BlockSpec((SEQ, dh), lambda b, h, i: (b, h)),
                  pl.BlockSpec((SEQ, dh), lambda b, h, i: (b, C_GQA_V // dh + h)),
                  pl.BlockSpec((CTX_LEN, dh), lambda b, h, i: (b, h)),
                  pl.BlockSpec((CTX_LEN, dh), lambda b, h, i: (b, C_GQA_V // dh + h))],
        out_specs=pl.BlockSpec((tq, gw), lambda b, h, i: (b * nq + i, h)),
        out_shape=jax.ShapeDtypeStruct((n_batch * SEQ, GQA_HEADS * dh), BF),
        compiler_params=_cparams("parallel", "parallel", "arbitrary"),
        name="gqa_latent",
    )(q_l, k_l, proj_l, k_c, proj_c)


def gqa_context(q_c, k_c, proj_c, n_batch):
    gw = GQA_GROUP * GQA_HEAD_DIM
    dh = GQA_HEAD_DIM
    return pl.pallas_call(
        _gqa_ctx_kernel,
        grid=(n_batch, GQA_KV_HEADS),
        in_specs=[pl.BlockSpec((CTX_LEN, gw), lambda b, h: (b, h)),
                  pl.BlockSpec((CTX_LEN, dh), lambda b, h: (b, h)),
                  pl.BlockSpec((CTX_LEN, dh), lambda b, h: (b, C_GQA_V // dh + h))],
        out_specs=pl.BlockSpec((CTX_LEN, gw), lambda b, h: (b, h)),
        out_shape=jax.ShapeDtypeStruct((n_batch * CTX_LEN, GQA_HEADS * dh), BF),
        compiler_params=_cparams("parallel", "parallel"),
        name="gqa_context",
    )(q_c, k_c, proj_c)


def _log_sigmoid(z):
    return jnp.minimum(z, 0.0) - jnp.log1p(jnp.exp(-jnp.abs(z)))


def _gla_step(c, direction, q_s, k_s, v_s, low_s, wa_ref, ba_ref, st_ref, o_s):
    n = GLA_STEP
    r0 = pl.multiple_of(c * n, n)
    rows = pl.ds(r0, n)
    qf = q_s[rows, :].astype(F32) * (GLA_DK ** -0.5)
    kf = k_s[rows, :].astype(F32)
    v = v_s[rows, :]
    z = _dot(low_s[rows, :], wa_ref[direction].astype(BF)) + ba_ref[direction]
    g = _log_sigmoid(z) * (1.0 / GLA_TAU)

    ri = lax.broadcasted_iota(jnp.int32, (n, n), 0)
    ci = lax.broadcasted_iota(jnp.int32, (n, n), 1)
    tri = (ci <= ri) if direction == 0 else (ci >= ri)
    tri = jnp.where(tri, 1.0, 0.0).astype(BF)
    g_hi = g.astype(BF)
    g_lo = (g - g_hi.astype(F32)).astype(BF)
    bc = _dot(tri, g_hi) + _dot(tri, g_lo)
    btot = bc[n - 1:n] if direction == 0 else bc[0:1]

    state = st_ref[...]
    q_in = (qf * jnp.exp(bc)).astype(BF)
    o = _dot_nt(q_in, state.astype(BF))
    k_out = (kf * jnp.exp(btot - bc)).astype(BF)
    v_t = v.astype(F32).T.astype(BF)
    st_ref[...] = jnp.exp(btot) * state + _dot(v_t, k_out)

    row_id = lax.broadcasted_iota(jnp.int32, (n, 1), 0)
    lane = lax.broadcasted_iota(jnp.int32, (GLA_SUB, n), 1)
    sub_row = lax.broadcasted_iota(jnp.int32, (GLA_SUB, 1), 0)
    neg_inf = jnp.float32(-jnp.inf)
    blocks = []
    for blk in range(n // GLA_SUB):
        lo, hi = blk * GLA_SUB, (blk + 1) * GLA_SUB
        q_b, b_b = qf[lo:hi], bc[lo:hi]
        if direction == 0 and blk > 0:
            ref, earlier = bc[lo - 1:lo], row_id < lo
        elif direction == 1 and hi < n:
            ref, earlier = bc[hi:hi + 1], row_id >= hi
        else:
            ref = None
        if ref is None:
            a_b = jnp.zeros((GLA_SUB, n), F32)
        else:
            k_dec = (kf * jnp.exp(jnp.where(earlier, ref - bc, neg_inf))).astype(BF)
            a_b = _dot_nt((q_b * jnp.exp(b_b - ref)).astype(BF), k_dec)
        for jl in range(GLA_SUB):
            j = lo + jl
            keep = (sub_row >= jl) if direction == 0 else (sub_row <= jl)
            dec = jnp.exp(jnp.where(keep, b_b - bc[j:j + 1], neg_inf))
            col = jnp.sum(q_b * kf[j:j + 1] * dec, axis=-1, keepdims=True)
            a_b = jnp.where(lane == j, col, a_b)
        blocks.append(a_b)
    attn = jnp.concatenate(blocks, axis=0).astype(BF)
    o_s[rows, :] = o + _dot(attn, v)


def _gla_kernel(ql, kl, vl, ogl, lowl, qc, kc, vc, ogc, lowc, wa_ref, ba_ref, ng_ref,
                yl_ref, yc_ref, q_s, k_s, v_s, low_s, of_s, ob_s, stf, stb):
    nc = CTX_LEN
    q_s[0:nc, :] = qc[...]
    q_s[nc:, :] = ql[...]
    k_s[0:nc, :] = kc[...]
    k_s[nc:, :] = kl[...]
    v_s[0:nc, :] = vc[...]
    v_s[nc:, :] = vl[...]
    low_s[0:nc, :] = lowc[...]
    low_s[nc:, :] = lowl[...]
    stf[...] = jnp.zeros_like(stf)
    stb[...] = jnp.zeros_like(stb)
    n_ctx = CTX_LEN // GLA_STEP
    n_all = (CTX_LEN + SEQ) // GLA_STEP

    def body(i, carry):
        cb = jnp.where(i < n_ctx, n_ctx - 1 - i, n_all + n_ctx - 1 - i)
        _gla_step(i, 0, q_s, k_s, v_s, low_s, wa_ref, ba_ref, stf, of_s)
        _gla_step(cb, 1, q_s, k_s, v_s, low_s, wa_ref, ba_ref, stb, ob_s)
        return carry

    lax.fori_loop(0, n_all, body, 0)

    def finish(lo, hi, og_ref, y_ref):
        o = of_s[lo:hi, :] + ob_s[lo:hi, :]
        ms = jnp.mean(o * o, axis=-1, keepdims=True)
        y = o * lax.rsqrt(ms + NORM_EPS) * ng_ref[...]
        y_ref[...] = (y * _silu(og_ref[...].astype(F32))).astype(y_ref.dtype)

    finish(0, nc, ogc, yc_ref)
    finish(nc, nc + SEQ, ogl, yl_ref)


def gla_bidirectional(proj_l, proj_c, w_a2, b_a, norm_g, n_batch):
    dk, dv = GLA_DK, GLA_DV
    wa = jnp.zeros((2, GLA_HEADS, V7X_LANES, dk), F32)
    for d in range(2):
        wa = wa.at[d, :, d * GLA_GATE_RANK:(d + 1) * GLA_GATE_RANK, :].set(
            w_a2[d].reshape(GLA_GATE_RANK, GLA_HEADS, dk).transpose(1, 0, 2))
    ba = b_a.reshape(2, GLA_HEADS, 1, dk)

    def specs(rows):
        return [pl.BlockSpec((rows, dk), lambda b, h: (b, C_GLA_Q // dk + h)),
                pl.BlockSpec((rows, dk), lambda b, h: (b, C_GLA_K // dk + h)),
                pl.BlockSpec((rows, dv), lambda b, h: (b, C_GLA_V // dv + h)),
                pl.BlockSpec((rows, dv), lambda b, h: (b, C_GLA_OG // dv + h)),
                pl.BlockSpec((rows, V7X_LANES), lambda b, h: (b, C_LOW // V7X_LANES))]

    n_rows = CTX_LEN + SEQ
    return pl.pallas_call(
        _gla_kernel,
        grid=(n_batch, GLA_HEADS),
        in_specs=specs(SEQ) + specs(CTX_LEN) + [
            pl.BlockSpec((2, None, V7X_LANES, dk), lambda b, h: (0, h, 0, 0)),
            pl.BlockSpec((2, None, 1, dk), lambda b, h: (0, h, 0, 0)),
            pl.BlockSpec((1, dv), lambda b, h: (0, 0))],
        out_specs=[pl.BlockSpec((SEQ, dv), lambda b, h: (b, h)),
                   pl.BlockSpec((CTX_LEN, dv), lambda b, h: (b, h))],
        out_shape=[jax.ShapeDtypeStruct((n_batch * SEQ, GLA_HEADS * dv), BF),
                   jax.ShapeDtypeStruct((n_batch * CTX_LEN, GLA_HEADS * dv), BF)],
        scratch_shapes=[pltpu.VMEM((n_rows, dk), BF), pltpu.VMEM((n_rows, dk), BF),
                        pltpu.VMEM((n_rows, dv), BF), pltpu.VMEM((n_rows, V7X_LANES), BF),
                        pltpu.VMEM((n_rows, dv), F32), pltpu.VMEM((n_rows, dv), F32),
                        pltpu.VMEM((dv, dk), F32), pltpu.VMEM((dv, dk), F32)],
        compiler_params=_cparams("parallel", "parallel"),
        name="gla_bidirectional",
    )(*([proj_l] * 5 + [proj_c] * 5), wa, ba, norm_g.reshape(1, dv))


def _router_kernel(x_ref, g_ref, mod_ref, rw_ref, rb_ref, h_ref, idx_ref, wt_ref):
    h = _modulated_norm(x_ref[...], g_ref[...], mod_ref[...], 3, 4)
    h_ref[...] = h.astype(h_ref.dtype)
    logits = jnp.dot(h, rw_ref[...], preferred_element_type=F32,
                     precision=lax.Precision.HIGHEST) + rb_ref[...]
    lane = lax.broadcasted_iota(jnp.int32, logits.shape, 1)
    neg_inf = jnp.float32(-jnp.inf)
    logits = jnp.where(lane < N_EXPERTS, logits, neg_inf)
    m1 = jnp.max(logits, axis=-1, keepdims=True)
    lane_f = lane.astype(F32)
    i1 = jnp.min(jnp.where(logits == m1, lane_f, float(V7X_LANES)), axis=-1, keepdims=True)
    rest = jnp.where(lane_f == i1, neg_inf, logits)
    m2 = jnp.max(rest, axis=-1, keepdims=True)
    i2 = jnp.min(jnp.where(rest == m2, lane_f, float(V7X_LANES)), axis=-1, keepdims=True)
    e = jnp.exp(m2 - m1)
    w1 = 1.0 / (1.0 + e)
    idx_ref[...] = jnp.where(lane == 0, i1, jnp.where(lane == 1, i2, 0.0)).astype(jnp.int32)
    wt_ref[...] = jnp.where(lane == 0, w1, jnp.where(lane == 1, e * w1, 0.0))


def moe_router(x, g, mod, mod_row, router_w, router_b, tm=512):
    m = x.shape[0]
    rw = jnp.zeros((D_MODEL, V7X_LANES), F32).at[:, :N_EXPERTS].set(router_w)
    rb = jnp.zeros((1, V7X_LANES), F32).at[0, :N_EXPERTS].set(router_b)
    return pl.pallas_call(
        _router_kernel,
        grid=(m // tm,),
        in_specs=[pl.BlockSpec((tm, D_MODEL), lambda t: (t, 0)),
                  pl.BlockSpec((1, D_MODEL), lambda t: (0, 0)),
                  pl.BlockSpec((None, 6, D_MODEL), lambda t: (mod_row(t), 0, 0)),
                  pl.BlockSpec((D_MODEL, V7X_LANES), lambda t: (0, 0)),
                  pl.BlockSpec((1, V7X_LANES), lambda t: (0, 0))],
        out_specs=[pl.BlockSpec((tm, D_MODEL), lambda t: (t, 0)),
                   pl.BlockSpec((tm, V7X_LANES), lambda t: (t, 0)),
                   pl.BlockSpec((tm, V7X_LANES), lambda t: (t, 0))],
        out_shape=[jax.ShapeDtypeStruct((m, D_MODEL), F32),
                   jax.ShapeDtypeStruct((m, V7X_LANES), jnp.int32),
                   jax.ShapeDtypeStruct((m, V7X_LANES), F32)],
        compiler_params=_cparams("parallel"),
        name="moe_router",
    )(x, g.reshape(1, D_MODEL), mod, rw, rb)


def _row_copy(src_hbm, src_row, dst, dst_row, sem):
    return pltpu.make_async_copy(src_hbm.at[pl.ds(src_row, 1)], dst.at[pl.ds(dst_row, 1)], sem)


def _dispatch_kernel(idx_ref, h_hbm, o_ref, buf, sem):
    n = buf.shape[0]

    def start(i, carry):
        _row_copy(h_hbm, idx_ref[0, i], buf, i, sem).start()
        return carry

    def wait(i, carry):
        _row_copy(h_hbm, idx_ref[0, i], buf, i, sem).wait()
        return carry

    lax.fori_loop(0, n, start, 0)
    lax.fori_loop(0, n, wait, 0)
    o_ref[...] = buf[...].astype(o_ref.dtype)


def moe_dispatch(h, slot_t):
    n_slots = slot_t.shape[0]
    rows = MOE_GATHER_ROWS
    return pl.pallas_call(
        _dispatch_kernel,
        grid=(n_slots // rows,),
        in_specs=[pl.BlockSpec((None, 1, rows), lambda t: (t, 0, 0), memory_space=pltpu.SMEM),
                  pl.BlockSpec(memory_space=pl.ANY)],
        out_specs=pl.BlockSpec((rows, D_MODEL), lambda t: (t, 0)),
        out_shape=jax.ShapeDtypeStruct((n_slots, D_MODEL), BF),
        scratch_shapes=[pltpu.VMEM((rows, D_MODEL), F32), pltpu.SemaphoreType.DMA(())],
        compiler_params=_cparams("arbitrary"),
        name="moe_dispatch",
    )(slot_t.reshape(n_slots // rows, 1, rows), h)


def _moe_ffn_kernel(be_ref, nv_ref, nu_ref, x_ref, wg_ref, wu_ref, wd_ref, o_ref):
    b = pl.program_id(0)
    f = pl.program_id(1)

    @pl.when(f == 0)
    def _():
        o_ref[...] = jnp.zeros_like(o_ref)

    wg = wg_ref[...].astype(BF)
    wu = wu_ref[...].astype(BF)
    wd = wd_ref[...].astype(BF)
    for s in range(MOE_ROWS // MOE_SUB):
        rows = slice(s * MOE_SUB, (s + 1) * MOE_SUB)

        @pl.when(s * MOE_SUB < nv_ref[b])
        def _():
            x = x_ref[rows, :]
            a = (_silu(_dot(x, wg)) * _dot(x, wu)).astype(BF)
            o_ref[rows, :] += _dot(a, wd)


def moe_expert_ffn(xb, block_e, n_valid, n_used, w_gate, w_up, w_down):
    n_blk = xb.shape[0] // MOE_ROWS
    n_f = D_FF // MOE_FF_TILE

    def last_used(b, nu):
        return jnp.minimum(b, nu[0] - 1)

    def f_idx(b, f, nu):
        return jnp.where(b < nu[0], f, n_f - 1)

    grid_spec = pltpu.PrefetchScalarGridSpec(
        num_scalar_prefetch=3,
        grid=(n_blk, n_f),
        in_specs=[
            pl.BlockSpec((MOE_ROWS, D_MODEL), lambda b, f, be, nv, nu: (last_used(b, nu), 0)),
            pl.BlockSpec((None, D_MODEL, MOE_FF_TILE),
                         lambda b, f, be, nv, nu: (be[last_used(b, nu)], 0, f_idx(b, f, nu))),
            pl.BlockSpec((None, D_MODEL, MOE_FF_TILE),
                         lambda b, f, be, nv, nu: (be[last_used(b, nu)], 0, f_idx(b, f, nu))),
            pl.BlockSpec((None, MOE_FF_TILE, D_MODEL),
                         lambda b, f, be, nv, nu: (be[last_used(b, nu)], f_idx(b, f, nu), 0)),
        ],
        out_specs=pl.BlockSpec((MOE_ROWS, D_MODEL), lambda b, f, be, nv, nu: (b, 0)),
    )
    return pl.pallas_call(
        _moe_ffn_kernel,
        grid_spec=grid_spec,
        out_shape=jax.ShapeDtypeStruct((n_blk * MOE_ROWS, D_MODEL), F32),
        compiler_params=_cparams("arbitrary", "arbitrary"),
        name="moe_expert_ffn",
    )(block_e, n_valid, n_used, xb, w_gate, w_up, w_down)


def _moe_combine_kernel(idx_ref, x_ref, wt_ref, mod_ref, g_ref, yb_hbm, o_ref, buf, sem, *, final_norm):
    tm = x_ref.shape[0]

    def copies(i):
        return [_row_copy(yb_hbm, idx_ref[0, TOP_K * i + k], buf.at[k], i, sem) for k in range(TOP_K)]

    def start(i, carry):
        for cp in copies(i):
            cp.start()
        return carry

    def wait(i, carry):
        for cp in copies(i):
            cp.wait()
        return carry

    lax.fori_loop(0, tm, start, 0)
    lax.fori_loop(0, tm, wait, 0)
    w = wt_ref[...]
    y = w[:, 0:1] * buf[0] + w[:, 1:2] * buf[1]
    out = x_ref[...] + mod_ref[5:6, :] * y
    if final_norm:
        ms = jnp.mean(out * out, axis=-1, keepdims=True)
        out = out * lax.rsqrt(ms + NORM_EPS) * g_ref[...]
    o_ref[...] = out


def moe_combine(x, yb, dest, wt, mod, mod_row, final_g, tm=256):
    m = x.shape[0]
    tm = min(tm, m)
    row = pl.BlockSpec((tm, D_MODEL), lambda t: (t, 0))
    final_norm = final_g is not None
    g = final_g if final_norm else jnp.ones((D_MODEL,), F32)
    return pl.pallas_call(
        functools.partial(_moe_combine_kernel, final_norm=final_norm),
        grid=(m // tm,),
        in_specs=[pl.BlockSpec((None, 1, TOP_K * tm), lambda t: (t, 0, 0), memory_space=pltpu.SMEM),
                  row,
                  pl.BlockSpec((tm, V7X_LANES), lambda t: (t, 0)),
                  pl.BlockSpec((None, 6, D_MODEL), lambda t: (mod_row(t), 0, 0)),
                  pl.BlockSpec((1, D_MODEL), lambda t: (0, 0)),
                  pl.BlockSpec(memory_space=pl.ANY)],
        out_specs=row,
        out_shape=jax.ShapeDtypeStruct((m, D_MODEL), F32),
        scratch_shapes=[pltpu.VMEM((TOP_K, tm, D_MODEL), F32), pltpu.SemaphoreType.DMA(())],
        compiler_params=_cparams("arbitrary"),
        name="moe_combine",
    )(dest.reshape(m // tm, 1, TOP_K * tm), x, wt, mod, g.reshape(1, D_MODEL), yb)


def moe_layout(top_i):
    t = top_i.shape[0]
    n_assign = t * TOP_K
    flat_e = top_i.reshape(-1)
    onehot = (flat_e[:, None] == jnp.arange(N_EXPERTS, dtype=jnp.int32)[None, :]).astype(jnp.int32)
    csum = jnp.cumsum(onehot, axis=0)
    rank = jnp.take_along_axis(csum, flat_e[:, None], axis=1)[:, 0] - 1
    counts = csum[-1]
    padded = (counts + MOE_ROWS - 1) // MOE_ROWS * MOE_ROWS
    pends = jnp.cumsum(padded)
    pstarts = pends - padded
    dest = pstarts[flat_e] + rank
    n_blk = n_assign // MOE_ROWS + N_EXPERTS
    flat_t = jnp.repeat(jnp.arange(t, dtype=jnp.int32), TOP_K)
    slot_t = jnp.zeros((n_blk * MOE_ROWS,), jnp.int32).at[dest].set(flat_t)
    blk = jnp.arange(n_blk, dtype=jnp.int32)
    blk_start = blk * MOE_ROWS
    block_e = jnp.minimum(jnp.sum((pends[None, :] <= blk_start[:, None]).astype(jnp.int32), axis=1),
                          N_EXPERTS - 1)
    n_used = pends[-1] // MOE_ROWS
    n_valid = jnp.clip(counts[block_e] - (blk_start - pstarts[block_e]), 0, MOE_ROWS)
    n_valid = jnp.where(blk < n_used, n_valid, 0)
    return (slot_t, dest.reshape(t, TOP_K).astype(jnp.int32), block_e.astype(jnp.int32),
            n_valid.astype(jnp.int32), n_used.astype(jnp.int32).reshape(1))


def moe_layer(x, norm_g, mod, mod_row_tm, router_w, router_b, w_gate, w_up, w_down, final_g):
    router_tm, combine_tm = 512, 256
    h, idx, wt = moe_router(x, norm_g, mod, mod_row_tm(router_tm), router_w, router_b, router_tm)
    slot_t, dest, block_e, n_valid, n_used = moe_layout(idx[:, :TOP_K])
    xb = moe_dispatch(h, slot_t)
    yb = moe_expert_ffn(xb, block_e, n_valid, n_used, w_gate, w_up, w_down)
    return moe_combine(x, yb, dest, wt, mod, mod_row_tm(combine_tm), final_g, combine_tm)


def pack_w_in(w):
    pad = jnp.zeros((D_MODEL, D_IN_PACKED - (R_END - R_GQA_Q) - R_LOW - 2 * GLA_GATE_RANK), w.dtype)
    return jnp.concatenate([w[:, :R_LOW], w[:, R_GQA_Q:R_END], w[:, R_LOW:R_GQA_Q], pad],
                           axis=1).astype(BF)


def kernel(x, c, ctx, c_ctx, w_ada, b_ada, norm1_g, norm2_g, w_in, na_rpb, gla_w_a2, gla_b_a, gla_norm_g, gqa_qn_g, gqa_kn_g, w_pa, w_pb, w_pc, w_out, dense_w_gate, dense_w_up, dense_w_down, router_w, router_b, moe_w_gate, moe_w_up, moe_w_down, final_norm_g):
    n_batch = x.shape[0]
    xl = x.reshape(n_batch * SEQ, D_MODEL)
    xc = ctx.reshape(n_batch * CTX_LEN, D_MODEL)
    cvec = jnp.zeros((8, D_MODEL), F32).at[:n_batch].set(c).at[n_batch].set(c_ctx)
    mods = ada_modulation(cvec, w_ada, b_ada)
    cos_t, sin_t = rope_tables()
    lat_row = _latent_mod_row
    ctx_row = _ctx_mod_row(n_batch)
    prep_tm = 256
    lat_table = lambda t: t % (SEQ // prep_tm)
    ctx_table = lambda t: SEQ // prep_tm

    for i in range(DEPTH):
        last = i == DEPTH - 1
        mod = mods[i]
        w_in_p = pack_w_in(w_in[i])
        hl = norm_mod(xl, norm1_g[i], mod, lat_row(512), 0, 1)
        hc = norm_mod(xc, norm1_g[i], mod, ctx_row, 0, 1)
        pl_ = matmul(hl, w_in_p, BF)
        pc_ = matmul(hc, w_in_p, BF)

        a_l = na_latent(pl_, pc_, na_bias_table(na_rpb[i]), n_batch)
        b_l, b_c = gla_bidirectional(pl_, pc_, gla_w_a2[i], gla_b_a[i], gla_norm_g[i], n_batch)
        q_l, k_l = gqa_prep(pl_, cos_t, sin_t, gqa_qn_g[i], gqa_kn_g[i], lat_table, prep_tm)
        q_c, k_c = gqa_prep(pc_, cos_t, sin_t, gqa_qn_g[i], gqa_kn_g[i], ctx_table, prep_tm)
        c_l = gqa_latent(q_l, k_l, k_c, pl_, pc_, n_batch)

        m_l = merge_branches(a_l, b_l, c_l, pl_, w_pa, w_pb, w_pc, i)
        xl = matmul_residual(m_l, w_out, i, xl, mod, lat_row(1024), 2)
        if not last:
            a_c = na_context(pc_, n_batch)
            c_c = gqa_context(q_c, k_c, pc_, n_batch)
            m_c = merge_branches(a_c, b_c, c_c, pc_, w_pa, w_pb, w_pc, i)
            xc = matmul_residual(m_c, w_out, i, xc, mod, ctx_row, 2)

        j = i // 2
        if i % 2 == 0:
            def ffn(xs, mod_row_tm):
                h2 = norm_mod(xs, norm2_g[i], mod, mod_row_tm(512), 3, 4)
                u = ffn_up(h2, dense_w_gate[j], dense_w_up[j])
                return matmul_residual(u, dense_w_down, j, xs, mod, mod_row_tm(1024), 5)
            xl = ffn(xl, lat_row)
            if not last:
                xc = ffn(xc, lambda tm: ctx_row)
        else:
            fin = final_norm_g if last else None
            xl = moe_layer(xl, norm2_g[i], mod, lat_row, router_w[j], router_b[j],
                           moe_w_gate[j], moe_w_up[j], moe_w_down[j], fin)
            if not last:
                xc = moe_layer(xc, norm2_g[i], mod, lambda tm: ctx_row, router_w[j], router_b[j],
                               moe_w_gate[j], moe_w_up[j], moe_w_down[j], None)
    if (DEPTH - 1) % 2 == 0:
        xl = final_rmsnorm(xl, final_norm_g)
    return xl.reshape(n_batch, SEQ, D_MODEL)
```

```python
import functools

import jax
import jax.numpy as jnp
from jax import lax
from jax.experimental import pallas as pl
from jax.experimental.pallas import tpu as pltpu

BF = jnp.bfloat16
F32 = jnp.float32

D_MODEL = 2048
SEQ = 2048
CTX_LEN = 256
DEPTH = 2
GRID_W = 64
GRID_ROWS = SEQ // GRID_W
NA_HEADS = 16
NA_HEAD_DIM = 64
NA_WIN_ROWS = 8
NA_WIN_COLS = 16
GLA_HEADS = 4
GLA_DK = 128
GLA_DV = 256
GLA_GATE_RANK = 16
GLA_TAU = 16.0
GQA_HEADS = 8
GQA_KV_HEADS = 2
GQA_HEAD_DIM = 128
GQA_GROUP = GQA_HEADS // GQA_KV_HEADS
ROPE_THETA = 10000.0
D_FF = 5632
N_EXPERTS = 8
TOP_K = 2
NORM_EPS = 1e-6
BRANCH_W = 1024

C_NA_Q, C_NA_K, C_NA_V = 0, 1024, 2048
C_GLA_Q, C_GLA_K, C_GLA_V, C_GLA_OG = 3072, 3584, 4096, 5120
C_GQA_Q, C_GQA_K, C_GQA_V = 6144, 7168, 7424
C_GATE_A, C_GATE_B, C_GATE_C = 7680, 9728, 11776
C_LOW = 13824
D_IN_PACKED = 14336
R_LOW = 6144
R_GQA_Q = 6176
R_END = 13856

V7X_LANES = 128
V7X_VMEM_LIMIT_BYTES = 56 * 1024 * 1024

GLA_STEP = 128
GLA_SUB = 16
MOE_ROWS = 1024
MOE_SUB = 512
MOE_UP_TILE = 256
MOE_DOWN_TILE = 256
MOE_GATHER_ROWS = 256


def _cparams(*sem):
    return pltpu.CompilerParams(dimension_semantics=sem,
                                vmem_limit_bytes=V7X_VMEM_LIMIT_BYTES)


def _dot(a, b):
    return jnp.dot(a, b, preferred_element_type=F32)


def _dot_nt(a, b):
    return lax.dot_general(a, b, (((1,), (1,)), ((), ())), preferred_element_type=F32)


def _silu(x):
    return x * jax.nn.sigmoid(x)


def _latent_mod_row(tm):
    return lambda t: (t * tm) // SEQ


def _ctx_mod_row(n_batch):
    return lambda t: n_batch


def _ada_kernel(c_ref, w_ref, b_ref, o_ref):
    a = _silu(c_ref[...])
    o_ref[...] = _dot(a.astype(BF), w_ref[...].astype(BF)) + b_ref[...]


def ada_modulation(cvec, w_ada, b_ada):
    tn = 1024
    n6 = 6 * D_MODEL
    out = pl.pallas_call(
        _ada_kernel,
        grid=(DEPTH, n6 // tn),
        in_specs=[
            pl.BlockSpec((8, D_MODEL), lambda l, j: (0, 0)),
            pl.BlockSpec((None, D_MODEL, tn), lambda l, j: (l, 0, j)),
            pl.BlockSpec((None, 1, tn), lambda l, j: (l, 0, j)),
        ],
        out_specs=pl.BlockSpec((None, 8, tn), lambda l, j: (l, 0, j)),
        out_shape=jax.ShapeDtypeStruct((DEPTH, 8, n6), F32),
        compiler_params=_cparams("parallel", "parallel"),
        name="ada_modulation",
    )(cvec, w_ada, b_ada.reshape(DEPTH, 1, n6))
    return out.reshape(DEPTH, 8, 6, D_MODEL)


def _modulated_norm(x, g, mod, shift_idx, scale_idx):
    ms = jnp.mean(x * x, axis=-1, keepdims=True)
    y = x * lax.rsqrt(ms + NORM_EPS) * g
    return y * (1.0 + mod[scale_idx:scale_idx + 1]) + mod[shift_idx:shift_idx + 1]


def _norm_mod_kernel(x_ref, g_ref, mod_ref, o_ref, *, shift_idx, scale_idx):
    o_ref[...] = _modulated_norm(x_ref[...], g_ref[...], mod_ref[...],
                                 shift_idx, scale_idx).astype(o_ref.dtype)


def norm_mod(x, g, mod, mod_row, shift_idx, scale_idx, tm=512):
    m = x.shape[0]
    tm = min(tm, m)
    return pl.pallas_call(
        functools.partial(_norm_mod_kernel, shift_idx=shift_idx, scale_idx=scale_idx),
        grid=(m // tm,),
        in_specs=[
            pl.BlockSpec((tm, D_MODEL), lambda t: (t, 0)),
            pl.BlockSpec((1, D_MODEL), lambda t: (0, 0)),
            pl.BlockSpec((None, 6, D_MODEL), lambda t: (mod_row(t), 0, 0)),
        ],
        out_specs=pl.BlockSpec((tm, D_MODEL), lambda t: (t, 0)),
        out_shape=jax.ShapeDtypeStruct((m, D_MODEL), BF),
        compiler_params=_cparams("parallel"),
        name="norm_mod",
    )(x, g.reshape(1, D_MODEL), mod)


def _rmsnorm_kernel(x_ref, g_ref, o_ref):
    x = x_ref[...]
    ms = jnp.mean(x * x, axis=-1, keepdims=True)
    o_ref[...] = x * lax.rsqrt(ms + NORM_EPS) * g_ref[...]


def final_rmsnorm(x, g, tm=512):
    m = x.shape[0]
    return pl.pallas_call(
        _rmsnorm_kernel,
        grid=(m // tm,),
        in_specs=[pl.BlockSpec((tm, D_MODEL), lambda t: (t, 0)),
                  pl.BlockSpec((1, D_MODEL), lambda t: (0, 0))],
        out_specs=pl.BlockSpec((tm, D_MODEL), lambda t: (t, 0)),
        out_shape=jax.ShapeDtypeStruct((m, D_MODEL), F32),
        compiler_params=_cparams("parallel"),
        name="final_rmsnorm",
    )(x, g.reshape(1, D_MODEL))


def _mm_kernel(a_ref, w_ref, o_ref):
    o_ref[...] = _dot(a_ref[...], w_ref[...].astype(BF)).astype(o_ref.dtype)


def matmul(a, w, out_dtype, tm=1024, tn=512):
    m, k = a.shape
    n = w.shape[1]
    tm = min(tm, m)
    return pl.pallas_call(
        _mm_kernel,
        grid=(m // tm, n // tn),
        in_specs=[pl.BlockSpec((tm, k), lambda i, j: (i, 0)),
                  pl.BlockSpec((k, tn), lambda i, j: (0, j))],
        out_specs=pl.BlockSpec((tm, tn), lambda i, j: (i, j)),
        out_shape=jax.ShapeDtypeStruct((m, n), out_dtype),
        compiler_params=_cparams("parallel", "parallel"),
        name="matmul",
    )(a, w)


def _mm_res_kernel(a_ref, w_ref, x_ref, mod_ref, o_ref, *, gate_idx):
    y = _dot(a_ref[...], w_ref[...].astype(BF))
    o_ref[...] = x_ref[...] + mod_ref[gate_idx:gate_idx + 1, :] * y


def matmul_residual(a, w, layer, x, mod, mod_row, gate_idx, tm=1024, tn=256):
    m, k = a.shape
    n = w.shape[2]
    tm = min(tm, m)
    return pl.pallas_call(
        functools.partial(_mm_res_kernel, gate_idx=gate_idx),
        grid=(m // tm, n // tn),
        in_specs=[pl.BlockSpec((tm, k), lambda i, j: (i, 0)),
                  pl.BlockSpec((None, k, tn), lambda i, j: (layer, 0, j)),
                  pl.BlockSpec((tm, tn), lambda i, j: (i, j)),
                  pl.BlockSpec((None, 6, tn), lambda i, j: (mod_row(i), 0, j))],
        out_specs=pl.BlockSpec((tm, tn), lambda i, j: (i, j)),
        out_shape=jax.ShapeDtypeStruct((m, n), F32),
        compiler_params=_cparams("parallel", "parallel"),
        name="matmul_residual",
    )(a, w, x, mod)


def _merge_kernel(a_ref, b_ref, c_ref, ga_ref, gb_ref, gc_ref, wa_ref, wb_ref, wc_ref, o_ref):
    def branch(x_ref, g_ref, w_ref):
        return jax.nn.sigmoid(g_ref[...].astype(F32)) * _dot(x_ref[...], w_ref[...].astype(BF))

    o_ref[...] = (branch(a_ref, ga_ref, wa_ref) + branch(b_ref, gb_ref, wb_ref)
                  + branch(c_ref, gc_ref, wc_ref)).astype(o_ref.dtype)


def merge_branches(a, b, c, proj, w_pa, w_pb, w_pc, layer, tm=1024, tn=512):
    m = a.shape[0]
    tm = min(tm, m)
    x_spec = pl.BlockSpec((tm, BRANCH_W), lambda i, j: (i, 0))
    w_spec = pl.BlockSpec((None, BRANCH_W, tn), lambda i, j: (layer, 0, j))

    def gate_spec(col):
        return pl.BlockSpec((tm, tn), lambda i, j: (i, col // tn + j))

    return pl.pallas_call(
        _merge_kernel,
        grid=(m // tm, D_MODEL // tn),
        in_specs=[x_spec, x_spec, x_spec,
                  gate_spec(C_GATE_A), gate_spec(C_GATE_B), gate_spec(C_GATE_C),
                  w_spec, w_spec, w_spec],
        out_specs=pl.BlockSpec((tm, tn), lambda i, j: (i, j)),
        out_shape=jax.ShapeDtypeStruct((m, D_MODEL), BF),
        compiler_params=_cparams("parallel", "parallel"),
        name="merge_branches",
    )(a, b, c, proj, proj, proj, w_pa, w_pb, w_pc)


def _ffn_up_kernel(h_ref, wg_ref, wu_ref, o_ref):
    h = h_ref[...]
    g = _dot(h, wg_ref[...].astype(BF))
    u = _dot(h, wu_ref[...].astype(BF))
    o_ref[...] = (_silu(g) * u).astype(o_ref.dtype)


def ffn_up(h, w_gate, w_up, tm=1024, tn=512):
    m = h.shape[0]
    tm = min(tm, m)
    return pl.pallas_call(
        _ffn_up_kernel,
        grid=(m // tm, D_FF // tn),
        in_specs=[pl.BlockSpec((tm, D_MODEL), lambda i, j: (i, 0)),
                  pl.BlockSpec((D_MODEL, tn), lambda i, j: (0, j)),
                  pl.BlockSpec((D_MODEL, tn), lambda i, j: (0, j))],
        out_specs=pl.BlockSpec((tm, tn), lambda i, j: (i, j)),
        out_shape=jax.ShapeDtypeStruct((m, D_FF), BF),
        compiler_params=_cparams("parallel", "parallel"),
        name="ffn_up",
    )(h, w_gate, w_up)


def _gqa_prep_kernel(q_ref, k_ref, cos_ref, sin_ref, qg_ref, kg_ref, qo_ref, ko_ref):
    cos = cos_ref[...]
    sin = sin_ref[...]
    lane = lax.broadcasted_iota(jnp.int32, cos.shape, 1)
    even = (lane & 1) == 0

    def prep(x_ref, g_ref, o_ref, heads):
        for h in range(heads):
            sl = slice(h * GQA_HEAD_DIM, (h + 1) * GQA_HEAD_DIM)
            x = x_ref[:, sl].astype(F32)
            ms = jnp.mean(x * x, axis=-1, keepdims=True)
            y = x * lax.rsqrt(ms + NORM_EPS) * g_ref[...]
            swapped = jnp.where(even, pltpu.roll(y, GQA_HEAD_DIM - 1, 1), pltpu.roll(y, 1, 1))
            o_ref[:, sl] = (y * cos + swapped * sin).astype(o_ref.dtype)

    prep(q_ref, qg_ref, qo_ref, GQA_HEADS)
    prep(k_ref, kg_ref, ko_ref, GQA_KV_HEADS)


def gqa_prep(proj, cos_t, sin_t, qn_g, kn_g, table_block, tm=256):
    m = proj.shape[0]
    qw = GQA_HEADS * GQA_HEAD_DIM
    kw = GQA_KV_HEADS * GQA_HEAD_DIM
    return pl.pallas_call(
        _gqa_prep_kernel,
        grid=(m // tm,),
        in_specs=[pl.BlockSpec((tm, qw), lambda t: (t, C_GQA_Q // qw)),
                  pl.BlockSpec((tm, kw), lambda t: (t, C_GQA_K // kw)),
                  pl.BlockSpec((tm, GQA_HEAD_DIM), lambda t: (table_block(t), 0)),
                  pl.BlockSpec((tm, GQA_HEAD_DIM), lambda t: (table_block(t), 0)),
                  pl.BlockSpec((1, GQA_HEAD_DIM), lambda t: (0, 0)),
                  pl.BlockSpec((1, GQA_HEAD_DIM), lambda t: (0, 0))],
        out_specs=[pl.BlockSpec((tm, qw), lambda t: (t, 0)),
                   pl.BlockSpec((tm, kw), lambda t: (t, 0))],
        out_shape=[jax.ShapeDtypeStruct((m, qw), BF), jax.ShapeDtypeStruct((m, kw), BF)],
        compiler_params=_cparams("parallel"),
        name="gqa_prep",
    )(proj, proj, cos_t, sin_t, qn_g.reshape(1, -1), kn_g.reshape(1, -1))


def rope_tables():
    half = GQA_HEAD_DIM // 2
    freqs = ROPE_THETA ** (-jnp.arange(0, half, 2, dtype=F32) / half)
    t = jnp.arange(SEQ)
    row = (t // GRID_W).astype(F32)
    col = (t % GRID_W).astype(F32)
    ang = jnp.concatenate([row[:, None] * freqs, col[:, None] * freqs], axis=-1)
    cos = jnp.repeat(jnp.cos(ang), 2, axis=-1)
    sin = jnp.repeat(jnp.sin(ang), 2, axis=-1)
    sign = jnp.tile(jnp.array([-1.0, 1.0], F32), half)
    cos = jnp.concatenate([cos, jnp.ones((256, GQA_HEAD_DIM), F32)], axis=0)
    sin = jnp.concatenate([sin * sign, jnp.zeros((256, GQA_HEAD_DIM), F32)], axis=0)
    return cos, sin


def _attend(q, kv_list, scale):
    scores = []
    for k, _, bias in kv_list:
        s = _dot_nt(q, k) * scale
        if bias is not None:
            s = s + bias
        scores.append(s)
    m = functools.reduce(jnp.maximum, [jnp.max(s, axis=-1, keepdims=True) for s in scores])
    ps = [jnp.exp(s - m) for s in scores]
    den = functools.reduce(jnp.add, [jnp.sum(p, axis=-1, keepdims=True) for p in ps])
    o = functools.reduce(jnp.add, [_dot(p.astype(BF), v) for p, (_, v, _) in zip(ps, kv_list)])
    return o / den


def _head_pair_rows(q2):
    lane = lax.broadcasted_iota(jnp.int32, q2.shape, 1)
    first = lane < NA_HEAD_DIM
    zero = jnp.zeros_like(q2)
    return jnp.concatenate([jnp.where(first, q2, zero), jnp.where(first, zero, q2)], axis=0)


def _head_pair_merge(o, m):
    lane = lax.broadcasted_iota(jnp.int32, (m, 2 * NA_HEAD_DIM), 1)
    return jnp.where(lane < NA_HEAD_DIM, o[:m], o[m:])


def _na_kernel(q_ref, kl_ref, vl_ref, kc_ref, vc_ref, bias_ref, o_ref):
    r = pl.program_id(1)
    start = jnp.clip(r - NA_WIN_ROWS // 2, 0, GRID_ROWS - NA_WIN_ROWS) * GRID_W
    start = pl.multiple_of(start, GRID_W)
    n_loc = NA_WIN_ROWS * GRID_W
    scale = NA_HEAD_DIM ** -0.5
    for hp in range(NA_HEADS // 2):
        sl = slice(hp * 128, (hp + 1) * 128)
        qq = _head_pair_rows(q_ref[:, sl])
        bias = jnp.concatenate([bias_ref[2 * hp], bias_ref[2 * hp + 1]], axis=0)
        o = _attend(qq, [(kl_ref[pl.ds(start, n_loc), sl], vl_ref[pl.ds(start, n_loc), sl], bias),
                         (kc_ref[:, sl], vc_ref[:, sl], None)], scale)
        o_ref[:, sl] = _head_pair_merge(o, GRID_W).astype(o_ref.dtype)


def na_latent(proj_l, proj_c, bias_tab, n_batch):
    w = NA_HEADS * NA_HEAD_DIM

    def pattern(r):
        return jnp.where(r < 4, r, jnp.where(r > GRID_ROWS - 4, r - (GRID_ROWS - NA_WIN_ROWS), 4))

    return pl.pallas_call(
        _na_kernel,
        grid=(n_batch, GRID_ROWS),
        in_specs=[pl.BlockSpec((GRID_W, w), lambda b, r: (b * GRID_ROWS + r, C_NA_Q // w)),
                  pl.BlockSpec((SEQ, w), lambda b, r: (b, C_NA_K // w)),
                  pl.BlockSpec((SEQ, w), lambda b, r: (b, C_NA_V // w)),
                  pl.BlockSpec((CTX_LEN, w), lambda b, r: (b, C_NA_K // w)),
                  pl.BlockSpec((CTX_LEN, w), lambda b, r: (b, C_NA_V // w)),
                  pl.BlockSpec((None, NA_HEADS, GRID_W, NA_WIN_ROWS * GRID_W),
                               lambda b, r: (pattern(r), 0, 0, 0))],
        out_specs=pl.BlockSpec((GRID_W, w), lambda b, r: (b * GRID_ROWS + r, 0)),
        out_shape=jax.ShapeDtypeStruct((n_batch * SEQ, w), BF),
        compiler_params=_cparams("parallel", "arbitrary"),
        name="na_latent",
    )(proj_l, proj_l, proj_l, proj_c, proj_c, bias_tab)


def na_bias_table(rpb):
    cols = jnp.arange(GRID_W)
    col_start = jnp.clip(cols - NA_WIN_COLS // 2, 0, GRID_W - NA_WIN_COLS)
    in_win = (cols[None, :] >= col_start[:, None]) & (cols[None, :] < col_start[:, None] + NA_WIN_COLS)
    dc = jnp.clip(cols[None, :] - cols[:, None] + NA_WIN_COLS - 1, 0, 2 * NA_WIN_COLS - 2)
    onehot = (dc[None] == jnp.arange(2 * NA_WIN_COLS - 1)[:, None, None]).astype(F32)
    by_col = jnp.einsum('hdc,cqk->hdqk', rpb.astype(F32), onehot, precision=lax.Precision.HIGHEST)
    by_col = jnp.where(in_win[None, None], by_col, -jnp.inf)
    last = NA_WIN_ROWS - 1
    bias = jnp.stack([by_col[:, last - p:last - p + NA_WIN_ROWS] for p in range(NA_WIN_ROWS)])
    bias = bias.transpose(0, 1, 3, 2, 4)
    return bias.reshape(NA_WIN_ROWS, NA_HEADS, GRID_W, NA_WIN_ROWS * GRID_W)


def _na_ctx_kernel(q_ref, k_ref, v_ref, o_ref):
    scale = NA_HEAD_DIM ** -0.5
    for hp in range(NA_HEADS // 2):
        sl = slice(hp * 128, (hp + 1) * 128)
        o = _attend(_head_pair_rows(q_ref[:, sl]), [(k_ref[:, sl], v_ref[:, sl], None)], scale)
        o_ref[:, sl] = _head_pair_merge(o, CTX_LEN).astype(o_ref.dtype)


def na_context(proj_c, n_batch):
    w = NA_HEADS * NA_HEAD_DIM
    return pl.pallas_call(
        _na_ctx_kernel,
        grid=(n_batch,),
        in_specs=[pl.BlockSpec((CTX_LEN, w), lambda b: (b, C_NA_Q // w)),
                  pl.BlockSpec((CTX_LEN, w), lambda b: (b, C_NA_K // w)),
                  pl.BlockSpec((CTX_LEN, w), lambda b: (b, C_NA_V // w))],
        out_specs=pl.BlockSpec((CTX_LEN, w), lambda b: (b, 0)),
        out_shape=jax.ShapeDtypeStruct((n_batch * CTX_LEN, w), BF),
        compiler_params=_cparams("parallel"),
        name="na_context",
    )(proj_c, proj_c, proj_c)


def _gqa_rows(q_ref):
    return jnp.concatenate([q_ref[:, g * GQA_HEAD_DIM:(g + 1) * GQA_HEAD_DIM]
                            for g in range(GQA_GROUP)], axis=0)


def _gqa_store(o, o_ref):
    tq = o_ref.shape[0]
    for g in range(GQA_GROUP):
        o_ref[:, g * GQA_HEAD_DIM:(g + 1) * GQA_HEAD_DIM] = o[g * tq:(g + 1) * tq].astype(o_ref.dtype)


def _gqa_kernel(q_ref, kl_ref, vl_ref, kc_ref, vc_ref, o_ref):
    for g in range(GQA_GROUP):
        sl = slice(g * GQA_HEAD_DIM, (g + 1) * GQA_HEAD_DIM)
        o = _attend(q_ref[:, sl], [(kl_ref[...], vl_ref[...], None), (kc_ref[...], vc_ref[...], None)],
                    GQA_HEAD_DIM ** -0.5)
        o_ref[:, sl] = o.astype(o_ref.dtype)


def _gqa_ctx_kernel(q_ref, kc_ref, vc_ref, o_ref):
    o = _attend(_gqa_rows(q_ref), [(kc_ref[...], vc_ref[...], None)], GQA_HEAD_DIM ** -0.5)
    _gqa_store(o, o_ref)


def gqa_latent(q_l, k_l, k_c, proj_l, proj_c, n_batch, tq=256):
    gw = GQA_GROUP * GQA_HEAD_DIM
    nq = SEQ // tq
    dh = GQA_HEAD_DIM
    return pl.pallas_call(
        _gqa_kernel,
        grid=(n_batch, GQA_KV_HEADS, nq),
        in_specs=[pl.BlockSpec((tq, gw), lambda b, h, i: (b * nq + i, h)),
                  pl.BlockSpec((SEQ, dh), lambda b, h, i: (b, h)),
                  pl.BlockSpec((SEQ, dh), lambda b, h, i: (b, C_GQA_V // dh + h)),
                  pl.BlockSpec((CTX_LEN, dh), lambda b, h, i: (b, h)),
                  pl.BlockSpec((CTX_LEN, dh), lambda b, h, i: (b, C_GQA_V // dh + h))],
        out_specs=pl.BlockSpec((tq, gw), lambda b, h, i: (b * nq + i, h)),
        out_shape=jax.ShapeDtypeStruct((n_batch * SEQ, GQA_HEADS * dh), BF),
        compiler_params=_cparams("parallel", "parallel", "arbitrary"),
        name="gqa_latent",
    )(q_l, k_l, proj_l, k_c, proj_c)


def gqa_context(q_c, k_c, proj_c, n_batch):
    gw = GQA_GROUP * GQA_HEAD_DIM
    dh = GQA_HEAD_DIM
    return pl.pallas_call(
        _gqa_ctx_kernel,
        grid=(n_batch, GQA_KV_HEADS),
        in_specs=[pl.BlockSpec((CTX_LEN, gw), lambda b, h: (b, h)),
                  pl.BlockSpec((CTX_LEN, dh), lambda b, h: (b, h)),
                  pl.BlockSpec((CTX_LEN, dh), lambda b, h: (b, C_GQA_V // dh + h))],
        out_specs=pl.BlockSpec((CTX_LEN, gw), lambda b, h: (b, h)),
        out_shape=jax.ShapeDtypeStruct((n_batch * CTX_LEN, GQA_HEADS * dh), BF),
        compiler_params=_cparams("parallel", "parallel"),
        name="gqa_context",
    )(q_c, k_c, proj_c)


def _log_sigmoid(z):
    return jnp.minimum(z, 0.0) - jnp.log1p(jnp.exp(-jnp.abs(z)))


def _gla_step(c, direction, q_s, k_s, v_s, low_s, wa_ref, ba_ref, st_ref, o_s):
    n = GLA_STEP
    r0 = pl.multiple_of(c * n, n)
    rows = pl.ds(r0, n)
    qf = q_s[rows, :].astype(F32) * (GLA_DK ** -0.5)
    kf = k_s[rows, :].astype(F32)
    v = v_s[rows, :]
    z = _dot(low_s[rows, :], wa_ref[direction].astype(BF)) + ba_ref[direction]
    g = _log_sigmoid(z) * (1.0 / GLA_TAU)

    ri = lax.broadcasted_iota(jnp.int32, (n, n), 0)
    ci = lax.broadcasted_iota(jnp.int32, (n, n), 1)
    tri = (ci <= ri) if direction == 0 else (ci >= ri)
    tri = jnp.where(tri, 1.0, 0.0).astype(BF)
    g_hi = g.astype(BF)
    g_lo = (g - g_hi.astype(F32)).astype(BF)
    bc = _dot(tri, g_hi) + _dot(tri, g_lo)
    btot = bc[n - 1:n] if direction == 0 else bc[0:1]

    state = st_ref[...]
    q_in = (qf * jnp.exp(bc)).astype(BF)
    o = _dot_nt(q_in, state.astype(BF))
    k_out = (kf * jnp.exp(btot - bc)).astype(BF)
    v_t = v.astype(F32).T.astype(BF)
    st_ref[...] = jnp.exp(btot) * state + _dot(v_t, k_out)

    row_id = lax.broadcasted_iota(jnp.int32, (n, 1), 0)
    lane = lax.broadcasted_iota(jnp.int32, (GLA_SUB, n), 1)
    sub_row = lax.broadcasted_iota(jnp.int32, (GLA_SUB, 1), 0)
    neg_inf = jnp.float32(-jnp.inf)
    blocks = []
    for blk in range(n // GLA_SUB):
        lo, hi = blk * GLA_SUB, (blk + 1) * GLA_SUB
        q_b, b_b = qf[lo:hi], bc[lo:hi]
        if direction == 0 and blk > 0:
            ref, earlier = bc[lo - 1:lo], row_id < lo
        elif direction == 1 and hi < n:
            ref, earlier = bc[hi:hi + 1], row_id >= hi
        else:
            ref = None
        if ref is None:
            a_b = jnp.zeros((GLA_SUB, n), F32)
        else:
            k_dec = (kf * jnp.exp(jnp.where(earlier, ref - bc, neg_inf))).astype(BF)
            a_b = _dot_nt((q_b * jnp.exp(b_b - ref)).astype(BF), k_dec)
        for jl in range(GLA_SUB):
            j = lo + jl
            keep = (sub_row >= jl) if direction == 0 else (sub_row <= jl)
            dec = jnp.exp(jnp.where(keep, b_b - bc[j:j + 1], neg_inf))
            col = jnp.sum(q_b * kf[j:j + 1] * dec, axis=-1, keepdims=True)
            a_b = jnp.where(lane == j, col, a_b)
        blocks.append(a_b)
    attn = jnp.concatenate(blocks, axis=0).astype(BF)
    o_s[rows, :] = o + _dot(attn, v)


def _gla_kernel(ql, kl, vl, ogl, lowl, qc, kc, vc, ogc, lowc, wa_ref, ba_ref, ng_ref,
                yl_ref, yc_ref, q_s, k_s, v_s, low_s, of_s, ob_s, stf, stb):
    nc = CTX_LEN
    q_s[0:nc, :] = qc[...]
    q_s[nc:, :] = ql[...]
    k_s[0:nc, :] = kc[...]
    k_s[nc:, :] = kl[...]
    v_s[0:nc, :] = vc[...]
    v_s[nc:, :] = vl[...]
    low_s[0:nc, :] = lowc[...]
    low_s[nc:, :] = lowl[...]
    stf[...] = jnp.zeros_like(stf)
    stb[...] = jnp.zeros_like(stb)
    n_ctx = CTX_LEN // GLA_STEP
    n_all = (CTX_LEN + SEQ) // GLA_STEP

    def body(i, carry):
        cb = jnp.where(i < n_ctx, n_ctx - 1 - i, n_all + n_ctx - 1 - i)
        _gla_step(i, 0, q_s, k_s, v_s, low_s, wa_ref, ba_ref, stf, of_s)
        _gla_step(cb, 1, q_s, k_s, v_s, low_s, wa_ref, ba_ref, stb, ob_s)
        return carry

    lax.fori_loop(0, n_all, body, 0)

    def finish(lo, hi, og_ref, y_ref):
        o = of_s[lo:hi, :] + ob_s[lo:hi, :]
        ms = jnp.mean(o * o, axis=-1, keepdims=True)
        y = o * lax.rsqrt(ms + NORM_EPS) * ng_ref[...]
        y_ref[...] = (y * _silu(og_ref[...].astype(F32))).astype(y_ref.dtype)

    finish(0, nc, ogc, yc_ref)
    finish(nc, nc + SEQ, ogl, yl_ref)


def gla_bidirectional(proj_l, proj_c, w_a2, b_a, norm_g, n_batch):
    dk, dv = GLA_DK, GLA_DV
    wa = jnp.zeros((2, GLA_HEADS, V7X_LANES, dk), F32)
    for d in range(2):
        wa = wa.at[d, :, d * GLA_GATE_RANK:(d + 1) * GLA_GATE_RANK, :].set(
            w_a2[d].reshape(GLA_GATE_RANK, GLA_HEADS, dk).transpose(1, 0, 2))
    ba = b_a.reshape(2, GLA_HEADS, 1, dk)

    def specs(rows):
        return [pl.BlockSpec((rows, dk), lambda b, h: (b, C_GLA_Q // dk + h)),
                pl.BlockSpec((rows, dk), lambda b, h: (b, C_GLA_K // dk + h)),
                pl.BlockSpec((rows, dv), lambda b, h: (b, C_GLA_V // dv + h)),
                pl.BlockSpec((rows, dv), lambda b, h: (b, C_GLA_OG // dv + h)),
                pl.BlockSpec((rows, V7X_LANES), lambda b, h: (b, C_LOW // V7X_LANES))]

    n_rows = CTX_LEN + SEQ
    return pl.pallas_call(
        _gla_kernel,
        grid=(n_batch, GLA_HEADS),
        in_specs=specs(SEQ) + specs(CTX_LEN) + [
            pl.BlockSpec((2, None, V7X_LANES, dk), lambda b, h: (0, h, 0, 0)),
            pl.BlockSpec((2, None, 1, dk), lambda b, h: (0, h, 0, 0)),
            pl.BlockSpec((1, dv), lambda b, h: (0, 0))],
        out_specs=[pl.BlockSpec((SEQ, dv), lambda b, h: (b, h)),
                   pl.BlockSpec((CTX_LEN, dv), lambda b, h: (b, h))],
        out_shape=[jax.ShapeDtypeStruct((n_batch * SEQ, GLA_HEADS * dv), BF),
                   jax.ShapeDtypeStruct((n_batch * CTX_LEN, GLA_HEADS * dv), BF)],
        scratch_shapes=[pltpu.VMEM((n_rows, dk), BF), pltpu.VMEM((n_rows, dk), BF),
                        pltpu.VMEM((n_rows, dv), BF), pltpu.VMEM((n_rows, V7X_LANES), BF),
                        pltpu.VMEM((n_rows, dv), F32), pltpu.VMEM((n_rows, dv), F32),
                        pltpu.VMEM((dv, dk), F32), pltpu.VMEM((dv, dk), F32)],
        compiler_params=_cparams("parallel", "parallel"),
        name="gla_bidirectional",
    )(*([proj_l] * 5 + [proj_c] * 5), wa, ba, norm_g.reshape(1, dv))


def _router_kernel(x_ref, g_ref, mod_ref, rw_ref, rb_ref, h_ref, idx_ref, wt_ref):
    h = _modulated_norm(x_ref[...], g_ref[...], mod_ref[...], 3, 4)
    h_ref[...] = h.astype(h_ref.dtype)
    logits = jnp.dot(h, rw_ref[...], preferred_element_type=F32,
                     precision=lax.Precision.HIGHEST) + rb_ref[...]
    lane = lax.broadcasted_iota(jnp.int32, logits.shape, 1)
    neg_inf = jnp.float32(-jnp.inf)
    logits = jnp.where(lane < N_EXPERTS, logits, neg_inf)
    m1 = jnp.max(logits, axis=-1, keepdims=True)
    lane_f = lane.astype(F32)
    i1 = jnp.min(jnp.where(logits == m1, lane_f, float(V7X_LANES)), axis=-1, keepdims=True)
    rest = jnp.where(lane_f == i1, neg_inf, logits)
    m2 = jnp.max(rest, axis=-1, keepdims=True)
    i2 = jnp.min(jnp.where(rest == m2, lane_f, float(V7X_LANES)), axis=-1, keepdims=True)
    e = jnp.exp(m2 - m1)
    w1 = 1.0 / (1.0 + e)
    idx_ref[...] = jnp.where(lane == 0, i1, jnp.where(lane == 1, i2, 0.0)).astype(jnp.int32)
    wt_ref[...] = jnp.where(lane == 0, w1, jnp.where(lane == 1, e * w1, 0.0))


def moe_router(x, g, mod, mod_row, router_w, router_b, tm=512):
    m = x.shape[0]
    rw = jnp.zeros((D_MODEL, V7X_LANES), F32).at[:, :N_EXPERTS].set(router_w)
    rb = jnp.zeros((1, V7X_LANES), F32).at[0, :N_EXPERTS].set(router_b)
    return pl.pallas_call(
        _router_kernel,
        grid=(m // tm,),
        in_specs=[pl.BlockSpec((tm, D_MODEL), lambda t: (t, 0)),
                  pl.BlockSpec((1, D_MODEL), lambda t: (0, 0)),
                  pl.BlockSpec((None, 6, D_MODEL), lambda t: (mod_row(t), 0, 0)),
                  pl.BlockSpec((D_MODEL, V7X_LANES), lambda t: (0, 0)),
                  pl.BlockSpec((1, V7X_LANES), lambda t: (0, 0))],
        out_specs=[pl.BlockSpec((tm, D_MODEL), lambda t: (t, 0)),
                   pl.BlockSpec((tm, V7X_LANES), lambda t: (t, 0)),
                   pl.BlockSpec((tm, V7X_LANES), lambda t: (t, 0))],
        out_shape=[jax.ShapeDtypeStruct((m, D_MODEL), F32),
                   jax.ShapeDtypeStruct((m, V7X_LANES), jnp.int32),
                   jax.ShapeDtypeStruct((m, V7X_LANES), F32)],
        compiler_params=_cparams("parallel"),
        name="moe_router",
    )(x, g.reshape(1, D_MODEL), mod, rw, rb)


def _row_copy(src_hbm, src_row, dst, dst_row, sem):
    return pltpu.make_async_copy(src_hbm.at[pl.ds(src_row, 1)], dst.at[pl.ds(dst_row, 1)], sem)


def _gather_tile(src_hbm, idx_ref, n_rows, per_row, buf, sem, action):
    def body(i, carry):
        for k in range(per_row):
            cp = _row_copy(src_hbm, idx_ref[0, per_row * i + k], buf.at[k], i, sem)
            cp.start() if action == "start" else cp.wait()
        return carry

    lax.fori_loop(0, n_rows, body, 0)


def _dispatch_kernel(nv_ref, idx_ref, idx_next_ref, h_hbm, o_ref, buf, sem):
    t = pl.program_id(0)
    rows = o_ref.shape[0]
    slot = t % 2

    def start_tile(tile_idx_ref, tile, s):
        nv = nv_ref[tile]

        @pl.when(nv < rows)
        def _():
            buf[s] = jnp.zeros(buf.shape[1:], buf.dtype)

        _gather_tile(h_hbm, tile_idx_ref, nv, 1, buf.at[s], sem.at[s], "start")

    @pl.when(t == 0)
    def _():
        start_tile(idx_ref, 0, 0)

    @pl.when(t + 1 < pl.num_programs(0))
    def _():
        start_tile(idx_next_ref, t + 1, 1 - slot)

    _gather_tile(h_hbm, idx_ref, nv_ref[t], 1, buf.at[slot], sem.at[slot], "wait")
    o_ref[...] = buf[slot, 0].astype(o_ref.dtype)


def moe_dispatch(h, slot_t, tile_valid):
    n_slots = slot_t.shape[0]
    rows = MOE_GATHER_ROWS
    n_tiles = n_slots // rows
    idx = slot_t.reshape(n_tiles, 1, rows)
    grid_spec = pltpu.PrefetchScalarGridSpec(
        num_scalar_prefetch=1,
        grid=(n_tiles,),
        in_specs=[pl.BlockSpec((None, 1, rows), lambda t, nv: (t, 0, 0), memory_space=pltpu.SMEM),
                  pl.BlockSpec((None, 1, rows), lambda t, nv: (jnp.minimum(t + 1, n_tiles - 1), 0, 0),
                               memory_space=pltpu.SMEM),
                  pl.BlockSpec(memory_space=pl.ANY)],
        out_specs=pl.BlockSpec((rows, D_MODEL), lambda t, nv: (t, 0)),
        scratch_shapes=[pltpu.VMEM((2, 1, rows, D_MODEL), F32), pltpu.SemaphoreType.DMA((2,))],
    )
    return pl.pallas_call(
        _dispatch_kernel,
        grid_spec=grid_spec,
        out_shape=jax.ShapeDtypeStruct((n_slots, D_MODEL), BF),
        compiler_params=_cparams("arbitrary"),
        name="moe_dispatch",
    )(tile_valid, idx, idx, h)


def _moe_ffn_kernel(be_ref, nv_ref, nu_ref, x_ref, wg_ref, wu_ref, wd_ref, o_ref, act):
    s = pl.program_id(1)
    nv = nv_ref[pl.program_id(0)]
    n_up = D_FF // MOE_UP_TILE
    full = nv > MOE_SUB
    half = jnp.logical_and(nv > 0, nv <= MOE_SUB)

    @pl.when(s < n_up)
    def _():
        wg = wg_ref[...].astype(BF)
        wu = wu_ref[...].astype(BF)
        cols = pl.ds(pl.multiple_of(s * MOE_UP_TILE, MOE_UP_TILE), MOE_UP_TILE)

        def up(n_rows):
            x = x_ref[0:n_rows, :]
            act[0:n_rows, cols] = (_silu(_dot(x, wg)) * _dot(x, wu)).astype(BF)

        pl.when(full)(lambda: up(MOE_ROWS))
        pl.when(half)(lambda: up(MOE_SUB))

    @pl.when(s >= n_up)
    def _():
        wd = wd_ref[...].astype(BF)

        @pl.when(full)
        def _():
            o_ref[...] = _dot(act[...], wd)

        @pl.when(half)
        def _():
            o_ref[0:MOE_SUB, :] = _dot(act[0:MOE_SUB, :], wd)
            o_ref[MOE_SUB:, :] = jnp.zeros((MOE_ROWS - MOE_SUB, MOE_DOWN_TILE), F32)

        @pl.when(nv == 0)
        def _():
            o_ref[...] = jnp.zeros_like(o_ref)


def moe_expert_ffn(xb, block_e, n_valid, n_used, w_gate, w_up, w_down):
    n_blk = xb.shape[0] // MOE_ROWS
    n_up = D_FF // MOE_UP_TILE
    n_down = D_MODEL // MOE_DOWN_TILE

    def last_used(b, nu):
        return jnp.minimum(b, nu[0] - 1)

    def up_idx(b, s, nu):
        return jnp.where(b < nu[0], jnp.minimum(s, n_up - 1), n_up - 1)

    def down_idx(b, s, nu):
        return jnp.where(b < nu[0], jnp.maximum(s - n_up, 0), n_down - 1)

    grid_spec = pltpu.PrefetchScalarGridSpec(
        num_scalar_prefetch=3,
        grid=(n_blk, n_up + n_down),
        in_specs=[
            pl.BlockSpec((MOE_ROWS, D_MODEL), lambda b, s, be, nv, nu: (last_used(b, nu), 0)),
            pl.BlockSpec((None, D_MODEL, MOE_UP_TILE),
                         lambda b, s, be, nv, nu: (be[last_used(b, nu)], 0, up_idx(b, s, nu))),
            pl.BlockSpec((None, D_MODEL, MOE_UP_TILE),
                         lambda b, s, be, nv, nu: (be[last_used(b, nu)], 0, up_idx(b, s, nu))),
            pl.BlockSpec((None, D_FF, MOE_DOWN_TILE),
                         lambda b, s, be, nv, nu: (be[last_used(b, nu)], 0, down_idx(b, s, nu))),
        ],
        out_specs=pl.BlockSpec((MOE_ROWS, MOE_DOWN_TILE),
                               lambda b, s, be, nv, nu: (b, jnp.maximum(s - n_up, 0))),
        scratch_shapes=[pltpu.VMEM((MOE_ROWS, D_FF), BF)],
    )
    return pl.pallas_call(
        _moe_ffn_kernel,
        grid_spec=grid_spec,
        out_shape=jax.ShapeDtypeStruct((n_blk * MOE_ROWS, D_MODEL), F32),
        compiler_params=_cparams("arbitrary", "arbitrary"),
        name="moe_expert_ffn",
    )(block_e, n_valid, n_used, xb, w_gate, w_up, w_down)


def _moe_combine_kernel(idx_ref, idx_next_ref, x_ref, wt_ref, mod_ref, g_ref, yb_hbm, o_ref, buf, sem,
                        *, final_norm):
    t = pl.program_id(0)
    tm = x_ref.shape[0]
    slot = t % 2

    @pl.when(t == 0)
    def _():
        _gather_tile(yb_hbm, idx_ref, tm, TOP_K, buf.at[0], sem.at[0], "start")

    @pl.when(t + 1 < pl.num_programs(0))
    def _():
        _gather_tile(yb_hbm, idx_next_ref, tm, TOP_K, buf.at[1 - slot], sem.at[1 - slot], "start")

    _gather_tile(yb_hbm, idx_ref, tm, TOP_K, buf.at[slot], sem.at[slot], "wait")
    w = wt_ref[...]
    y = w[:, 0:1] * buf[slot, 0] + w[:, 1:2] * buf[slot, 1]
    out = x_ref[...] + mod_ref[5:6, :] * y
    if final_norm:
        ms = jnp.mean(out * out, axis=-1, keepdims=True)
        out = out * lax.rsqrt(ms + NORM_EPS) * g_ref[...]
    o_ref[...] = out


def moe_combine(x, yb, dest, wt, mod, mod_row, final_g, tm=256):
    m = x.shape[0]
    tm = min(tm, m)
    n_tiles = m // tm
    row = pl.BlockSpec((tm, D_MODEL), lambda t: (t, 0))
    final_norm = final_g is not None
    g = final_g if final_norm else jnp.ones((D_MODEL,), F32)
    idx = dest.reshape(n_tiles, 1, TOP_K * tm)
    return pl.pallas_call(
        functools.partial(_moe_combine_kernel, final_norm=final_norm),
        grid=(n_tiles,),
        in_specs=[pl.BlockSpec((None, 1, TOP_K * tm), lambda t: (t, 0, 0), memory_space=pltpu.SMEM),
                  pl.BlockSpec((None, 1, TOP_K * tm), lambda t: (jnp.minimum(t + 1, n_tiles - 1), 0, 0),
                               memory_space=pltpu.SMEM),
                  row,
                  pl.BlockSpec((tm, V7X_LANES), lambda t: (t, 0)),
                  pl.BlockSpec((None, 6, D_MODEL), lambda t: (mod_row(t), 0, 0)),
                  pl.BlockSpec((1, D_MODEL), lambda t: (0, 0)),
                  pl.BlockSpec(memory_space=pl.ANY)],
        out_specs=row,
        out_shape=jax.ShapeDtypeStruct((m, D_MODEL), F32),
        scratch_shapes=[pltpu.VMEM((2, TOP_K, tm, D_MODEL), F32), pltpu.SemaphoreType.DMA((2,))],
        compiler_params=_cparams("arbitrary"),
        name="moe_combine",
    )(idx, idx, x, wt, mod, g.reshape(1, D_MODEL), yb)


def moe_layout(top_i):
    t = top_i.shape[0]
    n_assign = t * TOP_K
    flat_e = top_i.reshape(-1)
    onehot = (flat_e[:, None] == jnp.arange(N_EXPERTS, dtype=jnp.int32)[None, :]).astype(jnp.int32)
    csum = jnp.cumsum(onehot, axis=0)
    rank = jnp.take_along_axis(csum, flat_e[:, None], axis=1)[:, 0] - 1
    counts = csum[-1]
    padded = (counts + MOE_ROWS - 1) // MOE_ROWS * MOE_ROWS
    pends = jnp.cumsum(padded)
    pstarts = pends - padded
    dest = pstarts[flat_e] + rank
    n_blk = n_assign // MOE_ROWS + N_EXPERTS
    flat_t = jnp.repeat(jnp.arange(t, dtype=jnp.int32), TOP_K)
    slot_t = jnp.zeros((n_blk * MOE_ROWS,), jnp.int32).at[dest].set(flat_t)
    blk = jnp.arange(n_blk, dtype=jnp.int32)
    blk_start = blk * MOE_ROWS
    block_e = jnp.minimum(jnp.sum((pends[None, :] <= blk_start[:, None]).astype(jnp.int32), axis=1),
                          N_EXPERTS - 1)
    n_used = pends[-1] // MOE_ROWS
    n_valid = jnp.clip(counts[block_e] - (blk_start - pstarts[block_e]), 0, MOE_ROWS)
    n_valid = jnp.where(blk < n_used, n_valid, 0)
    tiles_per_blk = MOE_ROWS // MOE_GATHER_ROWS
    tile_off = jnp.arange(tiles_per_blk, dtype=jnp.int32) * MOE_GATHER_ROWS
    tile_valid = jnp.clip(n_valid[:, None] - tile_off[None, :], 0, MOE_GATHER_ROWS).reshape(-1)
    return (slot_t, dest.reshape(t, TOP_K).astype(jnp.int32), block_e.astype(jnp.int32),
            n_valid.astype(jnp.int32), n_used.astype(jnp.int32).reshape(1), tile_valid.astype(jnp.int32))


def moe_layer(x, norm_g, mod, mod_row_tm, router_w, router_b, w_gate, w_up, w_down, final_g):
    router_tm, combine_tm = 512, 256
    h, idx, wt = moe_router(x, norm_g, mod, mod_row_tm(router_tm), router_w, router_b, router_tm)
    slot_t, dest, block_e, n_valid, n_used, tile_valid = moe_layout(idx[:, :TOP_K])
    xb = moe_dispatch(h, slot_t, tile_valid)
    yb = moe_expert_ffn(xb, block_e, n_valid, n_used, w_gate, w_up, w_down)
    return moe_combine(x, yb, dest, wt, mod, mod_row_tm(combine_tm), final_g, combine_tm)


def pack_w_in(w):
    pad = jnp.zeros((D_MODEL, D_IN_PACKED - (R_END - R_GQA_Q) - R_LOW - 2 * GLA_GATE_RANK), w.dtype)
    return jnp.concatenate([w[:, :R_LOW], w[:, R_GQA_Q:R_END], w[:, R_LOW:R_GQA_Q], pad],
                           axis=1).astype(BF)


def kernel(x, c, ctx, c_ctx, w_ada, b_ada, norm1_g, norm2_g, w_in, na_rpb, gla_w_a2, gla_b_a, gla_norm_g, gqa_qn_g, gqa_kn_g, w_pa, w_pb, w_pc, w_out, dense_w_gate, dense_w_up, dense_w_down, router_w, router_b, moe_w_gate, moe_w_up, moe_w_down, final_norm_g):
    n_batch = x.shape[0]
    xl = x.reshape(n_batch * SEQ, D_MODEL)
    xc = ctx.reshape(n_batch * CTX_LEN, D_MODEL)
    cvec = jnp.zeros((8, D_MODEL), F32).at[:n_batch].set(c).at[n_batch].set(c_ctx)
    mods = ada_modulation(cvec, w_ada, b_ada)
    cos_t, sin_t = rope_tables()
    lat_row = _latent_mod_row
    ctx_row = _ctx_mod_row(n_batch)
    prep_tm = 256
    lat_table = lambda t: t % (SEQ // prep_tm)
    ctx_table = lambda t: SEQ // prep_tm

    for i in range(DEPTH):
        last = i == DEPTH - 1
        mod = mods[i]
        w_in_p = pack_w_in(w_in[i])
        hl = norm_mod(xl, norm1_g[i], mod, lat_row(512), 0, 1)
        hc = norm_mod(xc, norm1_g[i], mod, ctx_row, 0, 1)
        pl_ = matmul(hl, w_in_p, BF)
        pc_ = matmul(hc, w_in_p, BF)

        a_l = na_latent(pl_, pc_, na_bias_table(na_rpb[i]), n_batch)
        b_l, b_c = gla_bidirectional(pl_, pc_, gla_w_a2[i], gla_b_a[i], gla_norm_g[i], n_batch)
        q_l, k_l = gqa_prep(pl_, cos_t, sin_t, gqa_qn_g[i], gqa_kn_g[i], lat_table, prep_tm)
        q_c, k_c = gqa_prep(pc_, cos_t, sin_t, gqa_qn_g[i], gqa_kn_g[i], ctx_table, prep_tm)
        c_l = gqa_latent(q_l, k_l, k_c, pl_, pc_, n_batch)

        m_l = merge_branches(a_l, b_l, c_l, pl_, w_pa, w_pb, w_pc, i)
        xl = matmul_residual(m_l, w_out, i, xl, mod, lat_row(1024), 2)
        if not last:
            a_c = na_context(pc_, n_batch)
            c_c = gqa_context(q_c, k_c, pc_, n_batch)
            m_c = merge_branches(a_c, b_c, c_c, pc_, w_pa, w_pb, w_pc, i)
            xc = matmul_residual(m_c, w_out, i, xc, mod, ctx_row, 2)

        j = i // 2
        if i % 2 == 0:
            def ffn(xs, mod_row_tm):
                h2 = norm_mod(xs, norm2_g[i], mod, mod_row_tm(512), 3, 4)
                u = ffn_up(h2, dense_w_gate[j], dense_w_up[j])
                return matmul_residual(u, dense_w_down, j, xs, mod, mod_row_tm(1024), 5)
            xl = ffn(xl, lat_row)
            if not last:
                xc = ffn(xc, lambda tm: ctx_row)
        else:
            fin = final_norm_g if last else None
            xl = moe_layer(xl, norm2_g[i], mod, lat_row, router_w[j], router_b[j],
                           moe_w_gate[j], moe_w_up[j], moe_w_down[j], fin)
            if not last:
                xc = moe_layer(xc, norm2_g[i], mod, lambda tm: ctx_row, router_w[j], router_b[j],
                               moe_w_gate[j], moe_w_up[j], moe_w_down[j], None)
    if (DEPTH - 1) % 2 == 0:
        xl = final_rmsnorm(xl, final_norm_g)
    return xl.reshape(n_batch, SEQ, D_MODEL)
```

```python
import functools

import jax
import jax.numpy as jnp
from jax import lax
from jax.experimental import pallas as pl
from jax.experimental.pallas import tpu as pltpu

BF = jnp.bfloat16
F32 = jnp.float32

D_MODEL = 2048
SEQ = 2048
CTX_LEN = 256
DEPTH = 2
GRID_W = 64
GRID_ROWS = SEQ // GRID_W
NA_HEADS = 16
NA_HEAD_DIM = 64
NA_WIN_ROWS = 8
NA_WIN_COLS = 16
GLA_HEADS = 4
GLA_DK = 128
GLA_DV = 256
GLA_GATE_RANK = 16
GLA_TAU = 16.0
GQA_HEADS = 8
GQA_KV_HEADS = 2
GQA_HEAD_DIM = 128
GQA_GROUP = GQA_HEADS // GQA_KV_HEADS
ROPE_THETA = 10000.0
D_FF = 5632
N_EXPERTS = 8
TOP_K = 2
NORM_EPS = 1e-6
BRANCH_W = 1024
LOG2_E = 1.4426950408889634

C_NA_Q, C_NA_K, C_NA_V = 0, 1024, 2048
C_GLA_Q, C_GLA_K, C_GLA_V, C_GLA_OG = 3072, 3584, 4096, 5120
D_HEAD = 6144
C_GQA_Q, C_GQA_K, C_GQA_V = 0, 1024, 1280
C_GATE_A, C_GATE_B, C_GATE_C = 1536, 3584, 5632
C_LOW = 7680
D_TAIL = 8192
R_LOW = 6144
R_GQA_Q = 6176
R_END = 13856

V7X_LANES = 128
V7X_VMEM_LIMIT_BYTES = 56 * 1024 * 1024

GLA_STEP = 128
GLA_SUB = 16
GLA_HEADS_PER_STEP = 2
MOE_ROWS = 1024
MOE_SUB = 512
MOE_UP_TILE = 256
MOE_DOWN_TILE = 256
MOE_GATHER_ROWS = 256
MOE_CAST_CHUNK = 512


def _cparams(*sem):
    return pltpu.CompilerParams(dimension_semantics=sem,
                                vmem_limit_bytes=V7X_VMEM_LIMIT_BYTES)


def _dot(a, b):
    return jnp.dot(a, b, preferred_element_type=F32)


def _dot_nt(a, b):
    return lax.dot_general(a, b, (((1,), (1,)), ((), ())), preferred_element_type=F32)


def _dot_cast(a_ref, rows, w_ref, chunk):
    acc = None
    for k0 in range(0, w_ref.shape[0], chunk):
        part = _dot(a_ref[rows, k0:k0 + chunk], w_ref[k0:k0 + chunk, :].astype(BF))
        acc = part if acc is None else acc + part
    return acc


def _silu(x):
    return x * jax.nn.sigmoid(x)


def _latent_mod_row(tm):
    return lambda t: (t * tm) // SEQ


def _ctx_mod_row(n_batch):
    return lambda t: n_batch


def _ada_kernel(c_ref, w_ref, b_ref, o_ref):
    a = _silu(c_ref[...])
    o_ref[...] = _dot(a.astype(BF), w_ref[...].astype(BF)) + b_ref[...]


def ada_modulation(cvec, w_ada, b_ada):
    tn = 1024
    n6 = 6 * D_MODEL
    out = pl.pallas_call(
        _ada_kernel,
        grid=(DEPTH, n6 // tn),
        in_specs=[
            pl.BlockSpec((8, D_MODEL), lambda l, j: (0, 0)),
            pl.BlockSpec((None, D_MODEL, tn), lambda l, j: (l, 0, j)),
            pl.BlockSpec((None, 1, tn), lambda l, j: (l, 0, j)),
        ],
        out_specs=pl.BlockSpec((None, 8, tn), lambda l, j: (l, 0, j)),
        out_shape=jax.ShapeDtypeStruct((DEPTH, 8, n6), F32),
        compiler_params=_cparams("parallel", "parallel"),
        name="ada_modulation",
    )(cvec, w_ada, b_ada.reshape(DEPTH, 1, n6))
    return out.reshape(DEPTH, 8, 6, D_MODEL)


def _modulated_norm(x, g, mod, shift_idx, scale_idx):
    ms = jnp.mean(x * x, axis=-1, keepdims=True)
    y = x * lax.rsqrt(ms + NORM_EPS) * g
    return y * (1.0 + mod[scale_idx:scale_idx + 1]) + mod[shift_idx:shift_idx + 1]


def _norm_mod_kernel(x_ref, g_ref, mod_ref, o_ref, *, shift_idx, scale_idx):
    o_ref[...] = _modulated_norm(x_ref[...], g_ref[...], mod_ref[...],
                                 shift_idx, scale_idx).astype(o_ref.dtype)


def norm_mod(x, g, mod, mod_row, shift_idx, scale_idx, tm=512):
    m = x.shape[0]
    tm = min(tm, m)
    return pl.pallas_call(
        functools.partial(_norm_mod_kernel, shift_idx=shift_idx, scale_idx=scale_idx),
        grid=(m // tm,),
        in_specs=[
            pl.BlockSpec((tm, D_MODEL), lambda t: (t, 0)),
            pl.BlockSpec((1, D_MODEL), lambda t: (0, 0)),
            pl.BlockSpec((None, 6, D_MODEL), lambda t: (mod_row(t), 0, 0)),
        ],
        out_specs=pl.BlockSpec((tm, D_MODEL), lambda t: (t, 0)),
        out_shape=jax.ShapeDtypeStruct((m, D_MODEL), BF),
        compiler_params=_cparams("parallel"),
        name="norm_mod",
    )(x, g.reshape(1, D_MODEL), mod)


def _rmsnorm_kernel(x_ref, g_ref, o_ref):
    x = x_ref[...]
    ms = jnp.mean(x * x, axis=-1, keepdims=True)
    o_ref[...] = x * lax.rsqrt(ms + NORM_EPS) * g_ref[...]


def final_rmsnorm(x, g, tm=512):
    m = x.shape[0]
    return pl.pallas_call(
        _rmsnorm_kernel,
        grid=(m // tm,),
        in_specs=[pl.BlockSpec((tm, D_MODEL), lambda t: (t, 0)),
                  pl.BlockSpec((1, D_MODEL), lambda t: (0, 0))],
        out_specs=pl.BlockSpec((tm, D_MODEL), lambda t: (t, 0)),
        out_shape=jax.ShapeDtypeStruct((m, D_MODEL), F32),
        compiler_params=_cparams("parallel"),
        name="final_rmsnorm",
    )(x, g.reshape(1, D_MODEL))


def _mm_kernel(a_ref, w_ref, o_ref):
    o_ref[...] = _dot(a_ref[...], w_ref[...].astype(BF)).astype(o_ref.dtype)


def matmul(a, w, out_dtype, tm=1024, tn=512):
    m, k = a.shape
    n = w.shape[1]
    tm = min(tm, m)
    return pl.pallas_call(
        _mm_kernel,
        grid=(m // tm, n // tn),
        in_specs=[pl.BlockSpec((tm, k), lambda i, j: (i, 0)),
                  pl.BlockSpec((k, tn), lambda i, j: (0, j))],
        out_specs=pl.BlockSpec((tm, tn), lambda i, j: (i, j)),
        out_shape=jax.ShapeDtypeStruct((m, n), out_dtype),
        compiler_params=_cparams("parallel", "parallel"),
        name="matmul",
    )(a, w)


def _mm_cast_kernel(a_ref, w_ref, o_ref, *, chunk):
    o_ref[...] = _dot_cast(a_ref, slice(None), w_ref, chunk).astype(o_ref.dtype)


def matmul_leading_cols(a, w, layer, n_cols, out_dtype, tm=1024, tn=512):
    m, k = a.shape
    tm = min(tm, m)
    return pl.pallas_call(
        functools.partial(_mm_cast_kernel, chunk=512),
        grid=(m // tm, n_cols // tn),
        in_specs=[pl.BlockSpec((tm, k), lambda i, j: (i, 0)),
                  pl.BlockSpec((None, k, tn), lambda i, j: (layer, 0, j))],
        out_specs=pl.BlockSpec((tm, tn), lambda i, j: (i, j)),
        out_shape=jax.ShapeDtypeStruct((m, n_cols), out_dtype),
        compiler_params=_cparams("parallel", "parallel"),
        name="matmul_leading_cols",
    )(a, w)


def _mm_res_kernel(a_ref, w_ref, x_ref, mod_ref, o_ref, *, gate_idx):
    y = _dot(a_ref[...], w_ref[...].astype(BF))
    o_ref[...] = x_ref[...] + mod_ref[gate_idx:gate_idx + 1, :] * y


def matmul_residual(a, w, layer, x, mod, mod_row, gate_idx, tm=1024, tn=256):
    m, k = a.shape
    n = w.shape[2]
    tm = min(tm, m)
    return pl.pallas_call(
        functools.partial(_mm_res_kernel, gate_idx=gate_idx),
        grid=(m // tm, n // tn),
        in_specs=[pl.BlockSpec((tm, k), lambda i, j: (i, 0)),
                  pl.BlockSpec((None, k, tn), lambda i, j: (layer, 0, j)),
                  pl.BlockSpec((tm, tn), lambda i, j: (i, j)),
                  pl.BlockSpec((None, 6, tn), lambda i, j: (mod_row(i), 0, j))],
        out_specs=pl.BlockSpec((tm, tn), lambda i, j: (i, j)),
        out_shape=jax.ShapeDtypeStruct((m, n), F32),
        compiler_params=_cparams("parallel", "parallel"),
        name="matmul_residual",
    )(a, w, x, mod)


def _merge_kernel(a_ref, b_ref, c_ref, ga_ref, gb_ref, gc_ref, wa_ref, wb_ref, wc_ref, o_ref):
    def branch(x_ref, g_ref, w_ref):
        return jax.nn.sigmoid(g_ref[...].astype(F32)) * _dot(x_ref[...], w_ref[...].astype(BF))

    o_ref[...] = (branch(a_ref, ga_ref, wa_ref) + branch(b_ref, gb_ref, wb_ref)
                  + branch(c_ref, gc_ref, wc_ref)).astype(o_ref.dtype)


def merge_branches(a, b, c, proj, w_pa, w_pb, w_pc, layer, tm=1024, tn=512):
    m = a.shape[0]
    tm = min(tm, m)
    x_spec = pl.BlockSpec((tm, BRANCH_W), lambda i, j: (i, 0))
    w_spec = pl.BlockSpec((None, BRANCH_W, tn), lambda i, j: (layer, 0, j))

    def gate_spec(col):
        return pl.BlockSpec((tm, tn), lambda i, j: (i, col // tn + j))

    return pl.pallas_call(
        _merge_kernel,
        grid=(m // tm, D_MODEL // tn),
        in_specs=[x_spec, x_spec, x_spec,
                  gate_spec(C_GATE_A), gate_spec(C_GATE_B), gate_spec(C_GATE_C),
                  w_spec, w_spec, w_spec],
        out_specs=pl.BlockSpec((tm, tn), lambda i, j: (i, j)),
        out_shape=jax.ShapeDtypeStruct((m, D_MODEL), BF),
        compiler_params=_cparams("parallel", "parallel"),
        name="merge_branches",
    )(a, b, c, proj, proj, proj, w_pa, w_pb, w_pc)


def _ffn_up_kernel(h_ref, wg_ref, wu_ref, o_ref):
    h = h_ref[...]
    g = _dot(h, wg_ref[...].astype(BF))
    u = _dot(h, wu_ref[...].astype(BF))
    o_ref[...] = (_silu(g) * u).astype(o_ref.dtype)


def ffn_up(h, w_gate, w_up, tm=1024, tn=512):
    m = h.shape[0]
    tm = min(tm, m)
    return pl.pallas_call(
        _ffn_up_kernel,
        grid=(m // tm, D_FF // tn),
        in_specs=[pl.BlockSpec((tm, D_MODEL), lambda i, j: (i, 0)),
                  pl.BlockSpec((D_MODEL, tn), lambda i, j: (0, j)),
                  pl.BlockSpec((D_MODEL, tn), lambda i, j: (0, j))],
        out_specs=pl.BlockSpec((tm, tn), lambda i, j: (i, j)),
        out_shape=jax.ShapeDtypeStruct((m, D_FF), BF),
        compiler_params=_cparams("parallel", "parallel"),
        name="ffn_up",
    )(h, w_gate, w_up)


def _gqa_prep_kernel(q_ref, k_ref, cos_ref, sin_ref, qg_ref, kg_ref, qo_ref, ko_ref):
    cos = cos_ref[...]
    sin = sin_ref[...]
    lane = lax.broadcasted_iota(jnp.int32, cos.shape, 1)
    even = (lane & 1) == 0

    def prep(x_ref, g_ref, o_ref, heads):
        for h in range(heads):
            sl = slice(h * GQA_HEAD_DIM, (h + 1) * GQA_HEAD_DIM)
            x = x_ref[:, sl].astype(F32)
            ms = jnp.mean(x * x, axis=-1, keepdims=True)
            y = x * lax.rsqrt(ms + NORM_EPS) * g_ref[...]
            swapped = jnp.where(even, pltpu.roll(y, GQA_HEAD_DIM - 1, 1), pltpu.roll(y, 1, 1))
            o_ref[:, sl] = (y * cos + swapped * sin).astype(o_ref.dtype)

    prep(q_ref, qg_ref, qo_ref, GQA_HEADS)
    prep(k_ref, kg_ref, ko_ref, GQA_KV_HEADS)


def gqa_prep(proj, cos_t, sin_t, qn_g, kn_g, table_block, tm=256):
    m = proj.shape[0]
    qw = GQA_HEADS * GQA_HEAD_DIM
    kw = GQA_KV_HEADS * GQA_HEAD_DIM
    return pl.pallas_call(
        _gqa_prep_kernel,
        grid=(m // tm,),
        in_specs=[pl.BlockSpec((tm, qw), lambda t: (t, C_GQA_Q // qw)),
                  pl.BlockSpec((tm, kw), lambda t: (t, C_GQA_K // kw)),
                  pl.BlockSpec((tm, GQA_HEAD_DIM), lambda t: (table_block(t), 0)),
                  pl.BlockSpec((tm, GQA_HEAD_DIM), lambda t: (table_block(t), 0)),
                  pl.BlockSpec((1, GQA_HEAD_DIM), lambda t: (0, 0)),
                  pl.BlockSpec((1, GQA_HEAD_DIM), lambda t: (0, 0))],
        out_specs=[pl.BlockSpec((tm, qw), lambda t: (t, 0)),
                   pl.BlockSpec((tm, kw), lambda t: (t, 0))],
        out_shape=[jax.ShapeDtypeStruct((m, qw), BF), jax.ShapeDtypeStruct((m, kw), BF)],
        compiler_params=_cparams("parallel"),
        name="gqa_prep",
    )(proj, proj, cos_t, sin_t, qn_g.reshape(1, -1), kn_g.reshape(1, -1))


def rope_tables():
    half = GQA_HEAD_DIM // 2
    freqs = ROPE_THETA ** (-jnp.arange(0, half, 2, dtype=F32) / half)
    t = jnp.arange(SEQ)
    row = (t // GRID_W).astype(F32)
    col = (t % GRID_W).astype(F32)
    ang = jnp.concatenate([row[:, None] * freqs, col[:, None] * freqs], axis=-1)
    cos = jnp.repeat(jnp.cos(ang), 2, axis=-1)
    sin = jnp.repeat(jnp.sin(ang), 2, axis=-1)
    sign = jnp.tile(jnp.array([-1.0, 1.0], F32), half)
    cos = jnp.concatenate([cos, jnp.ones((256, GQA_HEAD_DIM), F32)], axis=0)
    sin = jnp.concatenate([sin * sign, jnp.zeros((256, GQA_HEAD_DIM), F32)], axis=0)
    return cos, sin


def _attend(q, kv_list, scale):
    scores = []
    for k, _, bias in kv_list:
        s = _dot_nt(q, k) * scale
        if bias is not None:
            s = s + bias
        scores.append(s)
    m = functools.reduce(jnp.maximum, [jnp.max(s, axis=-1, keepdims=True) for s in scores])
    ps = [jnp.exp(s - m) for s in scores]
    den = functools.reduce(jnp.add, [jnp.sum(p, axis=-1, keepdims=True) for p in ps])
    o = functools.reduce(jnp.add, [_dot(p.astype(BF), v) for p, (_, v, _) in zip(ps, kv_list)])
    return o / den


def _head_pair_rows(q2):
    lane = lax.broadcasted_iota(jnp.int32, q2.shape, 1)
    first = lane < NA_HEAD_DIM
    zero = jnp.zeros_like(q2)
    return jnp.concatenate([jnp.where(first, q2, zero), jnp.where(first, zero, q2)], axis=0)


def _head_pair_merge(o, m):
    lane = lax.broadcasted_iota(jnp.int32, (m, 2 * NA_HEAD_DIM), 1)
    return jnp.where(lane < NA_HEAD_DIM, o[:m], o[m:])


def _na_kernel(q_ref, kl_ref, vl_ref, kc_ref, vc_ref, bias_ref, o_ref):
    r = pl.program_id(1)
    start = jnp.clip(r - NA_WIN_ROWS // 2, 0, GRID_ROWS - NA_WIN_ROWS) * GRID_W
    start = pl.multiple_of(start, GRID_W)
    n_loc = NA_WIN_ROWS * GRID_W
    scale = NA_HEAD_DIM ** -0.5
    for hp in range(NA_HEADS // 2):
        sl = slice(hp * 128, (hp + 1) * 128)
        qq = _head_pair_rows(q_ref[:, sl])
        bias = jnp.concatenate([bias_ref[2 * hp], bias_ref[2 * hp + 1]], axis=0)
        o = _attend(qq, [(kl_ref[pl.ds(start, n_loc), sl], vl_ref[pl.ds(start, n_loc), sl], bias),
                         (kc_ref[:, sl], vc_ref[:, sl], None)], scale)
        o_ref[:, sl] = _head_pair_merge(o, GRID_W).astype(o_ref.dtype)


def na_latent(proj_l, proj_c, bias_tab, n_batch):
    w = NA_HEADS * NA_HEAD_DIM

    def pattern(r):
        return jnp.where(r < 4, r, jnp.where(r > GRID_ROWS - 4, r - (GRID_ROWS - NA_WIN_ROWS), 4))

    return pl.pallas_call(
        _na_kernel,
        grid=(n_batch, GRID_ROWS),
        in_specs=[pl.BlockSpec((GRID_W, w), lambda b, r: (b * GRID_ROWS + r, C_NA_Q // w)),
                  pl.BlockSpec((SEQ, w), lambda b, r: (b, C_NA_K // w)),
                  pl.BlockSpec((SEQ, w), lambda b, r: (b, C_NA_V // w)),
                  pl.BlockSpec((CTX_LEN, w), lambda b, r: (b, C_NA_K // w)),
                  pl.BlockSpec((CTX_LEN, w), lambda b, r: (b, C_NA_V // w)),
                  pl.BlockSpec((None, NA_HEADS, GRID_W, NA_WIN_ROWS * GRID_W),
                               lambda b, r: (pattern(r), 0, 0, 0))],
        out_specs=pl.BlockSpec((GRID_W, w), lambda b, r: (b * GRID_ROWS + r, 0)),
        out_shape=jax.ShapeDtypeStruct((n_batch * SEQ, w), BF),
        compiler_params=_cparams("parallel", "arbitrary"),
        name="na_latent",
    )(proj_l, proj_l, proj_l, proj_c, proj_c, bias_tab)


def na_bias_table(rpb):
    cols = jnp.arange(GRID_W)
    col_start = jnp.clip(cols - NA_WIN_COLS // 2, 0, GRID_W - NA_WIN_COLS)
    in_win = (cols[None, :] >= col_start[:, None]) & (cols[None, :] < col_start[:, None] + NA_WIN_COLS)
    dc = jnp.clip(cols[None, :] - cols[:, None] + NA_WIN_COLS - 1, 0, 2 * NA_WIN_COLS - 2)
    onehot = (dc[None] == jnp.arange(2 * NA_WIN_COLS - 1)[:, None, None]).astype(F32)
    by_col = jnp.einsum('hdc,cqk->hdqk', rpb.astype(F32), onehot, precision=lax.Precision.HIGHEST)
    by_col = jnp.where(in_win[None, None], by_col, -jnp.inf)
    last = NA_WIN_ROWS - 1
    bias = jnp.stack([by_col[:, last - p:last - p + NA_WIN_ROWS] for p in range(NA_WIN_ROWS)])
    bias = bias.transpose(0, 1, 3, 2, 4)
    return bias.reshape(NA_WIN_ROWS, NA_HEADS, GRID_W, NA_WIN_ROWS * GRID_W)


def _na_ctx_kernel(q_ref, k_ref, v_ref, o_ref):
    scale = NA_HEAD_DIM ** -0.5
    for hp in range(NA_HEADS // 2):
        sl = slice(hp * 128, (hp + 1) * 128)
        o = _attend(_head_pair_rows(q_ref[:, sl]), [(k_ref[:, sl], v_ref[:, sl], None)], scale)
        o_ref[:, sl] = _head_pair_merge(o, CTX_LEN).astype(o_ref.dtype)


def na_context(proj_c, n_batch):
    w = NA_HEADS * NA_HEAD_DIM
    return pl.pallas_call(
        _na_ctx_kernel,
        grid=(n_batch,),
        in_specs=[pl.BlockSpec((CTX_LEN, w), lambda b: (b, C_NA_Q // w)),
                  pl.BlockSpec((CTX_LEN, w), lambda b: (b, C_NA_K // w)),
                  pl.BlockSpec((CTX_LEN, w), lambda b: (b, C_NA_V // w))],
        out_specs=pl.BlockSpec((CTX_LEN, w), lambda b: (b, 0)),
        out_shape=jax.ShapeDtypeStruct((n_batch * CTX_LEN, w), BF),
        compiler_params=_cparams("parallel"),
        name="na_context",
    )(proj_c, proj_c, proj_c)


def _gqa_rows(q_ref):
    return jnp.concatenate([q_ref[:, g * GQA_HEAD_DIM:(g + 1) * GQA_HEAD_DIM]
                            for g in range(GQA_GROUP)], axis=0)


def _gqa_store(o, o_ref):
    tq = o_ref.shape[0]
    for g in range(GQA_GROUP):
        o_ref[:, g * GQA_HEAD_DIM:(g + 1) * GQA_HEAD_DIM] = o[g * tq:(g + 1) * tq].astype(o_ref.dtype)


def _gqa_kernel(q_ref, kl_ref, vl_ref, kc_ref, vc_ref, o_ref):
    for g in range(GQA_GROUP):
        sl = slice(g * GQA_HEAD_DIM, (g + 1) * GQA_HEAD_DIM)
        o = _attend(q_ref[:, sl], [(kl_ref[...], vl_ref[...], None), (kc_ref[...], vc_ref[...], None)],
                    GQA_HEAD_DIM ** -0.5)
        o_ref[:, sl] = o.astype(o_ref.dtype)


def _gqa_ctx_kernel(q_ref, kc_ref, vc_ref, o_ref):
    o = _attend(_gqa_rows(q_ref), [(kc_ref[...], vc_ref[...], None)], GQA_HEAD_DIM ** -0.5)
    _gqa_store(o, o_ref)


def gqa_latent(q_l, k_l, k_c, proj_l, proj_c, n_batch, tq=512):
    gw = GQA_GROUP * GQA_HEAD_DIM
    nq = SEQ // tq
    dh = GQA_HEAD_DIM
    return pl.pallas_call(
        _gqa_kernel,
        grid=(n_batch, GQA_KV_HEADS, nq),
        in_specs=[pl.BlockSpec((tq, gw), lambda b, h, i: (b * nq + i, h)),
                  pl.BlockSpec((SEQ, dh), lambda b, h, i: (b, h)),
                  pl.BlockSpec((SEQ, dh), lambda b, h, i: (b, C_GQA_V // dh + h)),
                  pl.BlockSpec((CTX_LEN, dh), lambda b, h, i: (b, h)),
                  pl.BlockSpec((CTX_LEN, dh), lambda b, h, i: (b, C_GQA_V // dh + h))],
        out_specs=pl.BlockSpec((tq, gw), lambda b, h, i: (b * nq + i, h)),
        out_shape=jax.ShapeDtypeStruct((n_batch * SEQ, GQA_HEADS * dh), BF),
        compiler_params=_cparams("parallel", "parallel", "arbitrary"),
        name="gqa_latent",
    )(q_l, k_l, proj_l, k_c, proj_c)


def gqa_context(q_c, k_c, proj_c, n_batch):
    gw = GQA_GROUP * GQA_HEAD_DIM
    dh = GQA_HEAD_DIM
    return pl.pallas_call(
        _gqa_ctx_kernel,
        grid=(n_batch, GQA_KV_HEADS),
        in_specs=[pl.BlockSpec((CTX_LEN, gw), lambda b, h: (b, h)),
                  pl.BlockSpec((CTX_LEN, dh), lambda b, h: (b, h)),
                  pl.BlockSpec((CTX_LEN, dh), lambda b, h: (b, C_GQA_V // dh + h))],
        out_specs=pl.BlockSpec((CTX_LEN, gw), lambda b, h: (b, h)),
        out_shape=jax.ShapeDtypeStruct((n_batch * CTX_LEN, GQA_HEADS * dh), BF),
        compiler_params=_cparams("parallel", "parallel"),
        name="gqa_context",
    )(q_c, k_c, proj_c)


def _log_sigmoid(z):
    return jnp.minimum(z, 0.0) - jnp.log1p(jnp.exp(-jnp.abs(z)))


def _gla_step(c, direction, head, q_s, k_s, v_s, low_s, wa_ref, ba_ref, st_ref, o_s):
    n = GLA_STEP
    r0 = pl.multiple_of(c * n, n)
    rows = pl.ds(r0, n)
    k_lanes = slice(head * GLA_DK, (head + 1) * GLA_DK)
    v_lanes = slice(head * GLA_DV, (head + 1) * GLA_DV)
    qf = q_s[rows, k_lanes].astype(F32) * (GLA_DK ** -0.5)
    k_bf = k_s[rows, k_lanes]
    kf = k_bf.astype(F32)
    v = v_s[rows, v_lanes]
    z = _dot(low_s[rows, :], wa_ref[direction, head].astype(BF)) + ba_ref[direction, head]
    g = _log_sigmoid(z) * (1.0 / GLA_TAU)

    ri = lax.broadcasted_iota(jnp.int32, (n, n), 0)
    ci = lax.broadcasted_iota(jnp.int32, (n, n), 1)
    tri = (ci <= ri) if direction == 0 else (ci >= ri)
    tri = jnp.where(tri, 1.0, 0.0).astype(BF)
    g_hi = g.astype(BF)
    g_lo = (g - g_hi.astype(F32)).astype(BF)
    bc = (_dot(tri, g_hi) + _dot(tri, g_lo)) * LOG2_E
    btot = bc[n - 1:n] if direction == 0 else bc[0:1]

    state = st_ref[head]
    q_in = (qf * jnp.exp2(bc)).astype(BF)
    o = _dot_nt(q_in, state.astype(BF))
    k_out = (kf * jnp.exp2(btot - bc)).astype(BF)
    v_t = v.astype(F32).T.astype(BF)
    st_ref[head] = jnp.exp2(btot) * state + _dot(v_t, k_out)

    across = None
    size = n // 2
    while size >= GLA_SUB:
        pieces = []
        for base in range(0, n, 2 * size):
            mid = base + size
            if direction == 0:
                early, late, ref = slice(base, mid), slice(mid, mid + size), bc[mid - 1:mid]
            else:
                early, late, ref = slice(mid, mid + size), slice(base, mid), bc[mid:mid + 1]
            k_dec = (kf[early] * jnp.exp2(ref - bc[early])).astype(BF)
            k_rows = [jnp.zeros((early.start, GLA_DK), BF)] if early.start else []
            k_rows.append(k_dec)
            if early.stop < n:
                k_rows.append(jnp.zeros((n - early.stop, GLA_DK), BF))
            part = _dot_nt((qf[late] * jnp.exp2(bc[late] - ref)).astype(BF),
                           jnp.concatenate(k_rows, axis=0))
            none = jnp.zeros((size, n), F32)
            pieces += [none, part] if direction == 0 else [part, none]
        level = jnp.concatenate(pieces, axis=0)
        across = level if across is None else across + level
        size //= 2

    lane = lax.broadcasted_iota(jnp.int32, (GLA_SUB, n), 1)
    sub_row = lax.broadcasted_iota(jnp.int32, (GLA_SUB, 1), 0)
    neg_inf = jnp.float32(-jnp.inf)
    blocks = []
    for blk in range(n // GLA_SUB):
        lo, hi = blk * GLA_SUB, (blk + 1) * GLA_SUB
        q_b, b_b, a_b = qf[lo:hi], bc[lo:hi], across[lo:hi]
        decayed = []
        for jl in range(GLA_SUB):
            keep = (sub_row >= jl) if direction == 0 else (sub_row <= jl)
            dec = jnp.exp2(jnp.where(keep, b_b - bc[lo + jl:lo + jl + 1], neg_inf))
            decayed.append((q_b * dec).astype(BF))
        pair = _dot_nt(jnp.concatenate(decayed, axis=0), k_bf)
        for jl in range(GLA_SUB):
            a_b = jnp.where(lane == lo + jl, pair[jl * GLA_SUB:(jl + 1) * GLA_SUB], a_b)
        blocks.append(a_b)
    attn = jnp.concatenate(blocks, axis=0).astype(BF)
    o_s[rows, v_lanes] = o + _dot(attn, v)


def _gla_kernel(ql, kl, vl, ogl, lowl, qc, kc, vc, ogc, lowc, wa_ref, ba_ref, ng_ref,
                yl_ref, yc_ref, q_s, k_s, v_s, low_s, of_s, ob_s, stf, stb):
    nc = CTX_LEN
    q_s[0:nc, :] = qc[...]
    q_s[nc:, :] = ql[...]
    k_s[0:nc, :] = kc[...]
    k_s[nc:, :] = kl[...]
    v_s[0:nc, :] = vc[...]
    v_s[nc:, :] = vl[...]
    low_s[0:nc, :] = lowc[...]
    low_s[nc:, :] = lowl[...]
    stf[...] = jnp.zeros_like(stf)
    stb[...] = jnp.zeros_like(stb)
    n_ctx = CTX_LEN // GLA_STEP
    n_all = (CTX_LEN + SEQ) // GLA_STEP

    def body(i, carry):
        cb = jnp.where(i < n_ctx, n_ctx - 1 - i, n_all + n_ctx - 1 - i)
        for head in range(GLA_HEADS_PER_STEP):
            _gla_step(i, 0, head, q_s, k_s, v_s, low_s, wa_ref, ba_ref, stf, of_s)
            _gla_step(cb, 1, head, q_s, k_s, v_s, low_s, wa_ref, ba_ref, stb, ob_s)
        return carry

    lax.fori_loop(0, n_all, body, 0)

    def finish(lo, hi, og_ref, y_ref):
        for head in range(GLA_HEADS_PER_STEP):
            lanes = slice(head * GLA_DV, (head + 1) * GLA_DV)
            o = of_s[lo:hi, lanes] + ob_s[lo:hi, lanes]
            ms = jnp.mean(o * o, axis=-1, keepdims=True)
            y = o * lax.rsqrt(ms + NORM_EPS) * ng_ref[...]
            y_ref[:, lanes] = (y * _silu(og_ref[:, lanes].astype(F32))).astype(y_ref.dtype)

    finish(0, nc, ogc, yc_ref)
    finish(nc, nc + SEQ, ogl, yl_ref)


def gla_bidirectional(head_l, head_c, tail_l, tail_c, w_a2, b_a, norm_g, n_batch):
    hps = GLA_HEADS_PER_STEP
    dk, dv = GLA_DK, GLA_DV
    kw, vw = hps * dk, hps * dv
    wa = jnp.zeros((2, GLA_HEADS, V7X_LANES, dk), F32)
    for d in range(2):
        wa = wa.at[d, :, d * GLA_GATE_RANK:(d + 1) * GLA_GATE_RANK, :].set(
            w_a2[d].reshape(GLA_GATE_RANK, GLA_HEADS, dk).transpose(1, 0, 2))
    ba = b_a.reshape(2, GLA_HEADS, 1, dk)

    def specs(rows):
        return [pl.BlockSpec((rows, kw), lambda b, h: (b, C_GLA_Q // kw + h)),
                pl.BlockSpec((rows, kw), lambda b, h: (b, C_GLA_K // kw + h)),
                pl.BlockSpec((rows, vw), lambda b, h: (b, C_GLA_V // vw + h)),
                pl.BlockSpec((rows, vw), lambda b, h: (b, C_GLA_OG // vw + h)),
                pl.BlockSpec((rows, V7X_LANES), lambda b, h: (b, C_LOW // V7X_LANES))]

    n_rows = CTX_LEN + SEQ
    return pl.pallas_call(
        _gla_kernel,
        grid=(n_batch, GLA_HEADS // hps),
        in_specs=specs(SEQ) + specs(CTX_LEN) + [
            pl.BlockSpec((2, hps, V7X_LANES, dk), lambda b, h: (0, h, 0, 0)),
            pl.BlockSpec((2, hps, 1, dk), lambda b, h: (0, h, 0, 0)),
            pl.BlockSpec((1, dv), lambda b, h: (0, 0))],
        out_specs=[pl.BlockSpec((SEQ, vw), lambda b, h: (b, h)),
                   pl.BlockSpec((CTX_LEN, vw), lambda b, h: (b, h))],
        out_shape=[jax.ShapeDtypeStruct((n_batch * SEQ, GLA_HEADS * dv), BF),
                   jax.ShapeDtypeStruct((n_batch * CTX_LEN, GLA_HEADS * dv), BF)],
        scratch_shapes=[pltpu.VMEM((n_rows, kw), BF), pltpu.VMEM((n_rows, kw), BF),
                        pltpu.VMEM((n_rows, vw), BF), pltpu.VMEM((n_rows, V7X_LANES), BF),
                        pltpu.VMEM((n_rows, vw), F32), pltpu.VMEM((n_rows, vw), F32),
                        pltpu.VMEM((hps, dv, dk), F32), pltpu.VMEM((hps, dv, dk), F32)],
        compiler_params=_cparams("parallel", "parallel"),
        name="gla_bidirectional",
    )(*([head_l] * 4 + [tail_l] + [head_c] * 4 + [tail_c]), wa, ba, norm_g.reshape(1, dv))


def _router_kernel(x_ref, g_ref, mod_ref, rw_ref, rb_ref, h_ref, idx_ref, wt_ref):
    h = _modulated_norm(x_ref[...], g_ref[...], mod_ref[...], 3, 4)
    h_ref[...] = h.astype(h_ref.dtype)
    logits = jnp.dot(h, rw_ref[...], preferred_element_type=F32,
                     precision=lax.Precision.HIGHEST) + rb_ref[...]
    lane = lax.broadcasted_iota(jnp.int32, logits.shape, 1)
    neg_inf = jnp.float32(-jnp.inf)
    logits = jnp.where(lane < N_EXPERTS, logits, neg_inf)
    m1 = jnp.max(logits, axis=-1, keepdims=True)
    lane_f = lane.astype(F32)
    i1 = jnp.min(jnp.where(logits == m1, lane_f, float(V7X_LANES)), axis=-1, keepdims=True)
    rest = jnp.where(lane_f == i1, neg_inf, logits)
    m2 = jnp.max(rest, axis=-1, keepdims=True)
    i2 = jnp.min(jnp.where(rest == m2, lane_f, float(V7X_LANES)), axis=-1, keepdims=True)
    e = jnp.exp(m2 - m1)
    w1 = 1.0 / (1.0 + e)
    idx_ref[...] = jnp.where(lane == 0, i1, jnp.where(lane == 1, i2, 0.0)).astype(jnp.int32)
    wt_ref[...] = jnp.where(lane == 0, w1, jnp.where(lane == 1, e * w1, 0.0))


def moe_router(x, g, mod, mod_row, router_w, router_b, tm=512):
    m = x.shape[0]
    rw = jnp.zeros((D_MODEL, V7X_LANES), F32).at[:, :N_EXPERTS].set(router_w)
    rb = jnp.zeros((1, V7X_LANES), F32).at[0, :N_EXPERTS].set(router_b)
    return pl.pallas_call(
        _router_kernel,
        grid=(m // tm,),
        in_specs=[pl.BlockSpec((tm, D_MODEL), lambda t: (t, 0)),
                  pl.BlockSpec((1, D_MODEL), lambda t: (0, 0)),
                  pl.BlockSpec((None, 6, D_MODEL), lambda t: (mod_row(t), 0, 0)),
                  pl.BlockSpec((D_MODEL, V7X_LANES), lambda t: (0, 0)),
                  pl.BlockSpec((1, V7X_LANES), lambda t: (0, 0))],
        out_specs=[pl.BlockSpec((tm, D_MODEL), lambda t: (t, 0)),
                   pl.BlockSpec((tm, V7X_LANES), lambda t: (t, 0)),
                   pl.BlockSpec((tm, V7X_LANES), lambda t: (t, 0))],
        out_shape=[jax.ShapeDtypeStruct((m, D_MODEL), F32),
                   jax.ShapeDtypeStruct((m, V7X_LANES), jnp.int32),
                   jax.ShapeDtypeStruct((m, V7X_LANES), F32)],
        compiler_params=_cparams("parallel"),
        name="moe_router",
    )(x, g.reshape(1, D_MODEL), mod, rw, rb)


def _row_copy(src_hbm, src_row, dst, dst_row, sem):
    return pltpu.make_async_copy(src_hbm.at[pl.ds(src_row, 1)], dst.at[pl.ds(dst_row, 1)], sem)


def _gather_tile(src_hbm, idx_ref, n_rows, per_row, buf, sem, action):
    def body(i, carry):
        for k in range(per_row):
            cp = _row_copy(src_hbm, idx_ref[0, per_row * i + k], buf.at[k], i, sem)
            cp.start() if action == "start" else cp.wait()
        return carry

    lax.fori_loop(0, n_rows, body, 0)


def _dispatch_kernel(nv_ref, idx_ref, idx_next_ref, h_hbm, o_ref, buf, sem):
    t = pl.program_id(0)
    rows = o_ref.shape[0]
    slot = t % 2

    def start_tile(tile_idx_ref, tile, s):
        nv = nv_ref[tile]

        @pl.when(nv < rows)
        def _():
            buf[s] = jnp.zeros(buf.shape[1:], buf.dtype)

        _gather_tile(h_hbm, tile_idx_ref, nv, 1, buf.at[s], sem.at[s], "start")

    @pl.when(t == 0)
    def _():
        start_tile(idx_ref, 0, 0)

    @pl.when(t + 1 < pl.num_programs(0))
    def _():
        start_tile(idx_next_ref, t + 1, 1 - slot)

    _gather_tile(h_hbm, idx_ref, nv_ref[t], 1, buf.at[slot], sem.at[slot], "wait")
    o_ref[...] = buf[slot, 0].astype(o_ref.dtype)


def moe_dispatch(h, slot_t, tile_valid):
    n_slots = slot_t.shape[0]
    rows = MOE_GATHER_ROWS
    n_tiles = n_slots // rows
    idx = slot_t.reshape(n_tiles, 1, rows)
    grid_spec = pltpu.PrefetchScalarGridSpec(
        num_scalar_prefetch=1,
        grid=(n_tiles,),
        in_specs=[pl.BlockSpec((None, 1, rows), lambda t, nv: (t, 0, 0), memory_space=pltpu.SMEM),
                  pl.BlockSpec((None, 1, rows), lambda t, nv: (jnp.minimum(t + 1, n_tiles - 1), 0, 0),
                               memory_space=pltpu.SMEM),
                  pl.BlockSpec(memory_space=pl.ANY)],
        out_specs=pl.BlockSpec((rows, D_MODEL), lambda t, nv: (t, 0)),
        scratch_shapes=[pltpu.VMEM((2, 1, rows, D_MODEL), F32), pltpu.SemaphoreType.DMA((2,))],
    )
    return pl.pallas_call(
        _dispatch_kernel,
        grid_spec=grid_spec,
        out_shape=jax.ShapeDtypeStruct((n_slots, D_MODEL), BF),
        compiler_params=_cparams("arbitrary"),
        name="moe_dispatch",
    )(tile_valid, idx, idx, h)


def _moe_ffn_kernel(be_ref, nv_ref, nu_ref, x_ref, wg_ref, wu_ref, wd_ref, o_ref, act):
    s = pl.program_id(1)
    nv = nv_ref[pl.program_id(0)]
    n_up = D_FF // MOE_UP_TILE
    full = nv > MOE_SUB
    half = jnp.logical_and(nv > 0, nv <= MOE_SUB)

    @pl.when(s < n_up)
    def _():
        cols = pl.ds(pl.multiple_of(s * MOE_UP_TILE, MOE_UP_TILE), MOE_UP_TILE)

        def up(n_rows):
            rows = slice(0, n_rows)
            g = _dot_cast(x_ref, rows, wg_ref, MOE_CAST_CHUNK)
            u = _dot_cast(x_ref, rows, wu_ref, MOE_CAST_CHUNK)
            act[rows, cols] = (_silu(g) * u).astype(BF)

        pl.when(full)(lambda: up(MOE_ROWS))
        pl.when(half)(lambda: up(MOE_SUB))

    @pl.when(s >= n_up)
    def _():
        @pl.when(full)
        def _():
            o_ref[...] = _dot_cast(act, slice(0, MOE_ROWS), wd_ref, MOE_CAST_CHUNK)

        @pl.when(half)
        def _():
            o_ref[0:MOE_SUB, :] = _dot_cast(act, slice(0, MOE_SUB), wd_ref, MOE_CAST_CHUNK)
            o_ref[MOE_SUB:, :] = jnp.zeros((MOE_ROWS - MOE_SUB, MOE_DOWN_TILE), F32)

        @pl.when(nv == 0)
        def _():
            o_ref[...] = jnp.zeros_like(o_ref)


def moe_expert_ffn(xb, block_e, n_valid, n_used, w_gate, w_up, w_down):
    n_blk = xb.shape[0] // MOE_ROWS
    n_up = D_FF // MOE_UP_TILE
    n_down = D_MODEL // MOE_DOWN_TILE

    def last_used(b, nu):
        return jnp.minimum(b, nu[0] - 1)

    def up_idx(b, s, nu):
        return jnp.where(b < nu[0], jnp.minimum(s, n_up - 1), n_up - 1)

    def down_idx(b, s, nu):
        return jnp.where(b < nu[0], jnp.maximum(s - n_up, 0), n_down - 1)

    grid_spec = pltpu.PrefetchScalarGridSpec(
        num_scalar_prefetch=3,
        grid=(n_blk, n_up + n_down),
        in_specs=[
            pl.BlockSpec((MOE_ROWS, D_MODEL), lambda b, s, be, nv, nu: (last_used(b, nu), 0)),
            pl.BlockSpec((None, D_MODEL, MOE_UP_TILE),
                         lambda b, s, be, nv, nu: (be[last_used(b, nu)], 0, up_idx(b, s, nu))),
            pl.BlockSpec((None, D_MODEL, MOE_UP_TILE),
                         lambda b, s, be, nv, nu: (be[last_used(b, nu)], 0, up_idx(b, s, nu))),
            pl.BlockSpec((None, D_FF, MOE_DOWN_TILE),
                         lambda b, s, be, nv, nu: (be[last_used(b, nu)], 0, down_idx(b, s, nu))),
        ],
        out_specs=pl.BlockSpec((MOE_ROWS, MOE_DOWN_TILE),
                               lambda b, s, be, nv, nu: (b, jnp.maximum(s - n_up, 0))),
        scratch_shapes=[pltpu.VMEM((MOE_ROWS, D_FF), BF)],
    )
    return pl.pallas_call(
        _moe_ffn_kernel,
        grid_spec=grid_spec,
        out_shape=jax.ShapeDtypeStruct((n_blk * MOE_ROWS, D_MODEL), F32),
        compiler_params=_cparams("arbitrary", "arbitrary"),
        name="moe_expert_ffn",
    )(block_e, n_valid, n_used, xb, w_gate, w_up, w_down)


def _moe_combine_kernel(idx_ref, idx_next_ref, x_ref, wt_ref, mod_ref, g_ref, yb_hbm, o_ref, buf, sem,
                        *, final_norm):
    t = pl.program_id(0)
    tm = x_ref.shape[0]
    slot = t % 2

    @pl.when(t == 0)
    def _():
        _gather_tile(yb_hbm, idx_ref, tm, TOP_K, buf.at[0], sem.at[0], "start")

    @pl.when(t + 1 < pl.num_programs(0))
    def _():
        _gather_tile(yb_hbm, idx_next_ref, tm, TOP_K, buf.at[1 - slot], sem.at[1 - slot], "start")

    _gather_tile(yb_hbm, idx_ref, tm, TOP_K, buf.at[slot], sem.at[slot], "wait")
    w = wt_ref[...]
    y = w[:, 0:1] * buf[slot, 0] + w[:, 1:2] * buf[slot, 1]
    out = x_ref[...] + mod_ref[5:6, :] * y
    if final_norm:
        ms = jnp.mean(out * out, axis=-1, keepdims=True)
        out = out * lax.rsqrt(ms + NORM_EPS) * g_ref[...]
    o_ref[...] = out


def moe_combine(x, yb, dest, wt, mod, mod_row, final_g, tm=256):
    m = x.shape[0]
    tm = min(tm, m)
    n_tiles = m // tm
    row = pl.BlockSpec((tm, D_MODEL), lambda t: (t, 0))
    final_norm = final_g is not None
    g = final_g if final_norm else jnp.ones((D_MODEL,), F32)
    idx = dest.reshape(n_tiles, 1, TOP_K * tm)
    return pl.pallas_call(
        functools.partial(_moe_combine_kernel, final_norm=final_norm),
        grid=(n_tiles,),
        in_specs=[pl.BlockSpec((None, 1, TOP_K * tm), lambda t: (t, 0, 0), memory_space=pltpu.SMEM),
                  pl.BlockSpec((None, 1, TOP_K * tm), lambda t: (jnp.minimum(t + 1, n_tiles - 1), 0, 0),
                               memory_space=pltpu.SMEM),
                  row,
                  pl.BlockSpec((tm, V7X_LANES), lambda t: (t, 0)),
                  pl.BlockSpec((None, 6, D_MODEL), lambda t: (mod_row(t), 0, 0)),
                  pl.BlockSpec((1, D_MODEL), lambda t: (0, 0)),
                  pl.BlockSpec(memory_space=pl.ANY)],
        out_specs=row,
        out_shape=jax.ShapeDtypeStruct((m, D_MODEL), F32),
        scratch_shapes=[pltpu.VMEM((2, TOP_K, tm, D_MODEL), F32), pltpu.SemaphoreType.DMA((2,))],
        compiler_params=_cparams("arbitrary"),
        name="moe_combine",
    )(idx, idx, x, wt, mod, g.reshape(1, D_MODEL), yb)


def moe_layout(top_i):
    t = top_i.shape[0]
    n_assign = t * TOP_K
    flat_e = top_i.reshape(-1)
    onehot = (flat_e[:, None] == jnp.arange(N_EXPERTS, dtype=jnp.int32)[None, :]).astype(jnp.int32)
    csum = jnp.cumsum(onehot, axis=0)
    rank = jnp.take_along_axis(csum, flat_e[:, None], axis=1)[:, 0] - 1
    counts = csum[-1]
    padded = (counts + MOE_ROWS - 1) // MOE_ROWS * MOE_ROWS
    pends = jnp.cumsum(padded)
    pstarts = pends - padded
    dest = pstarts[flat_e] + rank
    n_blk = n_assign // MOE_ROWS + N_EXPERTS
    flat_t = jnp.repeat(jnp.arange(t, dtype=jnp.int32), TOP_K)
    slot_t = jnp.zeros((n_blk * MOE_ROWS,), jnp.int32).at[dest].set(flat_t)
    blk = jnp.arange(n_blk, dtype=jnp.int32)
    blk_start = blk * MOE_ROWS
    block_e = jnp.minimum(jnp.sum((pends[None, :] <= blk_start[:, None]).astype(jnp.int32), axis=1),
                          N_EXPERTS - 1)
    n_used = pends[-1] // MOE_ROWS
    n_valid = jnp.clip(counts[block_e] - (blk_start - pstarts[block_e]), 0, MOE_ROWS)
    n_valid = jnp.where(blk < n_used, n_valid, 0)
    tiles_per_blk = MOE_ROWS // MOE_GATHER_ROWS
    tile_off = jnp.arange(tiles_per_blk, dtype=jnp.int32) * MOE_GATHER_ROWS
    tile_valid = jnp.clip(n_valid[:, None] - tile_off[None, :], 0, MOE_GATHER_ROWS).reshape(-1)
    return (slot_t, dest.reshape(t, TOP_K).astype(jnp.int32), block_e.astype(jnp.int32),
            n_valid.astype(jnp.int32), n_used.astype(jnp.int32).reshape(1), tile_valid.astype(jnp.int32))


def moe_layer(x, norm_g, mod, mod_row_tm, router_w, router_b, w_gate, w_up, w_down, final_g):
    router_tm, combine_tm = 512, 256
    h, idx, wt = moe_router(x, norm_g, mod, mod_row_tm(router_tm), router_w, router_b, router_tm)
    slot_t, dest, block_e, n_valid, n_used, tile_valid = moe_layout(idx[:, :TOP_K])
    xb = moe_dispatch(h, slot_t, tile_valid)
    yb = moe_expert_ffn(xb, block_e, n_valid, n_used, w_gate, w_up, w_down)
    return moe_combine(x, yb, dest, wt, mod, mod_row_tm(combine_tm), final_g, combine_tm)


def pack_w_in_tail(w):
    pad = jnp.zeros((D_MODEL, D_TAIL - (R_END - R_LOW)), BF)
    return jnp.concatenate([w[:, R_GQA_Q:R_END].astype(BF), w[:, R_LOW:R_GQA_Q].astype(BF), pad], axis=1)


def kernel(x, c, ctx, c_ctx, w_ada, b_ada, norm1_g, norm2_g, w_in, na_rpb, gla_w_a2, gla_b_a, gla_norm_g, gqa_qn_g, gqa_kn_g, w_pa, w_pb, w_pc, w_out, dense_w_gate, dense_w_up, dense_w_down, router_w, router_b, moe_w_gate, moe_w_up, moe_w_down, final_norm_g):
    n_batch = x.shape[0]
    xl = x.reshape(n_batch * SEQ, D_MODEL)
    xc = ctx.reshape(n_batch * CTX_LEN, D_MODEL)
    cvec = jnp.zeros((8, D_MODEL), F32).at[:n_batch].set(c).at[n_batch].set(c_ctx)
    mods = ada_modulation(cvec, w_ada, b_ada)
    cos_t, sin_t = rope_tables()
    lat_row = _latent_mod_row
    ctx_row = _ctx_mod_row(n_batch)
    prep_tm = 256
    lat_table = lambda t: t % (SEQ // prep_tm)
    ctx_table = lambda t: SEQ // prep_tm

    for i in range(DEPTH):
        last = i == DEPTH - 1
        mod = mods[i]
        w_tail = pack_w_in_tail(w_in[i])
        hl = norm_mod(xl, norm1_g[i], mod, lat_row(512), 0, 1)
        hc = norm_mod(xc, norm1_g[i], mod, ctx_row, 0, 1)
        head_l = matmul_leading_cols(hl, w_in, i, D_HEAD, BF)
        head_c = matmul_leading_cols(hc, w_in, i, D_HEAD, BF)
        tail_l = matmul(hl, w_tail, BF)
        tail_c = matmul(hc, w_tail, BF)

        a_l = na_latent(head_l, head_c, na_bias_table(na_rpb[i]), n_batch)
        b_l, b_c = gla_bidirectional(head_l, head_c, tail_l, tail_c, gla_w_a2[i], gla_b_a[i],
                                     gla_norm_g[i], n_batch)
        q_l, k_l = gqa_prep(tail_l, cos_t, sin_t, gqa_qn_g[i], gqa_kn_g[i], lat_table, prep_tm)
        q_c, k_c = gqa_prep(tail_c, cos_t, sin_t, gqa_qn_g[i], gqa_kn_g[i], ctx_table, prep_tm)
        c_l = gqa_latent(q_l, k_l, k_c, tail_l, tail_c, n_batch)

        m_l = merge_branches(a_l, b_l, c_l, tail_l, w_pa, w_pb, w_pc, i)
        xl = matmul_residual(m_l, w_out, i, xl, mod, lat_row(1024), 2, tn=512)
        if not last:
            a_c = na_context(head_c, n_batch)
            c_c = gqa_context(q_c, k_c, tail_c, n_batch)
            m_c = merge_branches(a_c, b_c, c_c, tail_c, w_pa, w_pb, w_pc, i)
            xc = matmul_residual(m_c, w_out, i, xc, mod, ctx_row, 2, tn=512)

        j = i // 2
        if i % 2 == 0:
            def ffn(xs, mod_row_tm):
                h2 = norm_mod(xs, norm2_g[i], mod, mod_row_tm(512), 3, 4)
                u = ffn_up(h2, dense_w_gate[j], dense_w_up[j])
                return matmul_residual(u, dense_w_down, j, xs, mod, mod_row_tm(1024), 5)
            xl = ffn(xl, lat_row)
            if not last:
                xc = ffn(xc, lambda tm: ctx_row)
        else:
            fin = final_norm_g if last else None
            xl = moe_layer(xl, norm2_g[i], mod, lat_row, router_w[j], router_b[j],
                           moe_w_gate[j], moe_w_up[j], moe_w_down[j], fin)
            if not last:
                xc = moe_layer(xc, norm2_g[i], mod, lambda tm: ctx_row, router_w[j], router_b[j],
                               moe_w_gate[j], moe_w_up[j], moe_w_down[j], None)
    if (DEPTH - 1) % 2 == 0:
        xl = final_rmsnorm(xl, final_norm_g)
    return xl.reshape(n_batch, SEQ, D_MODEL)
```

```python
import functools

import jax
import jax.numpy as jnp
from jax import lax
from jax.experimental import pallas as pl
from jax.experimental.pallas import tpu as pltpu

BF = jnp.bfloat16
F32 = jnp.float32

D_MODEL = 2048
SEQ = 2048
CTX_LEN = 256
DEPTH = 2
GRID_W = 64
GRID_ROWS = SEQ // GRID_W
NA_HEADS = 16
NA_HEAD_DIM = 64
NA_WIN_ROWS = 8
NA_WIN_COLS = 16
GLA_HEADS = 4
GLA_DK = 128
GLA_DV = 256
GLA_GATE_RANK = 16
GLA_TAU = 16.0
GQA_HEADS = 8
GQA_KV_HEADS = 2
GQA_HEAD_DIM = 128
GQA_GROUP = GQA_HEADS // GQA_KV_HEADS
ROPE_THETA = 10000.0
D_FF = 5632
N_EXPERTS = 8
TOP_K = 2
NORM_EPS = 1e-6
BRANCH_W = 1024
LOG2_E = 1.4426950408889634

C_NA_Q, C_NA_K, C_NA_V = 0, 1024, 2048
C_GLA_Q, C_GLA_K, C_GLA_V, C_GLA_OG = 3072, 3584, 4096, 5120
D_HEAD = 6144
C_GQA_Q, C_GQA_K, C_GQA_V = 0, 1024, 1280
C_GATE_A, C_GATE_B, C_GATE_C = 1536, 3584, 5632
C_LOW = 7680
D_TAIL = 8192
R_LOW = 6144
R_GQA_Q = 6176
R_END = 13856

V7X_LANES = 128
V7X_VMEM_LIMIT_BYTES = 56 * 1024 * 1024

GLA_STEP = 128
GLA_SUB = 16
GLA_HEADS_PER_STEP = 2
MOE_ROWS = 1536
MOE_SUB = 512
MOE_UP_TILE = 256
MOE_DOWN_TILE = 256
MOE_GATHER_ROWS = 256
MOE_CAST_CHUNK = 512


def _cparams(*sem):
    return pltpu.CompilerParams(dimension_semantics=sem,
                                vmem_limit_bytes=V7X_VMEM_LIMIT_BYTES)


def _dot(a, b):
    return jnp.dot(a, b, preferred_element_type=F32)


def _dot_nt(a, b):
    return lax.dot_general(a, b, (((1,), (1,)), ((), ())), preferred_element_type=F32)


def _dot_cast(a_ref, rows, w_ref, chunk):
    acc = None
    for k0 in range(0, w_ref.shape[0], chunk):
        part = _dot(a_ref[rows, k0:k0 + chunk], w_ref[k0:k0 + chunk, :].astype(BF))
        acc = part if acc is None else acc + part
    return acc


def _silu(x):
    return x * jax.nn.sigmoid(x)


def _latent_mod_row(tm):
    return lambda t: (t * tm) // SEQ


def _ctx_mod_row(n_batch):
    return lambda t: n_batch


def _ada_kernel(c_ref, w_ref, b_ref, o_ref):
    a = _silu(c_ref[...])
    o_ref[...] = _dot(a.astype(BF), w_ref[...].astype(BF)) + b_ref[...]


def ada_modulation(cvec, w_ada, b_ada):
    tn = 1024
    n6 = 6 * D_MODEL
    out = pl.pallas_call(
        _ada_kernel,
        grid=(DEPTH, n6 // tn),
        in_specs=[
            pl.BlockSpec((8, D_MODEL), lambda l, j: (0, 0)),
            pl.BlockSpec((None, D_MODEL, tn), lambda l, j: (l, 0, j)),
            pl.BlockSpec((None, 1, tn), lambda l, j: (l, 0, j)),
        ],
        out_specs=pl.BlockSpec((None, 8, tn), lambda l, j: (l, 0, j)),
        out_shape=jax.ShapeDtypeStruct((DEPTH, 8, n6), F32),
        compiler_params=_cparams("parallel", "parallel"),
        name="ada_modulation",
    )(cvec, w_ada, b_ada.reshape(DEPTH, 1, n6))
    return out.reshape(DEPTH, 8, 6, D_MODEL)


def _modulated_norm(x, g, mod, shift_idx, scale_idx):
    ms = jnp.mean(x * x, axis=-1, keepdims=True)
    y = x * lax.rsqrt(ms + NORM_EPS) * g
    return y * (1.0 + mod[scale_idx:scale_idx + 1]) + mod[shift_idx:shift_idx + 1]


def _norm_mod_kernel(x_ref, g_ref, mod_ref, o_ref, *, shift_idx, scale_idx):
    o_ref[...] = _modulated_norm(x_ref[...], g_ref[...], mod_ref[...],
                                 shift_idx, scale_idx).astype(o_ref.dtype)


def norm_mod(x, g, mod, mod_row, shift_idx, scale_idx, tm=512):
    m = x.shape[0]
    tm = min(tm, m)
    return pl.pallas_call(
        functools.partial(_norm_mod_kernel, shift_idx=shift_idx, scale_idx=scale_idx),
        grid=(m // tm,),
        in_specs=[
            pl.BlockSpec((tm, D_MODEL), lambda t: (t, 0)),
            pl.BlockSpec((1, D_MODEL), lambda t: (0, 0)),
            pl.BlockSpec((None, 6, D_MODEL), lambda t: (mod_row(t), 0, 0)),
        ],
        out_specs=pl.BlockSpec((tm, D_MODEL), lambda t: (t, 0)),
        out_shape=jax.ShapeDtypeStruct((m, D_MODEL), BF),
        compiler_params=_cparams("parallel"),
        name="norm_mod",
    )(x, g.reshape(1, D_MODEL), mod)


def _rmsnorm_kernel(x_ref, g_ref, o_ref):
    x = x_ref[...]
    ms = jnp.mean(x * x, axis=-1, keepdims=True)
    o_ref[...] = x * lax.rsqrt(ms + NORM_EPS) * g_ref[...]


def final_rmsnorm(x, g, tm=512):
    m = x.shape[0]
    return pl.pallas_call(
        _rmsnorm_kernel,
        grid=(m // tm,),
        in_specs=[pl.BlockSpec((tm, D_MODEL), lambda t: (t, 0)),
                  pl.BlockSpec((1, D_MODEL), lambda t: (0, 0))],
        out_specs=pl.BlockSpec((tm, D_MODEL), lambda t: (t, 0)),
        out_shape=jax.ShapeDtypeStruct((m, D_MODEL), F32),
        compiler_params=_cparams("parallel"),
        name="final_rmsnorm",
    )(x, g.reshape(1, D_MODEL))


def _mm_kernel(a_ref, w_ref, o_ref):
    o_ref[...] = _dot(a_ref[...], w_ref[...].astype(BF)).astype(o_ref.dtype)


def matmul(a, w, out_dtype, tm=1024, tn=512):
    m, k = a.shape
    n = w.shape[1]
    tm = min(tm, m)
    return pl.pallas_call(
        _mm_kernel,
        grid=(m // tm, n // tn),
        in_specs=[pl.BlockSpec((tm, k), lambda i, j: (i, 0)),
                  pl.BlockSpec((k, tn), lambda i, j: (0, j))],
        out_specs=pl.BlockSpec((tm, tn), lambda i, j: (i, j)),
        out_shape=jax.ShapeDtypeStruct((m, n), out_dtype),
        compiler_params=_cparams("parallel", "parallel"),
        name="matmul",
    )(a, w)


def _mm_cast_kernel(a_ref, w_ref, o_ref, *, chunk):
    o_ref[...] = _dot_cast(a_ref, slice(None), w_ref, chunk).astype(o_ref.dtype)


def matmul_leading_cols(a, w, layer, n_cols, out_dtype, tm=1024, tn=512):
    m, k = a.shape
    tm = min(tm, m)
    return pl.pallas_call(
        functools.partial(_mm_cast_kernel, chunk=512),
        grid=(m // tm, n_cols // tn),
        in_specs=[pl.BlockSpec((tm, k), lambda i, j: (i, 0)),
                  pl.BlockSpec((None, k, tn), lambda i, j: (layer, 0, j))],
        out_specs=pl.BlockSpec((tm, tn), lambda i, j: (i, j)),
        out_shape=jax.ShapeDtypeStruct((m, n_cols), out_dtype),
        compiler_params=_cparams("parallel", "parallel"),
        name="matmul_leading_cols",
    )(a, w)


def _mm_res_kernel(a_ref, w_ref, x_ref, mod_ref, o_ref, *, gate_idx):
    y = _dot(a_ref[...], w_ref[...].astype(BF))
    o_ref[...] = x_ref[...] + mod_ref[gate_idx:gate_idx + 1, :] * y


def matmul_residual(a, w, layer, x, mod, mod_row, gate_idx, tm=1024, tn=256):
    m, k = a.shape
    n = w.shape[2]
    tm = min(tm, m)
    return pl.pallas_call(
        functools.partial(_mm_res_kernel, gate_idx=gate_idx),
        grid=(m // tm, n // tn),
        in_specs=[pl.BlockSpec((tm, k), lambda i, j: (i, 0)),
                  pl.BlockSpec((None, k, tn), lambda i, j: (layer, 0, j)),
                  pl.BlockSpec((tm, tn), lambda i, j: (i, j)),
                  pl.BlockSpec((None, 6, tn), lambda i, j: (mod_row(i), 0, j))],
        out_specs=pl.BlockSpec((tm, tn), lambda i, j: (i, j)),
        out_shape=jax.ShapeDtypeStruct((m, n), F32),
        compiler_params=_cparams("parallel", "parallel"),
        name="matmul_residual",
    )(a, w, x, mod)


def _merge_kernel(a_ref, b_ref, c_ref, ga_ref, gb_ref, gc_ref, wa_ref, wb_ref, wc_ref, o_ref):
    def branch(x_ref, g_ref, w_ref):
        return jax.nn.sigmoid(g_ref[...].astype(F32)) * _dot(x_ref[...], w_ref[...].astype(BF))

    o_ref[...] = (branch(a_ref, ga_ref, wa_ref) + branch(b_ref, gb_ref, wb_ref)
                  + branch(c_ref, gc_ref, wc_ref)).astype(o_ref.dtype)


def merge_branches(a, b, c, proj, w_pa, w_pb, w_pc, layer, tm=1024, tn=512):
    m = a.shape[0]
    tm = min(tm, m)
    x_spec = pl.BlockSpec((tm, BRANCH_W), lambda i, j: (i, 0))
    w_spec = pl.BlockSpec((None, BRANCH_W, tn), lambda i, j: (layer, 0, j))

    def gate_spec(col):
        return pl.BlockSpec((tm, tn), lambda i, j: (i, col // tn + j))

    return pl.pallas_call(
        _merge_kernel,
        grid=(m // tm, D_MODEL // tn),
        in_specs=[x_spec, x_spec, x_spec,
                  gate_spec(C_GATE_A), gate_spec(C_GATE_B), gate_spec(C_GATE_C),
                  w_spec, w_spec, w_spec],
        out_specs=pl.BlockSpec((tm, tn), lambda i, j: (i, j)),
        out_shape=jax.ShapeDtypeStruct((m, D_MODEL), BF),
        compiler_params=_cparams("parallel", "parallel"),
        name="merge_branches",
    )(a, b, c, proj, proj, proj, w_pa, w_pb, w_pc)


def _ffn_up_kernel(h_ref, wg_ref, wu_ref, o_ref):
    h = h_ref[...]
    g = _dot(h, wg_ref[...].astype(BF))
    u = _dot(h, wu_ref[...].astype(BF))
    o_ref[...] = (_silu(g) * u).astype(o_ref.dtype)


def ffn_up(h, w_gate, w_up, tm=1024, tn=512):
    m = h.shape[0]
    tm = min(tm, m)
    return pl.pallas_call(
        _ffn_up_kernel,
        grid=(m // tm, D_FF // tn),
        in_specs=[pl.BlockSpec((tm, D_MODEL), lambda i, j: (i, 0)),
                  pl.BlockSpec((D_MODEL, tn), lambda i, j: (0, j)),
                  pl.BlockSpec((D_MODEL, tn), lambda i, j: (0, j))],
        out_specs=pl.BlockSpec((tm, tn), lambda i, j: (i, j)),
        out_shape=jax.ShapeDtypeStruct((m, D_FF), BF),
        compiler_params=_cparams("parallel", "parallel"),
        name="ffn_up",
    )(h, w_gate, w_up)


def _gqa_prep_kernel(q_ref, k_ref, cos_ref, sin_ref, qg_ref, kg_ref, qo_ref, ko_ref):
    cos = cos_ref[...]
    sin = sin_ref[...]
    lane = lax.broadcasted_iota(jnp.int32, cos.shape, 1)
    even = (lane & 1) == 0

    def prep(x_ref, g_ref, o_ref, heads):
        for h in range(heads):
            sl = slice(h * GQA_HEAD_DIM, (h + 1) * GQA_HEAD_DIM)
            x = x_ref[:, sl].astype(F32)
            ms = jnp.mean(x * x, axis=-1, keepdims=True)
            y = x * lax.rsqrt(ms + NORM_EPS) * g_ref[...]
            swapped = jnp.where(even, pltpu.roll(y, GQA_HEAD_DIM - 1, 1), pltpu.roll(y, 1, 1))
            o_ref[:, sl] = (y * cos + swapped * sin).astype(o_ref.dtype)

    prep(q_ref, qg_ref, qo_ref, GQA_HEADS)
    prep(k_ref, kg_ref, ko_ref, GQA_KV_HEADS)


def gqa_prep(proj, cos_t, sin_t, qn_g, kn_g, table_block, tm=256):
    m = proj.shape[0]
    qw = GQA_HEADS * GQA_HEAD_DIM
    kw = GQA_KV_HEADS * GQA_HEAD_DIM
    return pl.pallas_call(
        _gqa_prep_kernel,
        grid=(m // tm,),
        in_specs=[pl.BlockSpec((tm, qw), lambda t: (t, C_GQA_Q // qw)),
                  pl.BlockSpec((tm, kw), lambda t: (t, C_GQA_K // kw)),
                  pl.BlockSpec((tm, GQA_HEAD_DIM), lambda t: (table_block(t), 0)),
                  pl.BlockSpec((tm, GQA_HEAD_DIM), lambda t: (table_block(t), 0)),
                  pl.BlockSpec((1, GQA_HEAD_DIM), lambda t: (0, 0)),
                  pl.BlockSpec((1, GQA_HEAD_DIM), lambda t: (0, 0))],
        out_specs=[pl.BlockSpec((tm, qw), lambda t: (t, 0)),
                   pl.BlockSpec((tm, kw), lambda t: (t, 0))],
        out_shape=[jax.ShapeDtypeStruct((m, qw), BF), jax.ShapeDtypeStruct((m, kw), BF)],
        compiler_params=_cparams("parallel"),
        name="gqa_prep",
    )(proj, proj, cos_t, sin_t, qn_g.reshape(1, -1), kn_g.reshape(1, -1))


def rope_tables():
    half = GQA_HEAD_DIM // 2
    freqs = ROPE_THETA ** (-jnp.arange(0, half, 2, dtype=F32) / half)
    t = jnp.arange(SEQ)
    row = (t // GRID_W).astype(F32)
    col = (t % GRID_W).astype(F32)
    ang = jnp.concatenate([row[:, None] * freqs, col[:, None] * freqs], axis=-1)
    cos = jnp.repeat(jnp.cos(ang), 2, axis=-1)
    sin = jnp.repeat(jnp.sin(ang), 2, axis=-1)
    sign = jnp.tile(jnp.array([-1.0, 1.0], F32), half)
    cos = jnp.concatenate([cos, jnp.ones((256, GQA_HEAD_DIM), F32)], axis=0)
    sin = jnp.concatenate([sin * sign, jnp.zeros((256, GQA_HEAD_DIM), F32)], axis=0)
    return cos, sin


def _attend(q, kv_list, scale):
    scores = []
    for k, _, bias in kv_list:
        s = _dot_nt(q, k) * scale
        if bias is not None:
            s = s + bias
        scores.append(s)
    m = functools.reduce(jnp.maximum, [jnp.max(s, axis=-1, keepdims=True) for s in scores])
    ps = [jnp.exp(s - m) for s in scores]
    den = functools.reduce(jnp.add, [jnp.sum(p, axis=-1, keepdims=True) for p in ps])
    o = functools.reduce(jnp.add, [_dot(p.astype(BF), v) for p, (_, v, _) in zip(ps, kv_list)])
    return o / den


def _head_pair_rows(q2):
    lane = lax.broadcasted_iota(jnp.int32, q2.shape, 1)
    first = lane < NA_HEAD_DIM
    zero = jnp.zeros_like(q2)
    return jnp.concatenate([jnp.where(first, q2, zero), jnp.where(first, zero, q2)], axis=0)


def _head_pair_merge(o, m):
    lane = lax.broadcasted_iota(jnp.int32, (m, 2 * NA_HEAD_DIM), 1)
    return jnp.where(lane < NA_HEAD_DIM, o[:m], o[m:])


def _na_kernel(q_ref, kl_ref, vl_ref, kc_ref, vc_ref, bias_ref, o_ref):
    r = pl.program_id(1)
    start = jnp.clip(r - NA_WIN_ROWS // 2, 0, GRID_ROWS - NA_WIN_ROWS) * GRID_W
    start = pl.multiple_of(start, GRID_W)
    n_loc = NA_WIN_ROWS * GRID_W
    scale = NA_HEAD_DIM ** -0.5
    for hp in range(NA_HEADS // 2):
        sl = slice(hp * 128, (hp + 1) * 128)
        qq = _head_pair_rows(q_ref[:, sl])
        bias = jnp.concatenate([bias_ref[2 * hp], bias_ref[2 * hp + 1]], axis=0)
        o = _attend(qq, [(kl_ref[pl.ds(start, n_loc), sl], vl_ref[pl.ds(start, n_loc), sl], bias),
                         (kc_ref[:, sl], vc_ref[:, sl], None)], scale)
        o_ref[:, sl] = _head_pair_merge(o, GRID_W).astype(o_ref.dtype)


def na_latent(proj_l, proj_c, bias_tab, n_batch):
    w = NA_HEADS * NA_HEAD_DIM

    def pattern(r):
        return jnp.where(r < 4, r, jnp.where(r > GRID_ROWS - 4, r - (GRID_ROWS - NA_WIN_ROWS), 4))

    return pl.pallas_call(
        _na_kernel,
        grid=(n_batch, GRID_ROWS),
        in_specs=[pl.BlockSpec((GRID_W, w), lambda b, r: (b * GRID_ROWS + r, C_NA_Q // w)),
                  pl.BlockSpec((SEQ, w), lambda b, r: (b, C_NA_K // w)),
                  pl.BlockSpec((SEQ, w), lambda b, r: (b, C_NA_V // w)),
                  pl.BlockSpec((CTX_LEN, w), lambda b, r: (b, C_NA_K // w)),
                  pl.BlockSpec((CTX_LEN, w), lambda b, r: (b, C_NA_V // w)),
                  pl.BlockSpec((None, NA_HEADS, GRID_W, NA_WIN_ROWS * GRID_W),
                               lambda b, r: (pattern(r), 0, 0, 0))],
        out_specs=pl.BlockSpec((GRID_W, w), lambda b, r: (b * GRID_ROWS + r, 0)),
        out_shape=jax.ShapeDtypeStruct((n_batch * SEQ, w), BF),
        compiler_params=_cparams("parallel", "arbitrary"),
        name="na_latent",
    )(proj_l, proj_l, proj_l, proj_c, proj_c, bias_tab)


def na_bias_table(rpb):
    cols = jnp.arange(GRID_W)
    col_start = jnp.clip(cols - NA_WIN_COLS // 2, 0, GRID_W - NA_WIN_COLS)
    in_win = (cols[None, :] >= col_start[:, None]) & (cols[None, :] < col_start[:, None] + NA_WIN_COLS)
    dc = jnp.clip(cols[None, :] - cols[:, None] + NA_WIN_COLS - 1, 0, 2 * NA_WIN_COLS - 2)
    onehot = (dc[None] == jnp.arange(2 * NA_WIN_COLS - 1)[:, None, None]).astype(F32)
    by_col = jnp.einsum('hdc,cqk->hdqk', rpb.astype(F32), onehot, precision=lax.Precision.HIGHEST)
    by_col = jnp.where(in_win[None, None], by_col, -jnp.inf)
    last = NA_WIN_ROWS - 1
    bias = jnp.stack([by_col[:, last - p:last - p + NA_WIN_ROWS] for p in range(NA_WIN_ROWS)])
    bias = bias.transpose(0, 1, 3, 2, 4)
    return bias.reshape(NA_WIN_ROWS, NA_HEADS, GRID_W, NA_WIN_ROWS * GRID_W)


def _na_ctx_kernel(q_ref, k_ref, v_ref, o_ref):
    scale = NA_HEAD_DIM ** -0.5
    for hp in range(NA_HEADS // 2):
        sl = slice(hp * 128, (hp + 1) * 128)
        o = _attend(_head_pair_rows(q_ref[:, sl]), [(k_ref[:, sl], v_ref[:, sl], None)], scale)
        o_ref[:, sl] = _head_pair_merge(o, CTX_LEN).astype(o_ref.dtype)


def na_context(proj_c, n_batch):
    w = NA_HEADS * NA_HEAD_DIM
    return pl.pallas_call(
        _na_ctx_kernel,
        grid=(n_batch,),
        in_specs=[pl.BlockSpec((CTX_LEN, w), lambda b: (b, C_NA_Q // w)),
                  pl.BlockSpec((CTX_LEN, w), lambda b: (b, C_NA_K // w)),
                  pl.BlockSpec((CTX_LEN, w), lambda b: (b, C_NA_V // w))],
        out_specs=pl.BlockSpec((CTX_LEN, w), lambda b: (b, 0)),
        out_shape=jax.ShapeDtypeStruct((n_batch * CTX_LEN, w), BF),
        compiler_params=_cparams("parallel"),
        name="na_context",
    )(proj_c, proj_c, proj_c)


def _gqa_rows(q_ref):
    return jnp.concatenate([q_ref[:, g * GQA_HEAD_DIM:(g + 1) * GQA_HEAD_DIM]
                            for g in range(GQA_GROUP)], axis=0)


def _gqa_store(o, o_ref):
    tq = o_ref.shape[0]
    for g in range(GQA_GROUP):
        o_ref[:, g * GQA_HEAD_DIM:(g + 1) * GQA_HEAD_DIM] = o[g * tq:(g + 1) * tq].astype(o_ref.dtype)


def _gqa_kernel(q_ref, kl_ref, vl_ref, kc_ref, vc_ref, o_ref):
    for g in range(GQA_GROUP):
        sl = slice(g * GQA_HEAD_DIM, (g + 1) * GQA_HEAD_DIM)
        o = _attend(q_ref[:, sl], [(kl_ref[...], vl_ref[...], None), (kc_ref[...], vc_ref[...], None)],
                    GQA_HEAD_DIM ** -0.5)
        o_ref[:, sl] = o.astype(o_ref.dtype)


def _gqa_ctx_kernel(q_ref, kc_ref, vc_ref, o_ref):
    o = _attend(_gqa_rows(q_ref), [(kc_ref[...], vc_ref[...], None)], GQA_HEAD_DIM ** -0.5)
    _gqa_store(o, o_ref)


def gqa_latent(q_l, k_l, k_c, proj_l, proj_c, n_batch, tq=512):
    gw = GQA_GROUP * GQA_HEAD_DIM
    nq = SEQ // tq
    dh = GQA_HEAD_DIM
    return pl.pallas_call(
        _gqa_kernel,
        grid=(n_batch, GQA_KV_HEADS, nq),
        in_specs=[pl.BlockSpec((tq, gw), lambda b, h, i: (b * nq + i, h)),
                  pl.BlockSpec((SEQ, dh), lambda b, h, i: (b, h)),
                  pl.BlockSpec((SEQ, dh), lambda b, h, i: (b, C_GQA_V // dh + h)),
                  pl.BlockSpec((CTX_LEN, dh), lambda b, h, i: (b, h)),
                  pl.BlockSpec((CTX_LEN, dh), lambda b, h, i: (b, C_GQA_V // dh + h))],
        out_specs=pl.BlockSpec((tq, gw), lambda b, h, i: (b * nq + i, h)),
        out_shape=jax.ShapeDtypeStruct((n_batch * SEQ, GQA_HEADS * dh), BF),
        compiler_params=_cparams("parallel", "parallel", "arbitrary"),
        name="gqa_latent",
    )(q_l, k_l, proj_l, k_c, proj_c)


def gqa_context(q_c, k_c, proj_c, n_batch):
    gw = GQA_GROUP * GQA_HEAD_DIM
    dh = GQA_HEAD_DIM
    return pl.pallas_call(
        _gqa_ctx_kernel,
        grid=(n_batch, GQA_KV_HEADS),
        in_specs=[pl.BlockSpec((CTX_LEN, gw), lambda b, h: (b, h)),
                  pl.BlockSpec((CTX_LEN, dh), lambda b, h: (b, h)),
                  pl.BlockSpec((CTX_LEN, dh), lambda b, h: (b, C_GQA_V // dh + h))],
        out_specs=pl.BlockSpec((CTX_LEN, gw), lambda b, h: (b, h)),
        out_shape=jax.ShapeDtypeStruct((n_batch * CTX_LEN, GQA_HEADS * dh), BF),
        compiler_params=_cparams("parallel", "parallel"),
        name="gqa_context",
    )(q_c, k_c, proj_c)


def _log_sigmoid(z):
    return jnp.minimum(z, 0.0) - jnp.log1p(jnp.exp(-jnp.abs(z)))


def _gla_step(c, direction, head, q_s, k_s, v_s, low_s, wa_ref, ba_ref, st_ref, o_s):
    n = GLA_STEP
    r0 = pl.multiple_of(c * n, n)
    rows = pl.ds(r0, n)
    k_lanes = slice(head * GLA_DK, (head + 1) * GLA_DK)
    v_lanes = slice(head * GLA_DV, (head + 1) * GLA_DV)
    qf = q_s[rows, k_lanes].astype(F32) * (GLA_DK ** -0.5)
    k_bf = k_s[rows, k_lanes]
    kf = k_bf.astype(F32)
    v = v_s[rows, v_lanes]
    z = _dot(low_s[rows, :], wa_ref[direction, head].astype(BF)) + ba_ref[direction, head]
    g = _log_sigmoid(z) * (1.0 / GLA_TAU)

    ri = lax.broadcasted_iota(jnp.int32, (n, n), 0)
    ci = lax.broadcasted_iota(jnp.int32, (n, n), 1)
    tri = (ci <= ri) if direction == 0 else (ci >= ri)
    tri = jnp.where(tri, 1.0, 0.0).astype(BF)
    g_hi = g.astype(BF)
    g_lo = (g - g_hi.astype(F32)).astype(BF)
    bc = (_dot(tri, g_hi) + _dot(tri, g_lo)) * LOG2_E
    btot = bc[n - 1:n] if direction == 0 else bc[0:1]

    state = st_ref[head]
    q_in = (qf * jnp.exp2(bc)).astype(BF)
    o = _dot_nt(q_in, state.astype(BF))
    k_out = (kf * jnp.exp2(btot - bc)).astype(BF)
    v_t = v.astype(F32).T.astype(BF)
    st_ref[head] = jnp.exp2(btot) * state + _dot(v_t, k_out)

    across = None
    size = n // 2
    while size >= GLA_SUB:
        pieces = []
        for base in range(0, n, 2 * size):
            mid = base + size
            if direction == 0:
                early, late, ref = slice(base, mid), slice(mid, mid + size), bc[mid - 1:mid]
            else:
                early, late, ref = slice(mid, mid + size), slice(base, mid), bc[mid:mid + 1]
            k_dec = (kf[early] * jnp.exp2(ref - bc[early])).astype(BF)
            k_rows = [jnp.zeros((early.start, GLA_DK), BF)] if early.start else []
            k_rows.append(k_dec)
            if early.stop < n:
                k_rows.append(jnp.zeros((n - early.stop, GLA_DK), BF))
            part = _dot_nt((qf[late] * jnp.exp2(bc[late] - ref)).astype(BF),
                           jnp.concatenate(k_rows, axis=0))
            none = jnp.zeros((size, n), F32)
            pieces += [none, part] if direction == 0 else [part, none]
        level = jnp.concatenate(pieces, axis=0)
        across = level if across is None else across + level
        size //= 2

    lane = lax.broadcasted_iota(jnp.int32, (GLA_SUB, n), 1)
    sub_row = lax.broadcasted_iota(jnp.int32, (GLA_SUB, 1), 0)
    neg_inf = jnp.float32(-jnp.inf)
    blocks = []
    for blk in range(n // GLA_SUB):
        lo, hi = blk * GLA_SUB, (blk + 1) * GLA_SUB
        q_b, b_b, a_b = qf[lo:hi], bc[lo:hi], across[lo:hi]
        decayed = []
        for jl in range(GLA_SUB):
            keep = (sub_row >= jl) if direction == 0 else (sub_row <= jl)
            dec = jnp.exp2(jnp.where(keep, b_b - bc[lo + jl:lo + jl + 1], neg_inf))
            decayed.append((q_b * dec).astype(BF))
        pair = _dot_nt(jnp.concatenate(decayed, axis=0), k_bf)
        for jl in range(GLA_SUB):
            a_b = jnp.where(lane == lo + jl, pair[jl * GLA_SUB:(jl + 1) * GLA_SUB], a_b)
        blocks.append(a_b)
    attn = jnp.concatenate(blocks, axis=0).astype(BF)
    o_s[rows, v_lanes] = o + _dot(attn, v)


def _gla_kernel(ql, kl, vl, ogl, lowl, qc, kc, vc, ogc, lowc, wa_ref, ba_ref, ng_ref,
                yl_ref, yc_ref, q_s, k_s, v_s, low_s, of_s, ob_s, stf, stb):
    nc = CTX_LEN
    q_s[0:nc, :] = qc[...]
    q_s[nc:, :] = ql[...]
    k_s[0:nc, :] = kc[...]
    k_s[nc:, :] = kl[...]
    v_s[0:nc, :] = vc[...]
    v_s[nc:, :] = vl[...]
    low_s[0:nc, :] = lowc[...]
    low_s[nc:, :] = lowl[...]
    stf[...] = jnp.zeros_like(stf)
    stb[...] = jnp.zeros_like(stb)
    n_ctx = CTX_LEN // GLA_STEP
    n_all = (CTX_LEN + SEQ) // GLA_STEP

    def body(i, carry):
        cb = jnp.where(i < n_ctx, n_ctx - 1 - i, n_all + n_ctx - 1 - i)
        for head in range(GLA_HEADS_PER_STEP):
            _gla_step(i, 0, head, q_s, k_s, v_s, low_s, wa_ref, ba_ref, stf, of_s)
            _gla_step(cb, 1, head, q_s, k_s, v_s, low_s, wa_ref, ba_ref, stb, ob_s)
        return carry

    lax.fori_loop(0, n_all, body, 0)

    def finish(lo, hi, og_ref, y_ref):
        for head in range(GLA_HEADS_PER_STEP):
            lanes = slice(head * GLA_DV, (head + 1) * GLA_DV)
            o = of_s[lo:hi, lanes] + ob_s[lo:hi, lanes]
            ms = jnp.mean(o * o, axis=-1, keepdims=True)
            y = o * lax.rsqrt(ms + NORM_EPS) * ng_ref[...]
            y_ref[:, lanes] = (y * _silu(og_ref[:, lanes].astype(F32))).astype(y_ref.dtype)

    finish(0, nc, ogc, yc_ref)
    finish(nc, nc + SEQ, ogl, yl_ref)


def gla_bidirectional(head_l, head_c, tail_l, tail_c, w_a2, b_a, norm_g, n_batch):
    hps = GLA_HEADS_PER_STEP
    dk, dv = GLA_DK, GLA_DV
    kw, vw = hps * dk, hps * dv
    wa = jnp.zeros((2, GLA_HEADS, V7X_LANES, dk), F32)
    for d in range(2):
        wa = wa.at[d, :, d * GLA_GATE_RANK:(d + 1) * GLA_GATE_RANK, :].set(
            w_a2[d].reshape(GLA_GATE_RANK, GLA_HEADS, dk).transpose(1, 0, 2))
    ba = b_a.reshape(2, GLA_HEADS, 1, dk)

    def specs(rows):
        return [pl.BlockSpec((rows, kw), lambda b, h: (b, C_GLA_Q // kw + h)),
                pl.BlockSpec((rows, kw), lambda b, h: (b, C_GLA_K // kw + h)),
                pl.BlockSpec((rows, vw), lambda b, h: (b, C_GLA_V // vw + h)),
                pl.BlockSpec((rows, vw), lambda b, h: (b, C_GLA_OG // vw + h)),
                pl.BlockSpec((rows, V7X_LANES), lambda b, h: (b, C_LOW // V7X_LANES))]

    n_rows = CTX_LEN + SEQ
    return pl.pallas_call(
        _gla_kernel,
        grid=(n_batch, GLA_HEADS // hps),
        in_specs=specs(SEQ) + specs(CTX_LEN) + [
            pl.BlockSpec((2, hps, V7X_LANES, dk), lambda b, h: (0, h, 0, 0)),
            pl.BlockSpec((2, hps, 1, dk), lambda b, h: (0, h, 0, 0)),
            pl.BlockSpec((1, dv), lambda b, h: (0, 0))],
        out_specs=[pl.BlockSpec((SEQ, vw), lambda b, h: (b, h)),
                   pl.BlockSpec((CTX_LEN, vw), lambda b, h: (b, h))],
        out_shape=[jax.ShapeDtypeStruct((n_batch * SEQ, GLA_HEADS * dv), BF),
                   jax.ShapeDtypeStruct((n_batch * CTX_LEN, GLA_HEADS * dv), BF)],
        scratch_shapes=[pltpu.VMEM((n_rows, kw), BF), pltpu.VMEM((n_rows, kw), BF),
                        pltpu.VMEM((n_rows, vw), BF), pltpu.VMEM((n_rows, V7X_LANES), BF),
                        pltpu.VMEM((n_rows, vw), F32), pltpu.VMEM((n_rows, vw), F32),
                        pltpu.VMEM((hps, dv, dk), F32), pltpu.VMEM((hps, dv, dk), F32)],
        compiler_params=_cparams("parallel", "parallel"),
        name="gla_bidirectional",
    )(*([head_l] * 4 + [tail_l] + [head_c] * 4 + [tail_c]), wa, ba, norm_g.reshape(1, dv))


def _router_kernel(x_ref, g_ref, mod_ref, rw_ref, rb_ref, h_ref, idx_ref, wt_ref):
    h = _modulated_norm(x_ref[...], g_ref[...], mod_ref[...], 3, 4)
    h_ref[...] = h.astype(h_ref.dtype)
    logits = jnp.dot(h, rw_ref[...], preferred_element_type=F32,
                     precision=lax.Precision.HIGHEST) + rb_ref[...]
    lane = lax.broadcasted_iota(jnp.int32, logits.shape, 1)
    neg_inf = jnp.float32(-jnp.inf)
    logits = jnp.where(lane < N_EXPERTS, logits, neg_inf)
    m1 = jnp.max(logits, axis=-1, keepdims=True)
    lane_f = lane.astype(F32)
    i1 = jnp.min(jnp.where(logits == m1, lane_f, float(V7X_LANES)), axis=-1, keepdims=True)
    rest = jnp.where(lane_f == i1, neg_inf, logits)
    m2 = jnp.max(rest, axis=-1, keepdims=True)
    i2 = jnp.min(jnp.where(rest == m2, lane_f, float(V7X_LANES)), axis=-1, keepdims=True)
    e = jnp.exp(m2 - m1)
    w1 = 1.0 / (1.0 + e)
    idx_ref[...] = jnp.where(lane == 0, i1, jnp.where(lane == 1, i2, 0.0)).astype(jnp.int32)
    wt_ref[...] = jnp.where(lane == 0, w1, jnp.where(lane == 1, e * w1, 0.0))


def moe_router(x, g, mod, mod_row, router_w, router_b, tm=512):
    m = x.shape[0]
    rw = jnp.zeros((D_MODEL, V7X_LANES), F32).at[:, :N_EXPERTS].set(router_w)
    rb = jnp.zeros((1, V7X_LANES), F32).at[0, :N_EXPERTS].set(router_b)
    return pl.pallas_call(
        _router_kernel,
        grid=(m // tm,),
        in_specs=[pl.BlockSpec((tm, D_MODEL), lambda t: (t, 0)),
                  pl.BlockSpec((1, D_MODEL), lambda t: (0, 0)),
                  pl.BlockSpec((None, 6, D_MODEL), lambda t: (mod_row(t), 0, 0)),
                  pl.BlockSpec((D_MODEL, V7X_LANES), lambda t: (0, 0)),
                  pl.BlockSpec((1, V7X_LANES), lambda t: (0, 0))],
        out_specs=[pl.BlockSpec((tm, D_MODEL), lambda t: (t, 0)),
                   pl.BlockSpec((tm, V7X_LANES), lambda t: (t, 0)),
                   pl.BlockSpec((tm, V7X_LANES), lambda t: (t, 0))],
        out_shape=[jax.ShapeDtypeStruct((m, D_MODEL), F32),
                   jax.ShapeDtypeStruct((m, V7X_LANES), jnp.int32),
                   jax.ShapeDtypeStruct((m, V7X_LANES), F32)],
        compiler_params=_cparams("parallel"),
        name="moe_router",
    )(x, g.reshape(1, D_MODEL), mod, rw, rb)


def _row_copy(src_hbm, src_row, dst, dst_row, sem):
    return pltpu.make_async_copy(src_hbm.at[pl.ds(src_row, 1)], dst.at[pl.ds(dst_row, 1)], sem)


def _gather_tile(src_hbm, idx_ref, n_rows, per_row, buf, sem, action):
    def body(i, carry):
        for k in range(per_row):
            cp = _row_copy(src_hbm, idx_ref[0, per_row * i + k], buf.at[k], i, sem)
            cp.start() if action == "start" else cp.wait()
        return carry

    lax.fori_loop(0, n_rows, body, 0)


def _dispatch_kernel(nv_ref, idx_ref, idx_next_ref, h_hbm, o_ref, buf, sem):
    t = pl.program_id(0)
    rows = o_ref.shape[0]
    slot = t % 2

    def start_tile(tile_idx_ref, tile, s):
        nv = nv_ref[tile]

        @pl.when(nv < rows)
        def _():
            buf[s] = jnp.zeros(buf.shape[1:], buf.dtype)

        _gather_tile(h_hbm, tile_idx_ref, nv, 1, buf.at[s], sem.at[s], "start")

    @pl.when(t == 0)
    def _():
        start_tile(idx_ref, 0, 0)

    @pl.when(t + 1 < pl.num_programs(0))
    def _():
        start_tile(idx_next_ref, t + 1, 1 - slot)

    _gather_tile(h_hbm, idx_ref, nv_ref[t], 1, buf.at[slot], sem.at[slot], "wait")
    o_ref[...] = buf[slot, 0].astype(o_ref.dtype)


def moe_dispatch(h, slot_t, tile_valid):
    n_slots = slot_t.shape[0]
    rows = MOE_GATHER_ROWS
    n_tiles = n_slots // rows
    idx = slot_t.reshape(n_tiles, 1, rows)
    grid_spec = pltpu.PrefetchScalarGridSpec(
        num_scalar_prefetch=1,
        grid=(n_tiles,),
        in_specs=[pl.BlockSpec((None, 1, rows), lambda t, nv: (t, 0, 0), memory_space=pltpu.SMEM),
                  pl.BlockSpec((None, 1, rows), lambda t, nv: (jnp.minimum(t + 1, n_tiles - 1), 0, 0),
                               memory_space=pltpu.SMEM),
                  pl.BlockSpec(memory_space=pl.ANY)],
        out_specs=pl.BlockSpec((rows, D_MODEL), lambda t, nv: (t, 0)),
        scratch_shapes=[pltpu.VMEM((2, 1, rows, D_MODEL), F32), pltpu.SemaphoreType.DMA((2,))],
    )
    return pl.pallas_call(
        _dispatch_kernel,
        grid_spec=grid_spec,
        out_shape=jax.ShapeDtypeStruct((n_slots, D_MODEL), BF),
        compiler_params=_cparams("arbitrary"),
        name="moe_dispatch",
    )(tile_valid, idx, idx, h)


def _moe_ffn_kernel(be_ref, nv_ref, nu_ref, x_ref, wg_ref, wu_ref, wd_ref, o_ref, act):
    s = pl.program_id(1)
    nv = nv_ref[pl.program_id(0)]
    n_up = D_FF // MOE_UP_TILE
    n_sub = (nv + MOE_SUB - 1) // MOE_SUB

    @pl.when(s < n_up)
    def _():
        cols = pl.ds(pl.multiple_of(s * MOE_UP_TILE, MOE_UP_TILE), MOE_UP_TILE)

        def up(n_rows):
            rows = slice(0, n_rows)
            g = _dot_cast(x_ref, rows, wg_ref, MOE_CAST_CHUNK)
            u = _dot_cast(x_ref, rows, wu_ref, MOE_CAST_CHUNK)
            act[rows, cols] = (_silu(g) * u).astype(BF)

        for parts in range(1, MOE_ROWS // MOE_SUB + 1):
            pl.when(n_sub == parts)(functools.partial(up, parts * MOE_SUB))

    @pl.when(s >= n_up)
    def _():
        def down(n_rows):
            if n_rows:
                o_ref[0:n_rows, :] = _dot_cast(act, slice(0, n_rows), wd_ref, MOE_CAST_CHUNK)
            if n_rows < MOE_ROWS:
                o_ref[n_rows:, :] = jnp.zeros((MOE_ROWS - n_rows, MOE_DOWN_TILE), F32)

        for parts in range(0, MOE_ROWS // MOE_SUB + 1):
            pl.when(n_sub == parts)(functools.partial(down, parts * MOE_SUB))


def moe_expert_ffn(xb, block_e, n_valid, n_used, w_gate, w_up, w_down):
    n_blk = xb.shape[0] // MOE_ROWS
    n_up = D_FF // MOE_UP_TILE
    n_down = D_MODEL // MOE_DOWN_TILE

    def last_used(b, nu):
        return jnp.minimum(b, nu[0] - 1)

    def up_idx(b, s, nu):
        return jnp.where(b < nu[0], jnp.minimum(s, n_up - 1), n_up - 1)

    def down_idx(b, s, nu):
        return jnp.where(b < nu[0], jnp.maximum(s - n_up, 0), n_down - 1)

    grid_spec = pltpu.PrefetchScalarGridSpec(
        num_scalar_prefetch=3,
        grid=(n_blk, n_up + n_down),
        in_specs=[
            pl.BlockSpec((MOE_ROWS, D_MODEL), lambda b, s, be, nv, nu: (last_used(b, nu), 0),
                         pipeline_mode=pl.Buffered(1)),
            pl.BlockSpec((None, D_MODEL, MOE_UP_TILE),
                         lambda b, s, be, nv, nu: (be[last_used(b, nu)], 0, up_idx(b, s, nu))),
            pl.BlockSpec((None, D_MODEL, MOE_UP_TILE),
                         lambda b, s, be, nv, nu: (be[last_used(b, nu)], 0, up_idx(b, s, nu))),
            pl.BlockSpec((None, D_FF, MOE_DOWN_TILE),
                         lambda b, s, be, nv, nu: (be[last_used(b, nu)], 0, down_idx(b, s, nu))),
        ],
        out_specs=pl.BlockSpec((MOE_ROWS, MOE_DOWN_TILE),
                               lambda b, s, be, nv, nu: (b, jnp.maximum(s - n_up, 0))),
        scratch_shapes=[pltpu.VMEM((MOE_ROWS, D_FF), BF)],
    )
    return pl.pallas_call(
        _moe_ffn_kernel,
        grid_spec=grid_spec,
        out_shape=jax.ShapeDtypeStruct((n_blk * MOE_ROWS, D_MODEL), F32),
        compiler_params=_cparams("arbitrary", "arbitrary"),
        name="moe_expert_ffn",
    )(block_e, n_valid, n_used, xb, w_gate, w_up, w_down)


def _moe_combine_kernel(idx_ref, idx_next_ref, x_ref, wt_ref, mod_ref, g_ref, yb_hbm, o_ref, buf, sem,
                        *, final_norm):
    t = pl.program_id(0)
    tm = x_ref.shape[0]
    slot = t % 2

    @pl.when(t == 0)
    def _():
        _gather_tile(yb_hbm, idx_ref, tm, TOP_K, buf.at[0], sem.at[0], "start")

    @pl.when(t + 1 < pl.num_programs(0))
    def _():
        _gather_tile(yb_hbm, idx_next_ref, tm, TOP_K, buf.at[1 - slot], sem.at[1 - slot], "start")

    _gather_tile(yb_hbm, idx_ref, tm, TOP_K, buf.at[slot], sem.at[slot], "wait")
    w = wt_ref[...]
    y = w[:, 0:1] * buf[slot, 0] + w[:, 1:2] * buf[slot, 1]
    out = x_ref[...] + mod_ref[5:6, :] * y
    if final_norm:
        ms = jnp.mean(out * out, axis=-1, keepdims=True)
        out = out * lax.rsqrt(ms + NORM_EPS) * g_ref[...]
    o_ref[...] = out


def moe_combine(x, yb, dest, wt, mod, mod_row, final_g, tm=256):
    m = x.shape[0]
    tm = min(tm, m)
    n_tiles = m // tm
    row = pl.BlockSpec((tm, D_MODEL), lambda t: (t, 0))
    final_norm = final_g is not None
    g = final_g if final_norm else jnp.ones((D_MODEL,), F32)
    idx = dest.reshape(n_tiles, 1, TOP_K * tm)
    return pl.pallas_call(
        functools.partial(_moe_combine_kernel, final_norm=final_norm),
        grid=(n_tiles,),
        in_specs=[pl.BlockSpec((None, 1, TOP_K * tm), lambda t: (t, 0, 0), memory_space=pltpu.SMEM),
                  pl.BlockSpec((None, 1, TOP_K * tm), lambda t: (jnp.minimum(t + 1, n_tiles - 1), 0, 0),
                               memory_space=pltpu.SMEM),
                  row,
                  pl.BlockSpec((tm, V7X_LANES), lambda t: (t, 0)),
                  pl.BlockSpec((None, 6, D_MODEL), lambda t: (mod_row(t), 0, 0)),
                  pl.BlockSpec((1, D_MODEL), lambda t: (0, 0)),
                  pl.BlockSpec(memory_space=pl.ANY)],
        out_specs=row,
        out_shape=jax.ShapeDtypeStruct((m, D_MODEL), F32),
        scratch_shapes=[pltpu.VMEM((2, TOP_K, tm, D_MODEL), F32), pltpu.SemaphoreType.DMA((2,))],
        compiler_params=_cparams("arbitrary"),
        name="moe_combine",
    )(idx, idx, x, wt, mod, g.reshape(1, D_MODEL), yb)


def moe_layout(top_i):
    t = top_i.shape[0]
    n_assign = t * TOP_K
    flat_e = top_i.reshape(-1)
    onehot = (flat_e[:, None] == jnp.arange(N_EXPERTS, dtype=jnp.int32)[None, :]).astype(jnp.int32)
    csum = jnp.cumsum(onehot, axis=0)
    rank = jnp.take_along_axis(csum, flat_e[:, None], axis=1)[:, 0] - 1
    counts = csum[-1]
    padded = (counts + MOE_ROWS - 1) // MOE_ROWS * MOE_ROWS
    pends = jnp.cumsum(padded)
    pstarts = pends - padded
    dest = pstarts[flat_e] + rank
    n_blk = n_assign // MOE_ROWS + N_EXPERTS
    flat_t = jnp.repeat(jnp.arange(t, dtype=jnp.int32), TOP_K)
    slot_t = jnp.zeros((n_blk * MOE_ROWS,), jnp.int32).at[dest].set(flat_t)
    blk = jnp.arange(n_blk, dtype=jnp.int32)
    blk_start = blk * MOE_ROWS
    block_e = jnp.minimum(jnp.sum((pends[None, :] <= blk_start[:, None]).astype(jnp.int32), axis=1),
                          N_EXPERTS - 1)
    n_used = pends[-1] // MOE_ROWS
    n_valid = jnp.clip(counts[block_e] - (blk_start - pstarts[block_e]), 0, MOE_ROWS)
    n_valid = jnp.where(blk < n_used, n_valid, 0)
    tiles_per_blk = MOE_ROWS // MOE_GATHER_ROWS
    tile_off = jnp.arange(tiles_per_blk, dtype=jnp.int32) * MOE_GATHER_ROWS
    tile_valid = jnp.clip(n_valid[:, None] - tile_off[None, :], 0, MOE_GATHER_ROWS).reshape(-1)
    return (slot_t, dest.reshape(t, TOP_K).astype(jnp.int32), block_e.astype(jnp.int32),
            n_valid.astype(jnp.int32), n_used.astype(jnp.int32).reshape(1), tile_valid.astype(jnp.int32))


def moe_layer(x, norm_g, mod, mod_row_tm, router_w, router_b, w_gate, w_up, w_down, final_g):
    router_tm, combine_tm = 512, 256
    h, idx, wt = moe_router(x, norm_g, mod, mod_row_tm(router_tm), router_w, router_b, router_tm)
    slot_t, dest, block_e, n_valid, n_used, tile_valid = moe_layout(idx[:, :TOP_K])
    xb = moe_dispatch(h, slot_t, tile_valid)
    yb = moe_expert_ffn(xb, block_e, n_valid, n_used, w_gate, w_up, w_down)
    return moe_combine(x, yb, dest, wt, mod, mod_row_tm(combine_tm), final_g, combine_tm)


def pack_w_in_tail(w):
    pad = jnp.zeros((D_MODEL, D_TAIL - (R_END - R_LOW)), BF)
    return jnp.concatenate([w[:, R_GQA_Q:R_END].astype(BF), w[:, R_LOW:R_GQA_Q].astype(BF), pad], axis=1)


def kernel(x, c, ctx, c_ctx, w_ada, b_ada, norm1_g, norm2_g, w_in, na_rpb, gla_w_a2, gla_b_a, gla_norm_g, gqa_qn_g, gqa_kn_g, w_pa, w_pb, w_pc, w_out, dense_w_gate, dense_w_up, dense_w_down, router_w, router_b, moe_w_gate, moe_w_up, moe_w_down, final_norm_g):
    n_batch = x.shape[0]
    xl = x.reshape(n_batch * SEQ, D_MODEL)
    xc = ctx.reshape(n_batch * CTX_LEN, D_MODEL)
    cvec = jnp.zeros((8, D_MODEL), F32).at[:n_batch].set(c).at[n_batch].set(c_ctx)
    mods = ada_modulation(cvec, w_ada, b_ada)
    cos_t, sin_t = rope_tables()
    lat_row = _latent_mod_row
    ctx_row = _ctx_mod_row(n_batch)
    prep_tm = 256
    lat_table = lambda t: t % (SEQ // prep_tm)
    ctx_table = lambda t: SEQ // prep_tm

    w_in_bf = w_in.astype(BF)
    w_pa, w_pb, w_pc, w_out = (w.astype(BF) for w in (w_pa, w_pb, w_pc, w_out))
    dense_w_down = dense_w_down.astype(BF)

    for i in range(DEPTH):
        last = i == DEPTH - 1
        mod = mods[i]
        w_tail = pack_w_in_tail(w_in_bf[i])
        hl = norm_mod(xl, norm1_g[i], mod, lat_row(512), 0, 1)
        hc = norm_mod(xc, norm1_g[i], mod, ctx_row, 0, 1)
        head_l = matmul_leading_cols(hl, w_in_bf, i, D_HEAD, BF)
        head_c = matmul_leading_cols(hc, w_in_bf, i, D_HEAD, BF)
        tail_l = matmul(hl, w_tail, BF)
        tail_c = matmul(hc, w_tail, BF)

        a_l = na_latent(head_l, head_c, na_bias_table(na_rpb[i]), n_batch)
        b_l, b_c = gla_bidirectional(head_l, head_c, tail_l, tail_c, gla_w_a2[i], gla_b_a[i],
                                     gla_norm_g[i], n_batch)
        q_l, k_l = gqa_prep(tail_l, cos_t, sin_t, gqa_qn_g[i], gqa_kn_g[i], lat_table, prep_tm)
        q_c, k_c = gqa_prep(tail_c, cos_t, sin_t, gqa_qn_g[i], gqa_kn_g[i], ctx_table, prep_tm)
        c_l = gqa_latent(q_l, k_l, k_c, tail_l, tail_c, n_batch)

        m_l = merge_branches(a_l, b_l, c_l, tail_l, w_pa, w_pb, w_pc, i)
        xl = matmul_residual(m_l, w_out, i, xl, mod, lat_row(1024), 2, tn=512)
        if not last:
            a_c = na_context(head_c, n_batch)
            c_c = gqa_context(q_c, k_c, tail_c, n_batch)
            m_c = merge_branches(a_c, b_c, c_c, tail_c, w_pa, w_pb, w_pc, i)
            xc = matmul_residual(m_c, w_out, i, xc, mod, ctx_row, 2, tn=512)

        j = i // 2
        if i % 2 == 0:
            def ffn(xs, mod_row_tm):
                h2 = norm_mod(xs, norm2_g[i], mod, mod_row_tm(512), 3, 4)
                u = ffn_up(h2, dense_w_gate[j], dense_w_up[j], tm=2048)
                return matmul_residual(u, dense_w_down, j, xs, mod, mod_row_tm(1024), 5)
            xl = ffn(xl, lat_row)
            if not last:
                xc = ffn(xc, lambda tm: ctx_row)
        else:
            fin = final_norm_g if last else None
            xl = moe_layer(xl, norm2_g[i], mod, lat_row, router_w[j], router_b[j],
                           moe_w_gate[j], moe_w_up[j], moe_w_down[j], fin)
            if not last:
                xc = moe_layer(xc, norm2_g[i], mod, lambda tm: ctx_row, router_w[j], router_b[j],
                               moe_w_gate[j], moe_w_up[j], moe_w_down[j], None)
    if (DEPTH - 1) % 2 == 0:
        xl = final_rmsnorm(xl, final_norm_g)
    return xl.reshape(n_batch, SEQ, D_MODEL)
```

```python
import functools

import jax
import jax.numpy as jnp
from jax import lax
from jax.experimental import pallas as pl
from jax.experimental.pallas import tpu as pltpu

BF = jnp.bfloat16
F32 = jnp.float32

D_MODEL = 2048
SEQ = 2048
CTX_LEN = 256
DEPTH = 2
GRID_W = 64
GRID_ROWS = SEQ // GRID_W
NA_HEADS = 16
NA_HEAD_DIM = 64
NA_WIN_ROWS = 8
NA_WIN_COLS = 16
GLA_HEADS = 4
GLA_DK = 128
GLA_DV = 256
GLA_GATE_RANK = 16
GLA_TAU = 16.0
GQA_HEADS = 8
GQA_KV_HEADS = 2
GQA_HEAD_DIM = 128
GQA_GROUP = GQA_HEADS // GQA_KV_HEADS
ROPE_THETA = 10000.0
D_FF = 5632
N_EXPERTS = 8
TOP_K = 2
NORM_EPS = 1e-6
BRANCH_W = 1024
LOG2_E = 1.4426950408889634

C_NA_Q, C_NA_K, C_NA_V = 0, 1024, 2048
C_GLA_Q, C_GLA_K, C_GLA_V, C_GLA_OG = 3072, 3584, 4096, 5120
D_HEAD = 6144
C_GQA_Q, C_GQA_K, C_GQA_V = 0, 1024, 1280
C_GATE_A, C_GATE_B, C_GATE_C = 1536, 3584, 5632
C_LOW = 7680
D_TAIL = 8192
R_LOW = 6144
R_GQA_Q = 6176
R_END = 13856

V7X_LANES = 128
V7X_VMEM_LIMIT_BYTES = 56 * 1024 * 1024

NA_ROWS_PER_STEP = 2
GLA_STEP = 128
GLA_SUB = 16
GLA_HEADS_PER_STEP = 2
MOE_ROWS = 1536
MOE_SUB = 512
MOE_UP_TILE = 256
MOE_DOWN_TILE = 256
MOE_GATHER_ROWS = 256
MOE_CAST_CHUNK = 512


def _cparams(*sem):
    return pltpu.CompilerParams(dimension_semantics=sem,
                                vmem_limit_bytes=V7X_VMEM_LIMIT_BYTES)


def _dot(a, b):
    return jnp.dot(a, b, preferred_element_type=F32)


def _dot_nt(a, b):
    return lax.dot_general(a, b, (((1,), (1,)), ((), ())), preferred_element_type=F32)


def _dot_cast(a_ref, rows, w_ref, chunk):
    acc = None
    for k0 in range(0, w_ref.shape[0], chunk):
        part = _dot(a_ref[rows, k0:k0 + chunk], w_ref[k0:k0 + chunk, :].astype(BF))
        acc = part if acc is None else acc + part
    return acc


def _silu(x):
    return x * jax.nn.sigmoid(x)


def _latent_mod_row(tm):
    return lambda t: (t * tm) // SEQ


def _ctx_mod_row(n_batch):
    return lambda t: n_batch


def _ada_kernel(c_ref, w_ref, b_ref, o_ref):
    a = _silu(c_ref[...])
    o_ref[...] = _dot(a.astype(BF), w_ref[...].astype(BF)) + b_ref[...]


def ada_modulation(cvec, w_ada, b_ada):
    tn = 1024
    n6 = 6 * D_MODEL
    out = pl.pallas_call(
        _ada_kernel,
        grid=(DEPTH, n6 // tn),
        in_specs=[
            pl.BlockSpec((8, D_MODEL), lambda l, j: (0, 0)),
            pl.BlockSpec((None, D_MODEL, tn), lambda l, j: (l, 0, j)),
            pl.BlockSpec((None, 1, tn), lambda l, j: (l, 0, j)),
        ],
        out_specs=pl.BlockSpec((None, 8, tn), lambda l, j: (l, 0, j)),
        out_shape=jax.ShapeDtypeStruct((DEPTH, 8, n6), F32),
        compiler_params=_cparams("parallel", "parallel"),
        name="ada_modulation",
    )(cvec, w_ada, b_ada.reshape(DEPTH, 1, n6))
    return out.reshape(DEPTH, 8, 6, D_MODEL)


def _modulated_norm(x, g, mod, shift_idx, scale_idx):
    ms = jnp.mean(x * x, axis=-1, keepdims=True)
    y = x * lax.rsqrt(ms + NORM_EPS) * g
    return y * (1.0 + mod[scale_idx:scale_idx + 1]) + mod[shift_idx:shift_idx + 1]


def _rmsnorm_kernel(x_ref, g_ref, o_ref):
    x = x_ref[...]
    ms = jnp.mean(x * x, axis=-1, keepdims=True)
    o_ref[...] = x * lax.rsqrt(ms + NORM_EPS) * g_ref[...]


def final_rmsnorm(x, g, tm=512):
    m = x.shape[0]
    return pl.pallas_call(
        _rmsnorm_kernel,
        grid=(m // tm,),
        in_specs=[pl.BlockSpec((tm, D_MODEL), lambda t: (t, 0)),
                  pl.BlockSpec((1, D_MODEL), lambda t: (0, 0))],
        out_specs=pl.BlockSpec((tm, D_MODEL), lambda t: (t, 0)),
        out_shape=jax.ShapeDtypeStruct((m, D_MODEL), F32),
        compiler_params=_cparams("parallel"),
        name="final_rmsnorm",
    )(x, g.reshape(1, D_MODEL))


def _in_proj_kernel(x_ref, g_ref, mod_ref, wh_ref, wt_ref, oh_ref, ot_ref, h_s, *, n_head):
    j = pl.program_id(1)

    @pl.when(j == 0)
    def _():
        h_s[...] = _modulated_norm(x_ref[...], g_ref[...], mod_ref[...], 0, 1).astype(h_s.dtype)

    @pl.when(j < n_head)
    def _():
        oh_ref[...] = _dot(h_s[...], wh_ref[...]).astype(oh_ref.dtype)

    @pl.when(j >= n_head)
    def _():
        ot_ref[...] = _dot(h_s[...], wt_ref[...]).astype(ot_ref.dtype)


def in_proj(x, g, mod, mod_row, w_head, w_tail, layer, tm=1024, tn=512):
    m = x.shape[0]
    tm = min(tm, m)
    n_head, n_tail = D_HEAD // tn, D_TAIL // tn
    head_col = lambda j: jnp.minimum(j, n_head - 1)
    tail_col = lambda j: jnp.maximum(j - n_head, 0)
    return pl.pallas_call(
        functools.partial(_in_proj_kernel, n_head=n_head),
        grid=(m // tm, n_head + n_tail),
        in_specs=[pl.BlockSpec((tm, D_MODEL), lambda i, j: (i, 0)),
                  pl.BlockSpec((1, D_MODEL), lambda i, j: (0, 0)),
                  pl.BlockSpec((None, 6, D_MODEL), lambda i, j: (mod_row(i), 0, 0)),
                  pl.BlockSpec((None, D_MODEL, tn), lambda i, j: (layer, 0, head_col(j))),
                  pl.BlockSpec((None, D_MODEL, tn), lambda i, j: (layer, 0, tail_col(j)))],
        out_specs=[pl.BlockSpec((tm, tn), lambda i, j: (i, head_col(j))),
                   pl.BlockSpec((tm, tn), lambda i, j: (i, tail_col(j)))],
        out_shape=[jax.ShapeDtypeStruct((m, D_HEAD), BF), jax.ShapeDtypeStruct((m, D_TAIL), BF)],
        scratch_shapes=[pltpu.VMEM((tm, D_MODEL), BF)],
        compiler_params=_cparams("parallel", "arbitrary"),
        name="in_proj",
    )(x, g.reshape(1, D_MODEL), mod, w_head, w_tail)


def _mm_res_kernel(a_ref, w_ref, x_ref, mod_ref, o_ref, *, gate_idx):
    y = _dot(a_ref[...], w_ref[...].astype(BF))
    o_ref[...] = x_ref[...] + mod_ref[gate_idx:gate_idx + 1, :] * y


def matmul_residual(a, w, layer, x, mod, mod_row, gate_idx, tm=1024, tn=256):
    m, k = a.shape
    n = w.shape[2]
    tm = min(tm, m)
    return pl.pallas_call(
        functools.partial(_mm_res_kernel, gate_idx=gate_idx),
        grid=(m // tm, n // tn),
        in_specs=[pl.BlockSpec((tm, k), lambda i, j: (i, 0)),
                  pl.BlockSpec((None, k, tn), lambda i, j: (layer, 0, j)),
                  pl.BlockSpec((tm, tn), lambda i, j: (i, j)),
                  pl.BlockSpec((None, 6, tn), lambda i, j: (mod_row(i), 0, j))],
        out_specs=pl.BlockSpec((tm, tn), lambda i, j: (i, j)),
        out_shape=jax.ShapeDtypeStruct((m, n), F32),
        compiler_params=_cparams("parallel", "parallel"),
        name="matmul_residual",
    )(a, w, x, mod)


def _merge_kernel(a_ref, b_ref, c_ref, ga_ref, gb_ref, gc_ref, wa_ref, wb_ref, wc_ref, o_ref):
    def branch(x_ref, g_ref, w_ref):
        return jax.nn.sigmoid(g_ref[...].astype(F32)) * _dot(x_ref[...], w_ref[...].astype(BF))

    o_ref[...] = (branch(a_ref, ga_ref, wa_ref) + branch(b_ref, gb_ref, wb_ref)
                  + branch(c_ref, gc_ref, wc_ref)).astype(o_ref.dtype)


def merge_branches(a, b, c, proj, w_pa, w_pb, w_pc, layer, tm=1024, tn=512):
    m = a.shape[0]
    tm = min(tm, m)
    x_spec = pl.BlockSpec((tm, BRANCH_W), lambda i, j: (i, 0))
    w_spec = pl.BlockSpec((None, BRANCH_W, tn), lambda i, j: (layer, 0, j))

    def gate_spec(col):
        return pl.BlockSpec((tm, tn), lambda i, j: (i, col // tn + j))

    return pl.pallas_call(
        _merge_kernel,
        grid=(m // tm, D_MODEL // tn),
        in_specs=[x_spec, x_spec, x_spec,
                  gate_spec(C_GATE_A), gate_spec(C_GATE_B), gate_spec(C_GATE_C),
                  w_spec, w_spec, w_spec],
        out_specs=pl.BlockSpec((tm, tn), lambda i, j: (i, j)),
        out_shape=jax.ShapeDtypeStruct((m, D_MODEL), BF),
        compiler_params=_cparams("parallel", "parallel"),
        name="merge_branches",
    )(a, b, c, proj, proj, proj, w_pa, w_pb, w_pc)


def _ffn_up_kernel(x_ref, g_ref, mod_ref, wg_ref, wu_ref, o_ref, h_s):
    @pl.when(pl.program_id(1) == 0)
    def _():
        h_s[...] = _modulated_norm(x_ref[...], g_ref[...], mod_ref[...], 3, 4).astype(h_s.dtype)

    g = _dot_cast(h_s, slice(None), wg_ref, 512)
    u = _dot_cast(h_s, slice(None), wu_ref, 512)
    o_ref[...] = (_silu(g) * u).astype(o_ref.dtype)


def ffn_up(x, norm_g, mod, mod_row, w_gate, w_up, tm=1024, tn=512):
    m = x.shape[0]
    tm = min(tm, m)
    return pl.pallas_call(
        _ffn_up_kernel,
        grid=(m // tm, D_FF // tn),
        in_specs=[pl.BlockSpec((tm, D_MODEL), lambda i, j: (i, 0)),
                  pl.BlockSpec((1, D_MODEL), lambda i, j: (0, 0)),
                  pl.BlockSpec((None, 6, D_MODEL), lambda i, j: (mod_row(i), 0, 0)),
                  pl.BlockSpec((D_MODEL, tn), lambda i, j: (0, j)),
                  pl.BlockSpec((D_MODEL, tn), lambda i, j: (0, j))],
        out_specs=pl.BlockSpec((tm, tn), lambda i, j: (i, j)),
        out_shape=jax.ShapeDtypeStruct((m, D_FF), BF),
        scratch_shapes=[pltpu.VMEM((tm, D_MODEL), BF)],
        compiler_params=_cparams("parallel", "arbitrary"),
        name="ffn_up",
    )(x, norm_g.reshape(1, D_MODEL), mod, w_gate, w_up)


def _gqa_prep_kernel(q_ref, k_ref, cos_ref, sin_ref, qg_ref, kg_ref, qo_ref, ko_ref):
    cos = cos_ref[...]
    sin = sin_ref[...]
    lane = lax.broadcasted_iota(jnp.int32, cos.shape, 1)
    even = (lane & 1) == 0

    def prep(x_ref, g_ref, o_ref, heads):
        for h in range(heads):
            sl = slice(h * GQA_HEAD_DIM, (h + 1) * GQA_HEAD_DIM)
            x = x_ref[:, sl].astype(F32)
            ms = jnp.mean(x * x, axis=-1, keepdims=True)
            y = x * lax.rsqrt(ms + NORM_EPS) * g_ref[...]
            swapped = jnp.where(even, pltpu.roll(y, GQA_HEAD_DIM - 1, 1), pltpu.roll(y, 1, 1))
            o_ref[:, sl] = (y * cos + swapped * sin).astype(o_ref.dtype)

    prep(q_ref, qg_ref, qo_ref, GQA_HEADS)
    prep(k_ref, kg_ref, ko_ref, GQA_KV_HEADS)


def gqa_prep(proj, cos_t, sin_t, qn_g, kn_g, table_block, tm=256):
    m = proj.shape[0]
    qw = GQA_HEADS * GQA_HEAD_DIM
    kw = GQA_KV_HEADS * GQA_HEAD_DIM
    return pl.pallas_call(
        _gqa_prep_kernel,
        grid=(m // tm,),
        in_specs=[pl.BlockSpec((tm, qw), lambda t: (t, C_GQA_Q // qw)),
                  pl.BlockSpec((tm, kw), lambda t: (t, C_GQA_K // kw)),
                  pl.BlockSpec((tm, GQA_HEAD_DIM), lambda t: (table_block(t), 0)),
                  pl.BlockSpec((tm, GQA_HEAD_DIM), lambda t: (table_block(t), 0)),
                  pl.BlockSpec((1, GQA_HEAD_DIM), lambda t: (0, 0)),
                  pl.BlockSpec((1, GQA_HEAD_DIM), lambda t: (0, 0))],
        out_specs=[pl.BlockSpec((tm, qw), lambda t: (t, 0)),
                   pl.BlockSpec((tm, kw), lambda t: (t, 0))],
        out_shape=[jax.ShapeDtypeStruct((m, qw), BF), jax.ShapeDtypeStruct((m, kw), BF)],
        compiler_params=_cparams("parallel"),
        name="gqa_prep",
    )(proj, proj, cos_t, sin_t, qn_g.reshape(1, -1), kn_g.reshape(1, -1))


def rope_tables():
    half = GQA_HEAD_DIM // 2
    freqs = ROPE_THETA ** (-jnp.arange(0, half, 2, dtype=F32) / half)
    t = jnp.arange(SEQ)
    row = (t // GRID_W).astype(F32)
    col = (t % GRID_W).astype(F32)
    ang = jnp.concatenate([row[:, None] * freqs, col[:, None] * freqs], axis=-1)
    cos = jnp.repeat(jnp.cos(ang), 2, axis=-1)
    sin = jnp.repeat(jnp.sin(ang), 2, axis=-1)
    sign = jnp.tile(jnp.array([-1.0, 1.0], F32), half)
    cos = jnp.concatenate([cos, jnp.ones((256, GQA_HEAD_DIM), F32)], axis=0)
    sin = jnp.concatenate([sin * sign, jnp.zeros((256, GQA_HEAD_DIM), F32)], axis=0)
    return cos, sin


def _attend(q, kv_list, scale):
    scores = []
    for k, _, bias in kv_list:
        s = _dot_nt(q, k) * scale
        if bias is not None:
            s = s + bias
        scores.append(s)
    m = functools.reduce(jnp.maximum, [jnp.max(s, axis=-1, keepdims=True) for s in scores])
    ps = [jnp.exp(s - m) for s in scores]
    den = functools.reduce(jnp.add, [jnp.sum(p, axis=-1, keepdims=True) for p in ps])
    o = functools.reduce(jnp.add, [_dot(p.astype(BF), v) for p, (_, v, _) in zip(ps, kv_list)])
    return o / den


def _head_pair_rows(q2):
    lane = lax.broadcasted_iota(jnp.int32, q2.shape, 1)
    first = lane < NA_HEAD_DIM
    zero = jnp.zeros_like(q2)
    return jnp.concatenate([jnp.where(first, q2, zero), jnp.where(first, zero, q2)], axis=0)


def _head_pair_merge(o, m):
    lane = lax.broadcasted_iota(jnp.int32, (m, 2 * NA_HEAD_DIM), 1)
    return jnp.where(lane < NA_HEAD_DIM, o[:m], o[m:])


def _na_kernel(q_ref, kl_ref, vl_ref, kc_ref, vc_ref, *rest):
    bias_refs, o_ref = rest[:NA_ROWS_PER_STEP], rest[NA_ROWS_PER_STEP]
    n_loc = NA_WIN_ROWS * GRID_W
    scale = NA_HEAD_DIM ** -0.5
    for rr in range(NA_ROWS_PER_STEP):
        r = pl.program_id(1) * NA_ROWS_PER_STEP + rr
        start = jnp.clip(r - NA_WIN_ROWS // 2, 0, GRID_ROWS - NA_WIN_ROWS) * GRID_W
        start = pl.multiple_of(start, GRID_W)
        q_rows = slice(rr * GRID_W, (rr + 1) * GRID_W)
        for hp in range(NA_HEADS // 2):
            sl = slice(hp * 128, (hp + 1) * 128)
            qq = _head_pair_rows(q_ref[q_rows, sl])
            bias = jnp.concatenate([bias_refs[rr][2 * hp], bias_refs[rr][2 * hp + 1]], axis=0)
            o = _attend(qq, [(kl_ref[pl.ds(start, n_loc), sl], vl_ref[pl.ds(start, n_loc), sl], bias),
                             (kc_ref[:, sl], vc_ref[:, sl], None)], scale)
            o_ref[q_rows, sl] = _head_pair_merge(o, GRID_W).astype(o_ref.dtype)


def na_latent(proj_l, proj_c, bias_tab, n_batch):
    w = NA_HEADS * NA_HEAD_DIM
    rps = NA_ROWS_PER_STEP
    steps = GRID_ROWS // rps

    def pattern(r):
        return jnp.where(r < 4, r, jnp.where(r > GRID_ROWS - 4, r - (GRID_ROWS - NA_WIN_ROWS), 4))

    def bias_spec(rr):
        return pl.BlockSpec((None, NA_HEADS, GRID_W, NA_WIN_ROWS * GRID_W),
                            lambda b, i: (pattern(i * rps + rr), 0, 0, 0))

    return pl.pallas_call(
        _na_kernel,
        grid=(n_batch, steps),
        in_specs=[pl.BlockSpec((rps * GRID_W, w), lambda b, i: (b * steps + i, C_NA_Q // w)),
                  pl.BlockSpec((SEQ, w), lambda b, i: (b, C_NA_K // w)),
                  pl.BlockSpec((SEQ, w), lambda b, i: (b, C_NA_V // w)),
                  pl.BlockSpec((CTX_LEN, w), lambda b, i: (b, C_NA_K // w)),
                  pl.BlockSpec((CTX_LEN, w), lambda b, i: (b, C_NA_V // w))]
                 + [bias_spec(rr) for rr in range(rps)],
        out_specs=pl.BlockSpec((rps * GRID_W, w), lambda b, i: (b * steps + i, 0)),
        out_shape=jax.ShapeDtypeStruct((n_batch * SEQ, w), BF),
        compiler_params=_cparams("parallel", "arbitrary"),
        name="na_latent",
    )(proj_l, proj_l, proj_l, proj_c, proj_c, *([bias_tab] * rps))


def na_bias_table(rpb):
    cols = jnp.arange(GRID_W)
    col_start = jnp.clip(cols - NA_WIN_COLS // 2, 0, GRID_W - NA_WIN_COLS)
    in_win = (cols[None, :] >= col_start[:, None]) & (cols[None, :] < col_start[:, None] + NA_WIN_COLS)
    dc = jnp.clip(cols[None, :] - cols[:, None] + NA_WIN_COLS - 1, 0, 2 * NA_WIN_COLS - 2)
    onehot = (dc[None] == jnp.arange(2 * NA_WIN_COLS - 1)[:, None, None]).astype(F32)
    by_col = jnp.einsum('hdc,cqk->hdqk', rpb.astype(F32), onehot, precision=lax.Precision.HIGHEST)
    by_col = jnp.where(in_win[None, None], by_col, -jnp.inf)
    last = NA_WIN_ROWS - 1
    bias = jnp.stack([by_col[:, last - p:last - p + NA_WIN_ROWS] for p in range(NA_WIN_ROWS)])
    bias = bias.transpose(0, 1, 3, 2, 4)
    return bias.reshape(NA_WIN_ROWS, NA_HEADS, GRID_W, NA_WIN_ROWS * GRID_W)


def _na_ctx_kernel(q_ref, k_ref, v_ref, o_ref):
    scale = NA_HEAD_DIM ** -0.5
    for hp in range(NA_HEADS // 2):
        sl = slice(hp * 128, (hp + 1) * 128)
        o = _attend(_head_pair_rows(q_ref[:, sl]), [(k_ref[:, sl], v_ref[:, sl], None)], scale)
        o_ref[:, sl] = _head_pair_merge(o, CTX_LEN).astype(o_ref.dtype)


def na_context(proj_c, n_batch):
    w = NA_HEADS * NA_HEAD_DIM
    return pl.pallas_call(
        _na_ctx_kernel,
        grid=(n_batch,),
        in_specs=[pl.BlockSpec((CTX_LEN, w), lambda b: (b, C_NA_Q // w)),
                  pl.BlockSpec((CTX_LEN, w), lambda b: (b, C_NA_K // w)),
                  pl.BlockSpec((CTX_LEN, w), lambda b: (b, C_NA_V // w))],
        out_specs=pl.BlockSpec((CTX_LEN, w), lambda b: (b, 0)),
        out_shape=jax.ShapeDtypeStruct((n_batch * CTX_LEN, w), BF),
        compiler_params=_cparams("parallel"),
        name="na_context",
    )(proj_c, proj_c, proj_c)


def _gqa_rows(q_ref):
    return jnp.concatenate([q_ref[:, g * GQA_HEAD_DIM:(g + 1) * GQA_HEAD_DIM]
                            for g in range(GQA_GROUP)], axis=0)


def _gqa_store(o, o_ref):
    tq = o_ref.shape[0]
    for g in range(GQA_GROUP):
        o_ref[:, g * GQA_HEAD_DIM:(g + 1) * GQA_HEAD_DIM] = o[g * tq:(g + 1) * tq].astype(o_ref.dtype)


def _gqa_kernel(q_ref, kl_ref, vl_ref, kc_ref, vc_ref, o_ref):
    for g in range(GQA_GROUP):
        sl = slice(g * GQA_HEAD_DIM, (g + 1) * GQA_HEAD_DIM)
        o = _attend(q_ref[:, sl], [(kl_ref[...], vl_ref[...], None), (kc_ref[...], vc_ref[...], None)],
                    GQA_HEAD_DIM ** -0.5)
        o_ref[:, sl] = o.astype(o_ref.dtype)


def _gqa_ctx_kernel(q_ref, kc_ref, vc_ref, o_ref):
    o = _attend(_gqa_rows(q_ref), [(kc_ref[...], vc_ref[...], None)], GQA_HEAD_DIM ** -0.5)
    _gqa_store(o, o_ref)


def gqa_latent(q_l, k_l, k_c, proj_l, proj_c, n_batch, tq=512):
    gw = GQA_GROUP * GQA_HEAD_DIM
    nq = SEQ // tq
    dh = GQA_HEAD_DIM
    return pl.pallas_call(
        _gqa_kernel,
        grid=(n_batch, GQA_KV_HEADS, nq),
        in_specs=[pl.BlockSpec((tq, gw), lambda b, h, i: (b * nq + i, h)),
                  pl.BlockSpec((SEQ, dh), lambda b, h, i: (b, h)),
                  pl.BlockSpec((SEQ, dh), lambda b, h, i: (b, C_GQA_V // dh + h)),
                  pl.BlockSpec((CTX_LEN, dh), lambda b, h, i: (b, h)),
                  pl.BlockSpec((CTX_LEN, dh), lambda b, h, i: (b, C_GQA_V // dh + h))],
        out_specs=pl.BlockSpec((tq, gw), lambda b, h, i: (b * nq + i, h)),
        out_shape=jax.ShapeDtypeStruct((n_batch * SEQ, GQA_HEADS * dh), BF),
        compiler_params=_cparams("parallel", "parallel", "arbitrary"),
        name="gqa_latent",
    )(q_l, k_l, proj_l, k_c, proj_c)


def gqa_context(q_c, k_c, proj_c, n_batch):
    gw = GQA_GROUP * GQA_HEAD_DIM
    dh = GQA_HEAD_DIM
    return pl.pallas_call(
        _gqa_ctx_kernel,
        grid=(n_batch, GQA_KV_HEADS),
        in_specs=[pl.BlockSpec((CTX_LEN, gw), lambda b, h: (b, h)),
                  pl.BlockSpec((CTX_LEN, dh), lambda b, h: (b, h)),
                  pl.BlockSpec((CTX_LEN, dh), lambda b, h: (b, C_GQA_V // dh + h))],
        out_specs=pl.BlockSpec((CTX_LEN, gw), lambda b, h: (b, h)),
        out_shape=jax.ShapeDtypeStruct((n_batch * CTX_LEN, GQA_HEADS * dh), BF),
        compiler_params=_cparams("parallel", "parallel"),
        name="gqa_context",
    )(q_c, k_c, proj_c)


def _log_sigmoid(z):
    return jnp.minimum(z, 0.0) - jnp.log1p(jnp.exp(-jnp.abs(z)))


def _gla_step(c, direction, head, q_s, k_s, v_s, low_s, wa_ref, ba_ref, st_ref, o_s):
    n = GLA_STEP
    r0 = pl.multiple_of(c * n, n)
    rows = pl.ds(r0, n)
    k_lanes = slice(head * GLA_DK, (head + 1) * GLA_DK)
    v_lanes = slice(head * GLA_DV, (head + 1) * GLA_DV)
    qf = q_s[rows, k_lanes].astype(F32) * (GLA_DK ** -0.5)
    k_bf = k_s[rows, k_lanes]
    kf = k_bf.astype(F32)
    v = v_s[rows, v_lanes]
    z = _dot(low_s[rows, :], wa_ref[direction, head].astype(BF)) + ba_ref[direction, head]
    g = _log_sigmoid(z) * (1.0 / GLA_TAU)

    ri = lax.broadcasted_iota(jnp.int32, (n, n), 0)
    ci = lax.broadcasted_iota(jnp.int32, (n, n), 1)
    tri = (ci <= ri) if direction == 0 else (ci >= ri)
    tri = jnp.where(tri, 1.0, 0.0).astype(BF)
    g_hi = g.astype(BF)
    g_lo = (g - g_hi.astype(F32)).astype(BF)
    bc = (_dot(tri, g_hi) + _dot(tri, g_lo)) * LOG2_E
    btot = bc[n - 1:n] if direction == 0 else bc[0:1]

    state = st_ref[head]
    q_in = (qf * jnp.exp2(bc)).astype(BF)
    o = _dot_nt(q_in, state.astype(BF))
    k_out = (kf * jnp.exp2(btot - bc)).astype(BF)
    v_t = v.astype(F32).T.astype(BF)
    st_ref[head] = jnp.exp2(btot) * state + _dot(v_t, k_out)

    across = None
    size = n // 2
    while size >= GLA_SUB:
        pieces = []
        for base in range(0, n, 2 * size):
            mid = base + size
            if direction == 0:
                early, late, ref = slice(base, mid), slice(mid, mid + size), bc[mid - 1:mid]
            else:
                early, late, ref = slice(mid, mid + size), slice(base, mid), bc[mid:mid + 1]
            k_dec = (kf[early] * jnp.exp2(ref - bc[early])).astype(BF)
            k_rows = [jnp.zeros((early.start, GLA_DK), BF)] if early.start else []
            k_rows.append(k_dec)
            if early.stop < n:
                k_rows.append(jnp.zeros((n - early.stop, GLA_DK), BF))
            part = _dot_nt((qf[late] * jnp.exp2(bc[late] - ref)).astype(BF),
                           jnp.concatenate(k_rows, axis=0))
            none = jnp.zeros((size, n), F32)
            pieces += [none, part] if direction == 0 else [part, none]
        level = jnp.concatenate(pieces, axis=0)
        across = level if across is None else across + level
        size //= 2

    lane = lax.broadcasted_iota(jnp.int32, (GLA_SUB, n), 1)
    sub_row = lax.broadcasted_iota(jnp.int32, (GLA_SUB, 1), 0)
    neg_inf = jnp.float32(-jnp.inf)
    blocks = []
    for blk in range(n // GLA_SUB):
        lo, hi = blk * GLA_SUB, (blk + 1) * GLA_SUB
        q_b, b_b, a_b = qf[lo:hi], bc[lo:hi], across[lo:hi]
        decayed = []
        for jl in range(GLA_SUB):
            keep = (sub_row >= jl) if direction == 0 else (sub_row <= jl)
            dec = jnp.exp2(jnp.where(keep, b_b - bc[lo + jl:lo + jl + 1], neg_inf))
            decayed.append((q_b * dec).astype(BF))
        pair = _dot_nt(jnp.concatenate(decayed, axis=0), k_bf)
        for jl in range(GLA_SUB):
            a_b = jnp.where(lane == lo + jl, pair[jl * GLA_SUB:(jl + 1) * GLA_SUB], a_b)
        blocks.append(a_b)
    attn = jnp.concatenate(blocks, axis=0).astype(BF)
    o_s[rows, v_lanes] = o + _dot(attn, v)


def _gla_kernel(ql, kl, vl, ogl, lowl, qc, kc, vc, ogc, lowc, wa_ref, ba_ref, ng_ref,
                yl_ref, yc_ref, q_s, k_s, v_s, low_s, of_s, ob_s, stf, stb):
    nc = CTX_LEN
    q_s[0:nc, :] = qc[...]
    q_s[nc:, :] = ql[...]
    k_s[0:nc, :] = kc[...]
    k_s[nc:, :] = kl[...]
    v_s[0:nc, :] = vc[...]
    v_s[nc:, :] = vl[...]
    low_s[0:nc, :] = lowc[...]
    low_s[nc:, :] = lowl[...]
    stf[...] = jnp.zeros_like(stf)
    stb[...] = jnp.zeros_like(stb)
    n_ctx = CTX_LEN // GLA_STEP
    n_all = (CTX_LEN + SEQ) // GLA_STEP

    def body(i, carry):
        cb = jnp.where(i < n_ctx, n_ctx - 1 - i, n_all + n_ctx - 1 - i)
        for head in range(GLA_HEADS_PER_STEP):
            _gla_step(i, 0, head, q_s, k_s, v_s, low_s, wa_ref, ba_ref, stf, of_s)
            _gla_step(cb, 1, head, q_s, k_s, v_s, low_s, wa_ref, ba_ref, stb, ob_s)
        return carry

    lax.fori_loop(0, n_all, body, 0)

    def finish(lo, hi, og_ref, y_ref):
        for head in range(GLA_HEADS_PER_STEP):
            lanes = slice(head * GLA_DV, (head + 1) * GLA_DV)
            o = of_s[lo:hi, lanes] + ob_s[lo:hi, lanes]
            ms = jnp.mean(o * o, axis=-1, keepdims=True)
            y = o * lax.rsqrt(ms + NORM_EPS) * ng_ref[...]
            y_ref[:, lanes] = (y * _silu(og_ref[:, lanes].astype(F32))).astype(y_ref.dtype)

    finish(0, nc, ogc, yc_ref)
    finish(nc, nc + SEQ, ogl, yl_ref)


def gla_bidirectional(head_l, head_c, tail_l, tail_c, w_a2, b_a, norm_g, n_batch):
    hps = GLA_HEADS_PER_STEP
    dk, dv = GLA_DK, GLA_DV
    kw, vw = hps * dk, hps * dv
    wa = jnp.zeros((2, GLA_HEADS, V7X_LANES, dk), F32)
    for d in range(2):
        wa = wa.at[d, :, d * GLA_GATE_RANK:(d + 1) * GLA_GATE_RANK, :].set(
            w_a2[d].reshape(GLA_GATE_RANK, GLA_HEADS, dk).transpose(1, 0, 2))
    ba = b_a.reshape(2, GLA_HEADS, 1, dk)

    def specs(rows):
        return [pl.BlockSpec((rows, kw), lambda b, h: (b, C_GLA_Q // kw + h)),
                pl.BlockSpec((rows, kw), lambda b, h: (b, C_GLA_K // kw + h)),
                pl.BlockSpec((rows, vw), lambda b, h: (b, C_GLA_V // vw + h)),
                pl.BlockSpec((rows, vw), lambda b, h: (b, C_GLA_OG // vw + h)),
                pl.BlockSpec((rows, V7X_LANES), lambda b, h: (b, C_LOW // V7X_LANES))]

    n_rows = CTX_LEN + SEQ
    return pl.pallas_call(
        _gla_kernel,
        grid=(n_batch, GLA_HEADS // hps),
        in_specs=specs(SEQ) + specs(CTX_LEN) + [
            pl.BlockSpec((2, hps, V7X_LANES, dk), lambda b, h: (0, h, 0, 0)),
            pl.BlockSpec((2, hps, 1, dk), lambda b, h: (0, h, 0, 0)),
            pl.BlockSpec((1, dv), lambda b, h: (0, 0))],
        out_specs=[pl.BlockSpec((SEQ, vw), lambda b, h: (b, h)),
                   pl.BlockSpec((CTX_LEN, vw), lambda b, h: (b, h))],
        out_shape=[jax.ShapeDtypeStruct((n_batch * SEQ, GLA_HEADS * dv), BF),
                   jax.ShapeDtypeStruct((n_batch * CTX_LEN, GLA_HEADS * dv), BF)],
        scratch_shapes=[pltpu.VMEM((n_rows, kw), BF), pltpu.VMEM((n_rows, kw), BF),
                        pltpu.VMEM((n_rows, vw), BF), pltpu.VMEM((n_rows, V7X_LANES), BF),
                        pltpu.VMEM((n_rows, vw), F32), pltpu.VMEM((n_rows, vw), F32),
                        pltpu.VMEM((hps, dv, dk), F32), pltpu.VMEM((hps, dv, dk), F32)],
        compiler_params=_cparams("parallel", "parallel"),
        name="gla_bidirectional",
    )(*([head_l] * 4 + [tail_l] + [head_c] * 4 + [tail_c]), wa, ba, norm_g.reshape(1, dv))


def _router_kernel(x_ref, g_ref, mod_ref, rw_ref, rb_ref, h_ref, idx_ref, wt_ref):
    h = _modulated_norm(x_ref[...], g_ref[...], mod_ref[...], 3, 4)
    h_ref[...] = h.astype(h_ref.dtype)
    logits = jnp.dot(h, rw_ref[...], preferred_element_type=F32,
                     precision=lax.Precision.HIGHEST) + rb_ref[...]
    lane = lax.broadcasted_iota(jnp.int32, logits.shape, 1)
    neg_inf = jnp.float32(-jnp.inf)
    logits = jnp.where(lane < N_EXPERTS, logits, neg_inf)
    m1 = jnp.max(logits, axis=-1, keepdims=True)
    lane_f = lane.astype(F32)
    i1 = jnp.min(jnp.where(logits == m1, lane_f, float(V7X_LANES)), axis=-1, keepdims=True)
    rest = jnp.where(lane_f == i1, neg_inf, logits)
    m2 = jnp.max(rest, axis=-1, keepdims=True)
    i2 = jnp.min(jnp.where(rest == m2, lane_f, float(V7X_LANES)), axis=-1, keepdims=True)
    e = jnp.exp(m2 - m1)
    w1 = 1.0 / (1.0 + e)
    idx_ref[...] = jnp.where(lane == 0, i1, jnp.where(lane == 1, i2, 0.0)).astype(jnp.int32)
    wt_ref[...] = jnp.where(lane == 0, w1, jnp.where(lane == 1, e * w1, 0.0))


def moe_router(x, g, mod, mod_row, router_w, router_b, tm=512):
    m = x.shape[0]
    rw = jnp.zeros((D_MODEL, V7X_LANES), F32).at[:, :N_EXPERTS].set(router_w)
    rb = jnp.zeros((1, V7X_LANES), F32).at[0, :N_EXPERTS].set(router_b)
    return pl.pallas_call(
        _router_kernel,
        grid=(m // tm,),
        in_specs=[pl.BlockSpec((tm, D_MODEL), lambda t: (t, 0)),
                  pl.BlockSpec((1, D_MODEL), lambda t: (0, 0)),
                  pl.BlockSpec((None, 6, D_MODEL), lambda t: (mod_row(t), 0, 0)),
                  pl.BlockSpec((D_MODEL, V7X_LANES), lambda t: (0, 0)),
                  pl.BlockSpec((1, V7X_LANES), lambda t: (0, 0))],
        out_specs=[pl.BlockSpec((tm, D_MODEL), lambda t: (t, 0)),
                   pl.BlockSpec((tm, V7X_LANES), lambda t: (t, 0)),
                   pl.BlockSpec((tm, V7X_LANES), lambda t: (t, 0))],
        out_shape=[jax.ShapeDtypeStruct((m, D_MODEL), F32),
                   jax.ShapeDtypeStruct((m, V7X_LANES), jnp.int32),
                   jax.ShapeDtypeStruct((m, V7X_LANES), F32)],
        compiler_params=_cparams("parallel"),
        name="moe_router",
    )(x, g.reshape(1, D_MODEL), mod, rw, rb)


def _row_copy(src_hbm, src_row, dst, dst_row, sem):
    return pltpu.make_async_copy(src_hbm.at[pl.ds(src_row, 1)], dst.at[pl.ds(dst_row, 1)], sem)


def _gather_tile(src_hbm, idx_ref, n_rows, per_row, buf, sem, action):
    def body(i, carry):
        for k in range(per_row):
            cp = _row_copy(src_hbm, idx_ref[0, per_row * i + k], buf.at[k], i, sem)
            cp.start() if action == "start" else cp.wait()
        return carry

    lax.fori_loop(0, n_rows, body, 0)


def _dispatch_kernel(nv_ref, idx_ref, idx_next_ref, h_hbm, o_ref, buf, sem):
    t = pl.program_id(0)
    rows = o_ref.shape[0]
    slot = t % 2

    def start_tile(tile_idx_ref, tile, s):
        nv = nv_ref[tile]

        @pl.when(nv < rows)
        def _():
            buf[s] = jnp.zeros(buf.shape[1:], buf.dtype)

        _gather_tile(h_hbm, tile_idx_ref, nv, 1, buf.at[s], sem.at[s], "start")

    @pl.when(t == 0)
    def _():
        start_tile(idx_ref, 0, 0)

    @pl.when(t + 1 < pl.num_programs(0))
    def _():
        start_tile(idx_next_ref, t + 1, 1 - slot)

    _gather_tile(h_hbm, idx_ref, nv_ref[t], 1, buf.at[slot], sem.at[slot], "wait")
    o_ref[...] = buf[slot, 0].astype(o_ref.dtype)


def moe_dispatch(h, slot_t, tile_valid):
    n_slots = slot_t.shape[0]
    rows = MOE_GATHER_ROWS
    n_tiles = n_slots // rows
    idx = slot_t.reshape(n_tiles, 1, rows)
    grid_spec = pltpu.PrefetchScalarGridSpec(
        num_scalar_prefetch=1,
        grid=(n_tiles,),
        in_specs=[pl.BlockSpec((None, 1, rows), lambda t, nv: (t, 0, 0), memory_space=pltpu.SMEM),
                  pl.BlockSpec((None, 1, rows), lambda t, nv: (jnp.minimum(t + 1, n_tiles - 1), 0, 0),
                               memory_space=pltpu.SMEM),
                  pl.BlockSpec(memory_space=pl.ANY)],
        out_specs=pl.BlockSpec((rows, D_MODEL), lambda t, nv: (t, 0)),
        scratch_shapes=[pltpu.VMEM((2, 1, rows, D_MODEL), F32), pltpu.SemaphoreType.DMA((2,))],
    )
    return pl.pallas_call(
        _dispatch_kernel,
        grid_spec=grid_spec,
        out_shape=jax.ShapeDtypeStruct((n_slots, D_MODEL), BF),
        compiler_params=_cparams("arbitrary"),
        name="moe_dispatch",
    )(tile_valid, idx, idx, h)


def _moe_ffn_kernel(be_ref, nv_ref, nu_ref, x_ref, wg_ref, wu_ref, wd_ref, o_ref, act):
    s = pl.program_id(1)
    nv = nv_ref[pl.program_id(0)]
    n_up = D_FF // MOE_UP_TILE
    n_sub = (nv + MOE_SUB - 1) // MOE_SUB

    @pl.when(s < n_up)
    def _():
        cols = pl.ds(pl.multiple_of(s * MOE_UP_TILE, MOE_UP_TILE), MOE_UP_TILE)

        def up(n_rows):
            rows = slice(0, n_rows)
            g = _dot_cast(x_ref, rows, wg_ref, MOE_CAST_CHUNK)
            u = _dot_cast(x_ref, rows, wu_ref, MOE_CAST_CHUNK)
            act[rows, cols] = (_silu(g) * u).astype(BF)

        for parts in range(1, MOE_ROWS // MOE_SUB + 1):
            pl.when(n_sub == parts)(functools.partial(up, parts * MOE_SUB))

    @pl.when(s >= n_up)
    def _():
        def down(n_rows):
            if n_rows:
                o_ref[0:n_rows, :] = _dot_cast(act, slice(0, n_rows), wd_ref, MOE_CAST_CHUNK)
            if n_rows < MOE_ROWS:
                o_ref[n_rows:, :] = jnp.zeros((MOE_ROWS - n_rows, MOE_DOWN_TILE), F32)

        for parts in range(0, MOE_ROWS // MOE_SUB + 1):
            pl.when(n_sub == parts)(functools.partial(down, parts * MOE_SUB))


def moe_expert_ffn(xb, block_e, n_valid, n_used, w_gate, w_up, w_down):
    n_blk = xb.shape[0] // MOE_ROWS
    n_up = D_FF // MOE_UP_TILE
    n_down = D_MODEL // MOE_DOWN_TILE

    def last_used(b, nu):
        return jnp.minimum(b, nu[0] - 1)

    def up_idx(b, s, nu):
        return jnp.where(b < nu[0], jnp.minimum(s, n_up - 1), n_up - 1)

    def down_idx(b, s, nu):
        return jnp.where(b < nu[0], jnp.maximum(s - n_up, 0), n_down - 1)

    grid_spec = pltpu.PrefetchScalarGridSpec(
        num_scalar_prefetch=3,
        grid=(n_blk, n_up + n_down),
        in_specs=[
            pl.BlockSpec((MOE_ROWS, D_MODEL), lambda b, s, be, nv, nu: (last_used(b, nu), 0),
                         pipeline_mode=pl.Buffered(1)),
            pl.BlockSpec((None, D_MODEL, MOE_UP_TILE),
                         lambda b, s, be, nv, nu: (be[last_used(b, nu)], 0, up_idx(b, s, nu))),
            pl.BlockSpec((None, D_MODEL, MOE_UP_TILE),
                         lambda b, s, be, nv, nu: (be[last_used(b, nu)], 0, up_idx(b, s, nu))),
            pl.BlockSpec((None, D_FF, MOE_DOWN_TILE),
                         lambda b, s, be, nv, nu: (be[last_used(b, nu)], 0, down_idx(b, s, nu))),
        ],
        out_specs=pl.BlockSpec((MOE_ROWS, MOE_DOWN_TILE),
                               lambda b, s, be, nv, nu: (b, jnp.maximum(s - n_up, 0))),
        scratch_shapes=[pltpu.VMEM((MOE_ROWS, D_FF), BF)],
    )
    return pl.pallas_call(
        _moe_ffn_kernel,
        grid_spec=grid_spec,
        out_shape=jax.ShapeDtypeStruct((n_blk * MOE_ROWS, D_MODEL), F32),
        compiler_params=_cparams("arbitrary", "arbitrary"),
        name="moe_expert_ffn",
    )(block_e, n_valid, n_used, xb, w_gate, w_up, w_down)


def _moe_combine_kernel(idx_ref, idx_next_ref, x_ref, wt_ref, mod_ref, g_ref, yb_hbm, o_ref, buf, sem,
                        *, final_norm):
    t = pl.program_id(0)
    tm = x_ref.shape[0]
    slot = t % 2

    @pl.when(t == 0)
    def _():
        _gather_tile(yb_hbm, idx_ref, tm, TOP_K, buf.at[0], sem.at[0], "start")

    @pl.when(t + 1 < pl.num_programs(0))
    def _():
        _gather_tile(yb_hbm, idx_next_ref, tm, TOP_K, buf.at[1 - slot], sem.at[1 - slot], "start")

    _gather_tile(yb_hbm, idx_ref, tm, TOP_K, buf.at[slot], sem.at[slot], "wait")
    w = wt_ref[...]
    y = w[:, 0:1] * buf[slot, 0] + w[:, 1:2] * buf[slot, 1]
    out = x_ref[...] + mod_ref[5:6, :] * y
    if final_norm:
        ms = jnp.mean(out * out, axis=-1, keepdims=True)
        out = out * lax.rsqrt(ms + NORM_EPS) * g_ref[...]
    o_ref[...] = out


def moe_combine(x, yb, dest, wt, mod, mod_row, final_g, tm=256):
    m = x.shape[0]
    tm = min(tm, m)
    n_tiles = m // tm
    row = pl.BlockSpec((tm, D_MODEL), lambda t: (t, 0))
    final_norm = final_g is not None
    g = final_g if final_norm else jnp.ones((D_MODEL,), F32)
    idx = dest.reshape(n_tiles, 1, TOP_K * tm)
    return pl.pallas_call(
        functools.partial(_moe_combine_kernel, final_norm=final_norm),
        grid=(n_tiles,),
        in_specs=[pl.BlockSpec((None, 1, TOP_K * tm), lambda t: (t, 0, 0), memory_space=pltpu.SMEM),
                  pl.BlockSpec((None, 1, TOP_K * tm), lambda t: (jnp.minimum(t + 1, n_tiles - 1), 0, 0),
                               memory_space=pltpu.SMEM),
                  row,
                  pl.BlockSpec((tm, V7X_LANES), lambda t: (t, 0)),
                  pl.BlockSpec((None, 6, D_MODEL), lambda t: (mod_row(t), 0, 0)),
                  pl.BlockSpec((1, D_MODEL), lambda t: (0, 0)),
                  pl.BlockSpec(memory_space=pl.ANY)],
        out_specs=row,
        out_shape=jax.ShapeDtypeStruct((m, D_MODEL), F32),
        scratch_shapes=[pltpu.VMEM((2, TOP_K, tm, D_MODEL), F32), pltpu.SemaphoreType.DMA((2,))],
        compiler_params=_cparams("arbitrary"),
        name="moe_combine",
    )(idx, idx, x, wt, mod, g.reshape(1, D_MODEL), yb)


def moe_layout(top_i):
    t = top_i.shape[0]
    n_assign = t * TOP_K
    flat_e = top_i.reshape(-1)
    onehot = (flat_e[:, None] == jnp.arange(N_EXPERTS, dtype=jnp.int32)[None, :]).astype(jnp.int32)
    csum = jnp.cumsum(onehot, axis=0)
    rank = jnp.take_along_axis(csum, flat_e[:, None], axis=1)[:, 0] - 1
    counts = csum[-1]
    padded = (counts + MOE_ROWS - 1) // MOE_ROWS * MOE_ROWS
    pends = jnp.cumsum(padded)
    pstarts = pends - padded
    dest = pstarts[flat_e] + rank
    n_blk = n_assign // MOE_ROWS + N_EXPERTS
    flat_t = jnp.repeat(jnp.arange(t, dtype=jnp.int32), TOP_K)
    slot_t = jnp.zeros((n_blk * MOE_ROWS,), jnp.int32).at[dest].set(flat_t)
    blk = jnp.arange(n_blk, dtype=jnp.int32)
    blk_start = blk * MOE_ROWS
    block_e = jnp.minimum(jnp.sum((pends[None, :] <= blk_start[:, None]).astype(jnp.int32), axis=1),
                          N_EXPERTS - 1)
    n_used = pends[-1] // MOE_ROWS
    n_valid = jnp.clip(counts[block_e] - (blk_start - pstarts[block_e]), 0, MOE_ROWS)
    n_valid = jnp.where(blk < n_used, n_valid, 0)
    tiles_per_blk = MOE_ROWS // MOE_GATHER_ROWS
    tile_off = jnp.arange(tiles_per_blk, dtype=jnp.int32) * MOE_GATHER_ROWS
    tile_valid = jnp.clip(n_valid[:, None] - tile_off[None, :], 0, MOE_GATHER_ROWS).reshape(-1)
    return (slot_t, dest.reshape(t, TOP_K).astype(jnp.int32), block_e.astype(jnp.int32),
            n_valid.astype(jnp.int32), n_used.astype(jnp.int32).reshape(1), tile_valid.astype(jnp.int32))


def moe_layer(x, norm_g, mod, mod_row_tm, router_w, router_b, w_gate, w_up, w_down, final_g):
    router_tm, combine_tm = 512, 256
    h, idx, wt = moe_router(x, norm_g, mod, mod_row_tm(router_tm), router_w, router_b, router_tm)
    slot_t, dest, block_e, n_valid, n_used, tile_valid = moe_layout(idx[:, :TOP_K])
    xb = moe_dispatch(h, slot_t, tile_valid)
    yb = moe_expert_ffn(xb, block_e, n_valid, n_used, w_gate, w_up, w_down)
    return moe_combine(x, yb, dest, wt, mod, mod_row_tm(combine_tm), final_g, combine_tm)


def split_w_in(w_in):
    head = w_in[:, :, :D_HEAD].astype(BF)
    body = jnp.pad(w_in[:, :, R_GQA_Q:R_END], ((0, 0), (0, 0), (0, D_TAIL - C_LOW)))
    low = jnp.pad(w_in[:, :, R_LOW:R_GQA_Q], ((0, 0), (0, 0), (C_LOW, D_TAIL - C_LOW - (R_GQA_Q - R_LOW))))
    return head, (body + low).astype(BF)


def kernel(x, c, ctx, c_ctx, w_ada, b_ada, norm1_g, norm2_g, w_in, na_rpb, gla_w_a2, gla_b_a, gla_norm_g, gqa_qn_g, gqa_kn_g, w_pa, w_pb, w_pc, w_out, dense_w_gate, dense_w_up, dense_w_down, router_w, router_b, moe_w_gate, moe_w_up, moe_w_down, final_norm_g):
    n_batch = x.shape[0]
    xl = x.reshape(n_batch * SEQ, D_MODEL)
    xc = ctx.reshape(n_batch * CTX_LEN, D_MODEL)
    cvec = jnp.zeros((8, D_MODEL), F32).at[:n_batch].set(c).at[n_batch].set(c_ctx)
    mods = ada_modulation(cvec, w_ada, b_ada)
    cos_t, sin_t = rope_tables()
    lat_row = _latent_mod_row
    ctx_row = _ctx_mod_row(n_batch)
    prep_tm = 256
    lat_table = lambda t: t % (SEQ // prep_tm)
    ctx_table = lambda t: SEQ // prep_tm

    w_head, w_tail = split_w_in(w_in)
    w_pa, w_pb, w_pc, w_out = (w.astype(BF) for w in (w_pa, w_pb, w_pc, w_out))
    dense_w_down = dense_w_down.astype(BF)

    for i in range(DEPTH):
        last = i == DEPTH - 1
        mod = mods[i]
        head_l, tail_l = in_proj(xl, norm1_g[i], mod, lat_row(1024), w_head, w_tail, i)
        head_c, tail_c = in_proj(xc, norm1_g[i], mod, ctx_row, w_head, w_tail, i)

        a_l = na_latent(head_l, head_c, na_bias_table(na_rpb[i]), n_batch)
        b_l, b_c = gla_bidirectional(head_l, head_c, tail_l, tail_c, gla_w_a2[i], gla_b_a[i],
                                     gla_norm_g[i], n_batch)
        q_l, k_l = gqa_prep(tail_l, cos_t, sin_t, gqa_qn_g[i], gqa_kn_g[i], lat_table, prep_tm)
        q_c, k_c = gqa_prep(tail_c, cos_t, sin_t, gqa_qn_g[i], gqa_kn_g[i], ctx_table, prep_tm)
        c_l = gqa_latent(q_l, k_l, k_c, tail_l, tail_c, n_batch)

        m_l = merge_branches(a_l, b_l, c_l, tail_l, w_pa, w_pb, w_pc, i)
        xl = matmul_residual(m_l, w_out, i, xl, mod, lat_row(1024), 2, tn=512)
        if not last:
            a_c = na_context(head_c, n_batch)
            c_c = gqa_context(q_c, k_c, tail_c, n_batch)
            m_c = merge_branches(a_c, b_c, c_c, tail_c, w_pa, w_pb, w_pc, i)
            xc = matmul_residual(m_c, w_out, i, xc, mod, ctx_row, 2, tn=512)

        j = i // 2
        if i % 2 == 0:
            def ffn(xs, mod_row_tm):
                u = ffn_up(xs, norm2_g[i], mod, mod_row_tm(1024), dense_w_gate[j], dense_w_up[j])
                return matmul_residual(u, dense_w_down, j, xs, mod, mod_row_tm(1024), 5)
            xl = ffn(xl, lat_row)
            if not last:
                xc = ffn(xc, lambda tm: ctx_row)
        else:
            fin = final_norm_g if last else None
            xl = moe_layer(xl, norm2_g[i], mod, lat_row, router_w[j], router_b[j],
                           moe_w_gate[j], moe_w_up[j], moe_w_down[j], fin)
            if not last:
                xc = moe_layer(xc, norm2_g[i], mod, lambda tm: ctx_row, router_w[j], router_b[j],
                               moe_w_gate[j], moe_w_up[j], moe_w_down[j], None)
    if (DEPTH - 1) % 2 == 0:
        xl = final_rmsnorm(xl, final_norm_g)
    return xl.reshape(n_batch, SEQ, D_MODEL)
```

```python
import functools

import jax
import jax.numpy as jnp
from jax import lax
from jax.experimental import pallas as pl
from jax.experimental.pallas import tpu as pltpu

BF = jnp.bfloat16
F32 = jnp.float32

D_MODEL = 2048
SEQ = 2048
CTX_LEN = 256
DEPTH = 2
GRID_W = 64
GRID_ROWS = SEQ // GRID_W
NA_HEADS = 16
NA_HEAD_DIM = 64
NA_WIN_ROWS = 8
NA_WIN_COLS = 16
GLA_HEADS = 4
GLA_DK = 128
GLA_DV = 256
GLA_GATE_RANK = 16
GLA_TAU = 16.0
GQA_HEADS = 8
GQA_KV_HEADS = 2
GQA_HEAD_DIM = 128
GQA_GROUP = GQA_HEADS // GQA_KV_HEADS
ROPE_THETA = 10000.0
D_FF = 5632
N_EXPERTS = 8
TOP_K = 2
NORM_EPS = 1e-6
BRANCH_W = 1024
LOG2_E = 1.4426950408889634

C_NA_Q, C_NA_K, C_NA_V = 0, 1024, 2048
C_GLA_Q, C_GLA_K, C_GLA_V, C_GLA_OG = 3072, 3584, 4096, 5120
D_HEAD = 6144
C_GQA_Q, C_GQA_K, C_GQA_V = 0, 1024, 1280
C_GATE_A, C_GATE_B, C_GATE_C = 1536, 3584, 5632
C_LOW = 7680
D_TAIL = 8192
R_LOW = 6144
R_GQA_Q = 6176
R_END = 13856

V7X_LANES = 128
V7X_VMEM_LIMIT_BYTES = 56 * 1024 * 1024

NA_ROWS_PER_STEP = 2
GLA_STEP = 128
GLA_SUB = 16
GLA_HEADS_PER_STEP = 2
MOE_ROWS = 1536
MOE_SUB = 512
MOE_UP_TILE = 256
MOE_DOWN_TILE = 256
MOE_GATHER_ROWS = 256
MOE_CAST_CHUNK = 512


def _cparams(*sem):
    return pltpu.CompilerParams(dimension_semantics=sem,
                                vmem_limit_bytes=V7X_VMEM_LIMIT_BYTES)


def _dot(a, b):
    return jnp.dot(a, b, preferred_element_type=F32)


def _dot_nt(a, b):
    return lax.dot_general(a, b, (((1,), (1,)), ((), ())), preferred_element_type=F32)


def _dot_cast(a_ref, rows, w_ref, chunk):
    acc = None
    for k0 in range(0, w_ref.shape[0], chunk):
        part = _dot(a_ref[rows, k0:k0 + chunk], w_ref[k0:k0 + chunk, :].astype(BF))
        acc = part if acc is None else acc + part
    return acc


def _silu(x):
    return x * jax.nn.sigmoid(x)


def _latent_mod_row(tm):
    return lambda t: (t * tm) // SEQ


def _ctx_mod_row(n_batch):
    return lambda t: n_batch


def _ada_kernel(c_ref, w_ref, b_ref, o_ref):
    a = _silu(c_ref[...])
    o_ref[...] = _dot(a.astype(BF), w_ref[...].astype(BF)) + b_ref[...]


def ada_modulation(cvec, w_ada, b_ada):
    tn = 1024
    n6 = 6 * D_MODEL
    out = pl.pallas_call(
        _ada_kernel,
        grid=(DEPTH, n6 // tn),
        in_specs=[
            pl.BlockSpec((8, D_MODEL), lambda l, j: (0, 0)),
            pl.BlockSpec((None, D_MODEL, tn), lambda l, j: (l, 0, j)),
            pl.BlockSpec((None, 1, tn), lambda l, j: (l, 0, j)),
        ],
        out_specs=pl.BlockSpec((None, 8, tn), lambda l, j: (l, 0, j)),
        out_shape=jax.ShapeDtypeStruct((DEPTH, 8, n6), F32),
        compiler_params=_cparams("parallel", "parallel"),
        name="ada_modulation",
    )(cvec, w_ada, b_ada.reshape(DEPTH, 1, n6))
    return out.reshape(DEPTH, 8, 6, D_MODEL)


def _modulated_norm(x, g, mod, shift_idx, scale_idx):
    ms = jnp.mean(x * x, axis=-1, keepdims=True)
    y = x * lax.rsqrt(ms + NORM_EPS) * g
    return y * (1.0 + mod[scale_idx:scale_idx + 1]) + mod[shift_idx:shift_idx + 1]


def _rmsnorm_kernel(x_ref, g_ref, o_ref):
    x = x_ref[...]
    ms = jnp.mean(x * x, axis=-1, keepdims=True)
    o_ref[...] = x * lax.rsqrt(ms + NORM_EPS) * g_ref[...]


def final_rmsnorm(x, g, tm=512):
    m = x.shape[0]
    return pl.pallas_call(
        _rmsnorm_kernel,
        grid=(m // tm,),
        in_specs=[pl.BlockSpec((tm, D_MODEL), lambda t: (t, 0)),
                  pl.BlockSpec((1, D_MODEL), lambda t: (0, 0))],
        out_specs=pl.BlockSpec((tm, D_MODEL), lambda t: (t, 0)),
        out_shape=jax.ShapeDtypeStruct((m, D_MODEL), F32),
        compiler_params=_cparams("parallel"),
        name="final_rmsnorm",
    )(x, g.reshape(1, D_MODEL))


def _in_proj_kernel(x_ref, g_ref, mod_ref, wh_ref, wt_ref, oh_ref, ot_ref, h_s, *, n_head):
    j = pl.program_id(1)

    @pl.when(j == 0)
    def _():
        h_s[...] = _modulated_norm(x_ref[...], g_ref[...], mod_ref[...], 0, 1).astype(h_s.dtype)

    @pl.when(j < n_head)
    def _():
        oh_ref[...] = _dot(h_s[...], wh_ref[...]).astype(oh_ref.dtype)

    @pl.when(j >= n_head)
    def _():
        ot_ref[...] = _dot(h_s[...], wt_ref[...]).astype(ot_ref.dtype)


def in_proj(x, g, mod, mod_row, w_head, w_tail, layer, tm=1024, tn=512):
    m = x.shape[0]
    tm = min(tm, m)
    n_head, n_tail = D_HEAD // tn, D_TAIL // tn
    head_col = lambda j: jnp.minimum(j, n_head - 1)
    tail_col = lambda j: jnp.maximum(j - n_head, 0)
    return pl.pallas_call(
        functools.partial(_in_proj_kernel, n_head=n_head),
        grid=(m // tm, n_head + n_tail),
        in_specs=[pl.BlockSpec((tm, D_MODEL), lambda i, j: (i, 0)),
                  pl.BlockSpec((1, D_MODEL), lambda i, j: (0, 0)),
                  pl.BlockSpec((None, 6, D_MODEL), lambda i, j: (mod_row(i), 0, 0)),
                  pl.BlockSpec((None, D_MODEL, tn), lambda i, j: (layer, 0, head_col(j))),
                  pl.BlockSpec((None, D_MODEL, tn), lambda i, j: (layer, 0, tail_col(j)))],
        out_specs=[pl.BlockSpec((tm, tn), lambda i, j: (i, head_col(j))),
                   pl.BlockSpec((tm, tn), lambda i, j: (i, tail_col(j)))],
        out_shape=[jax.ShapeDtypeStruct((m, D_HEAD), BF), jax.ShapeDtypeStruct((m, D_TAIL), BF)],
        scratch_shapes=[pltpu.VMEM((tm, D_MODEL), BF)],
        compiler_params=_cparams("parallel", "arbitrary"),
        name="in_proj",
    )(x, g.reshape(1, D_MODEL), mod, w_head, w_tail)


def _mm_res_kernel(a_ref, w_ref, x_ref, mod_ref, o_ref, *, gate_idx):
    y = _dot(a_ref[...], w_ref[...].astype(BF))
    o_ref[...] = x_ref[...] + mod_ref[gate_idx:gate_idx + 1, :] * y


def matmul_residual(a, w, layer, x, mod, mod_row, gate_idx, tm=1024, tn=256):
    m, k = a.shape
    n = w.shape[2]
    tm = min(tm, m)
    return pl.pallas_call(
        functools.partial(_mm_res_kernel, gate_idx=gate_idx),
        grid=(m // tm, n // tn),
        in_specs=[pl.BlockSpec((tm, k), lambda i, j: (i, 0)),
                  pl.BlockSpec((None, k, tn), lambda i, j: (layer, 0, j)),
                  pl.BlockSpec((tm, tn), lambda i, j: (i, j)),
                  pl.BlockSpec((None, 6, tn), lambda i, j: (mod_row(i), 0, j))],
        out_specs=pl.BlockSpec((tm, tn), lambda i, j: (i, j)),
        out_shape=jax.ShapeDtypeStruct((m, n), F32),
        compiler_params=_cparams("parallel", "parallel"),
        name="matmul_residual",
    )(a, w, x, mod)


def _merge_kernel(a_ref, b_ref, c_ref, ga_ref, gb_ref, gc_ref, wa_ref, wb_ref, wc_ref, o_ref):
    def branch(x_ref, g_ref, w_ref):
        return jax.nn.sigmoid(g_ref[...].astype(F32)) * _dot(x_ref[...], w_ref[...].astype(BF))

    o_ref[...] = (branch(a_ref, ga_ref, wa_ref) + branch(b_ref, gb_ref, wb_ref)
                  + branch(c_ref, gc_ref, wc_ref)).astype(o_ref.dtype)


def merge_branches(a, b, c, proj, w_pa, w_pb, w_pc, layer, tm=1024, tn=512):
    m = a.shape[0]
    tm = min(tm, m)
    x_spec = pl.BlockSpec((tm, BRANCH_W), lambda i, j: (i, 0))
    w_spec = pl.BlockSpec((None, BRANCH_W, tn), lambda i, j: (layer, 0, j))

    def gate_spec(col):
        return pl.BlockSpec((tm, tn), lambda i, j: (i, col // tn + j))

    return pl.pallas_call(
        _merge_kernel,
        grid=(m // tm, D_MODEL // tn),
        in_specs=[x_spec, x_spec, x_spec,
                  gate_spec(C_GATE_A), gate_spec(C_GATE_B), gate_spec(C_GATE_C),
                  w_spec, w_spec, w_spec],
        out_specs=pl.BlockSpec((tm, tn), lambda i, j: (i, j)),
        out_shape=jax.ShapeDtypeStruct((m, D_MODEL), BF),
        compiler_params=_cparams("parallel", "parallel"),
        name="merge_branches",
    )(a, b, c, proj, proj, proj, w_pa, w_pb, w_pc)


def _ffn_up_kernel(x_ref, g_ref, mod_ref, wg_ref, wu_ref, o_ref, h_s):
    @pl.when(pl.program_id(1) == 0)
    def _():
        h_s[...] = _modulated_norm(x_ref[...], g_ref[...], mod_ref[...], 3, 4).astype(h_s.dtype)

    g = _dot_cast(h_s, slice(None), wg_ref, 512)
    u = _dot_cast(h_s, slice(None), wu_ref, 512)
    o_ref[...] = (_silu(g) * u).astype(o_ref.dtype)


def ffn_up(x, norm_g, mod, mod_row, w_gate, w_up, tm=1024, tn=512):
    m = x.shape[0]
    tm = min(tm, m)
    return pl.pallas_call(
        _ffn_up_kernel,
        grid=(m // tm, D_FF // tn),
        in_specs=[pl.BlockSpec((tm, D_MODEL), lambda i, j: (i, 0)),
                  pl.BlockSpec((1, D_MODEL), lambda i, j: (0, 0)),
                  pl.BlockSpec((None, 6, D_MODEL), lambda i, j: (mod_row(i), 0, 0)),
                  pl.BlockSpec((D_MODEL, tn), lambda i, j: (0, j)),
                  pl.BlockSpec((D_MODEL, tn), lambda i, j: (0, j))],
        out_specs=pl.BlockSpec((tm, tn), lambda i, j: (i, j)),
        out_shape=jax.ShapeDtypeStruct((m, D_FF), BF),
        scratch_shapes=[pltpu.VMEM((tm, D_MODEL), BF)],
        compiler_params=_cparams("parallel", "arbitrary"),
        name="ffn_up",
    )(x, norm_g.reshape(1, D_MODEL), mod, w_gate, w_up)


def _gqa_prep_kernel(q_ref, k_ref, cos_ref, sin_ref, qg_ref, kg_ref, qo_ref, ko_ref):
    cos = cos_ref[...]
    sin = sin_ref[...]
    lane = lax.broadcasted_iota(jnp.int32, cos.shape, 1)
    even = (lane & 1) == 0

    def prep(x_ref, g_ref, o_ref, heads):
        for h in range(heads):
            sl = slice(h * GQA_HEAD_DIM, (h + 1) * GQA_HEAD_DIM)
            x = x_ref[:, sl].astype(F32)
            ms = jnp.mean(x * x, axis=-1, keepdims=True)
            y = x * lax.rsqrt(ms + NORM_EPS) * g_ref[...]
            swapped = jnp.where(even, pltpu.roll(y, GQA_HEAD_DIM - 1, 1), pltpu.roll(y, 1, 1))
            o_ref[:, sl] = (y * cos + swapped * sin).astype(o_ref.dtype)

    prep(q_ref, qg_ref, qo_ref, GQA_HEADS)
    prep(k_ref, kg_ref, ko_ref, GQA_KV_HEADS)


def gqa_prep(proj, cos_t, sin_t, qn_g, kn_g, table_block, tm=256):
    m = proj.shape[0]
    qw = GQA_HEADS * GQA_HEAD_DIM
    kw = GQA_KV_HEADS * GQA_HEAD_DIM
    return pl.pallas_call(
        _gqa_prep_kernel,
        grid=(m // tm,),
        in_specs=[pl.BlockSpec((tm, qw), lambda t: (t, C_GQA_Q // qw)),
                  pl.BlockSpec((tm, kw), lambda t: (t, C_GQA_K // kw)),
                  pl.BlockSpec((tm, GQA_HEAD_DIM), lambda t: (table_block(t), 0)),
                  pl.BlockSpec((tm, GQA_HEAD_DIM), lambda t: (table_block(t), 0)),
                  pl.BlockSpec((1, GQA_HEAD_DIM), lambda t: (0, 0)),
                  pl.BlockSpec((1, GQA_HEAD_DIM), lambda t: (0, 0))],
        out_specs=[pl.BlockSpec((tm, qw), lambda t: (t, 0)),
                   pl.BlockSpec((tm, kw), lambda t: (t, 0))],
        out_shape=[jax.ShapeDtypeStruct((m, qw), BF), jax.ShapeDtypeStruct((m, kw), BF)],
        compiler_params=_cparams("parallel"),
        name="gqa_prep",
    )(proj, proj, cos_t, sin_t, qn_g.reshape(1, -1), kn_g.reshape(1, -1))


def rope_tables():
    half = GQA_HEAD_DIM // 2
    freqs = ROPE_THETA ** (-jnp.arange(0, half, 2, dtype=F32) / half)
    t = jnp.arange(SEQ)
    row = (t // GRID_W).astype(F32)
    col = (t % GRID_W).astype(F32)
    ang = jnp.concatenate([row[:, None] * freqs, col[:, None] * freqs], axis=-1)
    cos = jnp.repeat(jnp.cos(ang), 2, axis=-1)
    sin = jnp.repeat(jnp.sin(ang), 2, axis=-1)
    sign = jnp.tile(jnp.array([-1.0, 1.0], F32), half)
    cos = jnp.concatenate([cos, jnp.ones((256, GQA_HEAD_DIM), F32)], axis=0)
    sin = jnp.concatenate([sin * sign, jnp.zeros((256, GQA_HEAD_DIM), F32)], axis=0)
    return cos, sin


def _attend(q, kv_list, scale):
    scores = []
    for k, _, bias in kv_list:
        s = _dot_nt(q, k)
        if bias is not None:
            s = s + bias
        scores.append(s)

    def lane_chunks(xs):
        return [x[:, c:c + V7X_LANES] for x in xs for c in range(0, x.shape[1], V7X_LANES)]

    m = jnp.max(functools.reduce(jnp.maximum, lane_chunks(scores)), axis=-1, keepdims=True)
    ps = [jnp.exp2((s - m) * (scale * LOG2_E)) for s in scores]
    den = jnp.sum(functools.reduce(jnp.add, lane_chunks(ps)), axis=-1, keepdims=True)
    o = functools.reduce(jnp.add, [_dot(p.astype(BF), v) for p, (_, v, _) in zip(ps, kv_list)])
    return o / den


def _head_pair_rows(q2):
    lane = lax.broadcasted_iota(jnp.int32, q2.shape, 1)
    first = lane < NA_HEAD_DIM
    zero = jnp.zeros_like(q2)
    return jnp.concatenate([jnp.where(first, q2, zero), jnp.where(first, zero, q2)], axis=0)


def _head_pair_merge(o, m):
    lane = lax.broadcasted_iota(jnp.int32, (m, 2 * NA_HEAD_DIM), 1)
    return jnp.where(lane < NA_HEAD_DIM, o[:m], o[m:])


def _na_kernel(q_ref, kl_ref, vl_ref, kc_ref, vc_ref, *rest):
    bias_refs, o_ref = rest[:NA_ROWS_PER_STEP], rest[NA_ROWS_PER_STEP]
    n_loc = NA_WIN_ROWS * GRID_W
    scale = NA_HEAD_DIM ** -0.5
    for rr in range(NA_ROWS_PER_STEP):
        r = pl.program_id(1) * NA_ROWS_PER_STEP + rr
        start = jnp.clip(r - NA_WIN_ROWS // 2, 0, GRID_ROWS - NA_WIN_ROWS) * GRID_W
        start = pl.multiple_of(start, GRID_W)
        q_rows = slice(rr * GRID_W, (rr + 1) * GRID_W)
        for hp in range(NA_HEADS // 2):
            sl = slice(hp * 128, (hp + 1) * 128)
            qq = _head_pair_rows(q_ref[q_rows, sl])
            bias = jnp.concatenate([bias_refs[rr][2 * hp], bias_refs[rr][2 * hp + 1]], axis=0)
            o = _attend(qq, [(kl_ref[pl.ds(start, n_loc), sl], vl_ref[pl.ds(start, n_loc), sl], bias),
                             (kc_ref[:, sl], vc_ref[:, sl], None)], scale)
            o_ref[q_rows, sl] = _head_pair_merge(o, GRID_W).astype(o_ref.dtype)


def na_latent(proj_l, proj_c, bias_tab, n_batch):
    w = NA_HEADS * NA_HEAD_DIM
    rps = NA_ROWS_PER_STEP
    steps = GRID_ROWS // rps

    def pattern(r):
        return jnp.where(r < 4, r, jnp.where(r > GRID_ROWS - 4, r - (GRID_ROWS - NA_WIN_ROWS), 4))

    def bias_spec(rr):
        return pl.BlockSpec((None, NA_HEADS, GRID_W, NA_WIN_ROWS * GRID_W),
                            lambda b, i: (pattern(i * rps + rr), 0, 0, 0))

    return pl.pallas_call(
        _na_kernel,
        grid=(n_batch, steps),
        in_specs=[pl.BlockSpec((rps * GRID_W, w), lambda b, i: (b * steps + i, C_NA_Q // w)),
                  pl.BlockSpec((SEQ, w), lambda b, i: (b, C_NA_K // w)),
                  pl.BlockSpec((SEQ, w), lambda b, i: (b, C_NA_V // w)),
                  pl.BlockSpec((CTX_LEN, w), lambda b, i: (b, C_NA_K // w)),
                  pl.BlockSpec((CTX_LEN, w), lambda b, i: (b, C_NA_V // w))]
                 + [bias_spec(rr) for rr in range(rps)],
        out_specs=pl.BlockSpec((rps * GRID_W, w), lambda b, i: (b * steps + i, 0)),
        out_shape=jax.ShapeDtypeStruct((n_batch * SEQ, w), BF),
        compiler_params=_cparams("parallel", "arbitrary"),
        name="na_latent",
    )(proj_l, proj_l, proj_l, proj_c, proj_c, *([bias_tab] * rps))


def na_bias_table(rpb):
    cols = jnp.arange(GRID_W)
    col_start = jnp.clip(cols - NA_WIN_COLS // 2, 0, GRID_W - NA_WIN_COLS)
    in_win = (cols[None, :] >= col_start[:, None]) & (cols[None, :] < col_start[:, None] + NA_WIN_COLS)
    dc = jnp.clip(cols[None, :] - cols[:, None] + NA_WIN_COLS - 1, 0, 2 * NA_WIN_COLS - 2)
    onehot = (dc[None] == jnp.arange(2 * NA_WIN_COLS - 1)[:, None, None]).astype(F32)
    by_col = jnp.einsum('hdc,cqk->hdqk', rpb.astype(F32), onehot, precision=lax.Precision.HIGHEST)
    by_col = jnp.where(in_win[None, None], by_col * (NA_HEAD_DIM ** 0.5), -jnp.inf)
    last = NA_WIN_ROWS - 1
    bias = jnp.stack([by_col[:, last - p:last - p + NA_WIN_ROWS] for p in range(NA_WIN_ROWS)])
    bias = bias.transpose(0, 1, 3, 2, 4)
    return bias.reshape(NA_WIN_ROWS, NA_HEADS, GRID_W, NA_WIN_ROWS * GRID_W)


def _na_ctx_kernel(q_ref, k_ref, v_ref, o_ref):
    scale = NA_HEAD_DIM ** -0.5
    for hp in range(NA_HEADS // 2):
        sl = slice(hp * 128, (hp + 1) * 128)
        o = _attend(_head_pair_rows(q_ref[:, sl]), [(k_ref[:, sl], v_ref[:, sl], None)], scale)
        o_ref[:, sl] = _head_pair_merge(o, CTX_LEN).astype(o_ref.dtype)


def na_context(proj_c, n_batch):
    w = NA_HEADS * NA_HEAD_DIM
    return pl.pallas_call(
        _na_ctx_kernel,
        grid=(n_batch,),
        in_specs=[pl.BlockSpec((CTX_LEN, w), lambda b: (b, C_NA_Q // w)),
                  pl.BlockSpec((CTX_LEN, w), lambda b: (b, C_NA_K // w)),
                  pl.BlockSpec((CTX_LEN, w), lambda b: (b, C_NA_V // w))],
        out_specs=pl.BlockSpec((CTX_LEN, w), lambda b: (b, 0)),
        out_shape=jax.ShapeDtypeStruct((n_batch * CTX_LEN, w), BF),
        compiler_params=_cparams("parallel"),
        name="na_context",
    )(proj_c, proj_c, proj_c)


def _gqa_rows(q_ref):
    return jnp.concatenate([q_ref[:, g * GQA_HEAD_DIM:(g + 1) * GQA_HEAD_DIM]
                            for g in range(GQA_GROUP)], axis=0)


def _gqa_store(o, o_ref):
    tq = o_ref.shape[0]
    for g in range(GQA_GROUP):
        o_ref[:, g * GQA_HEAD_DIM:(g + 1) * GQA_HEAD_DIM] = o[g * tq:(g + 1) * tq].astype(o_ref.dtype)


def _gqa_kernel(q_ref, kl_ref, vl_ref, kc_ref, vc_ref, o_ref):
    for g in range(GQA_GROUP):
        sl = slice(g * GQA_HEAD_DIM, (g + 1) * GQA_HEAD_DIM)
        o = _attend(q_ref[:, sl], [(kl_ref[...], vl_ref[...], None), (kc_ref[...], vc_ref[...], None)],
                    GQA_HEAD_DIM ** -0.5)
        o_ref[:, sl] = o.astype(o_ref.dtype)


def _gqa_ctx_kernel(q_ref, kc_ref, vc_ref, o_ref):
    o = _attend(_gqa_rows(q_ref), [(kc_ref[...], vc_ref[...], None)], GQA_HEAD_DIM ** -0.5)
    _gqa_store(o, o_ref)


def gqa_latent(q_l, k_l, k_c, proj_l, proj_c, n_batch, tq=512):
    gw = GQA_GROUP * GQA_HEAD_DIM
    nq = SEQ // tq
    dh = GQA_HEAD_DIM
    return pl.pallas_call(
        _gqa_kernel,
        grid=(n_batch, GQA_KV_HEADS, nq),
        in_specs=[pl.BlockSpec((tq, gw), lambda b, h, i: (b * nq + i, h)),
                  pl.BlockSpec((SEQ, dh), lambda b, h, i: (b, h)),
                  pl.BlockSpec((SEQ, dh), lambda b, h, i: (b, C_GQA_V // dh + h)),
                  pl.BlockSpec((CTX_LEN, dh), lambda b, h, i: (b, h)),
                  pl.BlockSpec((CTX_LEN, dh), lambda b, h, i: (b, C_GQA_V // dh + h))],
        out_specs=pl.BlockSpec((tq, gw), lambda b, h, i: (b * nq + i, h)),
        out_shape=jax.ShapeDtypeStruct((n_batch * SEQ, GQA_HEADS * dh), BF),
        compiler_params=_cparams("parallel", "parallel", "arbitrary"),
        name="gqa_latent",
    )(q_l, k_l, proj_l, k_c, proj_c)


def gqa_context(q_c, k_c, proj_c, n_batch):
    gw = GQA_GROUP * GQA_HEAD_DIM
    dh = GQA_HEAD_DIM
    return pl.pallas_call(
        _gqa_ctx_kernel,
        grid=(n_batch, GQA_KV_HEADS),
        in_specs=[pl.BlockSpec((CTX_LEN, gw), lambda b, h: (b, h)),
                  pl.BlockSpec((CTX_LEN, dh), lambda b, h: (b, h)),
                  pl.BlockSpec((CTX_LEN, dh), lambda b, h: (b, C_GQA_V // dh + h))],
        out_specs=pl.BlockSpec((CTX_LEN, gw), lambda b, h: (b, h)),
        out_shape=jax.ShapeDtypeStruct((n_batch * CTX_LEN, GQA_HEADS * dh), BF),
        compiler_params=_cparams("parallel", "parallel"),
        name="gqa_context",
    )(q_c, k_c, proj_c)


def _log_sigmoid(z):
    return jnp.minimum(z, 0.0) - jnp.log1p(jnp.exp(-jnp.abs(z)))


def _gla_decay(c, direction, head, low_s, wa_ref, ba_ref):
    n = GLA_STEP
    rows = pl.ds(pl.multiple_of(c * n, n), n)
    z = _dot(low_s[rows, :], wa_ref[direction, head].astype(BF)) + ba_ref[direction, head]
    g = _log_sigmoid(z) * (1.0 / GLA_TAU)
    ri = lax.broadcasted_iota(jnp.int32, (n, n), 0)
    ci = lax.broadcasted_iota(jnp.int32, (n, n), 1)
    tri = (ci <= ri) if direction == 0 else (ci >= ri)
    tri = jnp.where(tri, 1.0, 0.0).astype(BF)
    g_hi = g.astype(BF)
    g_lo = (g - g_hi.astype(F32)).astype(BF)
    return (_dot(tri, g_hi) + _dot(tri, g_lo)) * LOG2_E


def _gla_step(c, direction, head, bc, q_s, k_s, v_s, st_ref, o_s):
    n = GLA_STEP
    r0 = pl.multiple_of(c * n, n)
    rows = pl.ds(r0, n)
    k_lanes = slice(head * GLA_DK, (head + 1) * GLA_DK)
    v_lanes = slice(head * GLA_DV, (head + 1) * GLA_DV)
    qf = q_s[rows, k_lanes].astype(F32) * (GLA_DK ** -0.5)
    k_bf = k_s[rows, k_lanes]
    kf = k_bf.astype(F32)
    v = v_s[rows, v_lanes]
    btot = bc[n - 1:n] if direction == 0 else bc[0:1]

    state = st_ref[...]
    q_in = (qf * jnp.exp2(bc)).astype(BF)
    o = _dot_nt(q_in, state.astype(BF))
    k_out = (kf * jnp.exp2(btot - bc)).astype(BF)
    v_t = v.astype(F32).T.astype(BF)
    st_ref[...] = jnp.exp2(btot) * state + _dot(v_t, k_out)

    across = None
    size = n // 2
    while size >= GLA_SUB:
        pieces = []
        for base in range(0, n, 2 * size):
            mid = base + size
            if direction == 0:
                early, late, ref = slice(base, mid), slice(mid, mid + size), bc[mid - 1:mid]
            else:
                early, late, ref = slice(mid, mid + size), slice(base, mid), bc[mid:mid + 1]
            k_dec = (kf[early] * jnp.exp2(ref - bc[early])).astype(BF)
            k_rows = [jnp.zeros((early.start, GLA_DK), BF)] if early.start else []
            k_rows.append(k_dec)
            if early.stop < n:
                k_rows.append(jnp.zeros((n - early.stop, GLA_DK), BF))
            part = _dot_nt((qf[late] * jnp.exp2(bc[late] - ref)).astype(BF),
                           jnp.concatenate(k_rows, axis=0))
            none = jnp.zeros((size, n), F32)
            pieces += [none, part] if direction == 0 else [part, none]
        level = jnp.concatenate(pieces, axis=0)
        across = level if across is None else across + level
        size //= 2

    lane = lax.broadcasted_iota(jnp.int32, (GLA_SUB, n), 1)
    sub_row = lax.broadcasted_iota(jnp.int32, (GLA_SUB, 1), 0)
    neg_inf = jnp.float32(-jnp.inf)
    blocks = []
    for blk in range(n // GLA_SUB):
        lo, hi = blk * GLA_SUB, (blk + 1) * GLA_SUB
        q_b, b_b, a_b = qf[lo:hi], bc[lo:hi], across[lo:hi]
        decayed = []
        for jl in range(GLA_SUB):
            keep = (sub_row >= jl) if direction == 0 else (sub_row <= jl)
            dec = jnp.exp2(jnp.where(keep, b_b - bc[lo + jl:lo + jl + 1], neg_inf))
            decayed.append((q_b * dec).astype(BF))
        pair = _dot_nt(jnp.concatenate(decayed, axis=0), k_bf)
        for jl in range(GLA_SUB):
            a_b = jnp.where(lane == lo + jl, pair[jl * GLA_SUB:(jl + 1) * GLA_SUB], a_b)
        blocks.append(a_b)
    attn = jnp.concatenate(blocks, axis=0).astype(BF)
    o_s[rows, :] = o + _dot(attn, v)


def _gla_kernel(ql, kl, vl, ogl, lowl, qc, kc, vc, ogc, lowc, wa_ref, ba_ref, ng_ref,
                yl_ref, yc_ref, q_s, k_s, v_s, low_s, *per_head):
    hps = GLA_HEADS_PER_STEP
    of_s, ob_s, stf, stb, bcf, bcb = (per_head[g * hps:(g + 1) * hps] for g in range(6))
    nc = CTX_LEN
    q_s[0:nc, :] = qc[...]
    q_s[nc:, :] = ql[...]
    k_s[0:nc, :] = kc[...]
    k_s[nc:, :] = kl[...]
    v_s[0:nc, :] = vc[...]
    v_s[nc:, :] = vl[...]
    low_s[0:nc, :] = lowc[...]
    low_s[nc:, :] = lowl[...]
    for st in (*stf, *stb):
        st[...] = jnp.zeros_like(st)
    n_ctx = CTX_LEN // GLA_STEP
    n_all = (CTX_LEN + SEQ) // GLA_STEP

    def backward_step(i):
        return jnp.where(i < n_ctx, n_ctx - 1 - i, n_all + n_ctx - 1 - i)

    def store_decays(i):
        for head in range(hps):
            bcf[head][...] = _gla_decay(i, 0, head, low_s, wa_ref, ba_ref)
            bcb[head][...] = _gla_decay(backward_step(i), 1, head, low_s, wa_ref, ba_ref)

    store_decays(0)

    def body(i, carry):
        for head in range(hps):
            _gla_step(i, 0, head, bcf[head][...], q_s, k_s, v_s, stf[head], of_s[head])
            _gla_step(backward_step(i), 1, head, bcb[head][...], q_s, k_s, v_s, stb[head], ob_s[head])
        store_decays(jnp.minimum(i + 1, n_all - 1))
        return carry

    lax.fori_loop(0, n_all, body, 0)

    def finish(lo, hi, og_ref, y_ref):
        for head in range(GLA_HEADS_PER_STEP):
            lanes = slice(head * GLA_DV, (head + 1) * GLA_DV)
            o = of_s[head][lo:hi, :] + ob_s[head][lo:hi, :]
            ms = jnp.mean(o * o, axis=-1, keepdims=True)
            y = o * lax.rsqrt(ms + NORM_EPS) * ng_ref[...]
            y_ref[:, lanes] = (y * _silu(og_ref[:, lanes].astype(F32))).astype(y_ref.dtype)

    finish(0, nc, ogc, yc_ref)
    finish(nc, nc + SEQ, ogl, yl_ref)


def gla_bidirectional(head_l, head_c, tail_l, tail_c, w_a2, b_a, norm_g, n_batch):
    hps = GLA_HEADS_PER_STEP
    dk, dv = GLA_DK, GLA_DV
    kw, vw = hps * dk, hps * dv
    wa = jnp.zeros((2, GLA_HEADS, V7X_LANES, dk), F32)
    for d in range(2):
        wa = wa.at[d, :, d * GLA_GATE_RANK:(d + 1) * GLA_GATE_RANK, :].set(
            w_a2[d].reshape(GLA_GATE_RANK, GLA_HEADS, dk).transpose(1, 0, 2))
    ba = b_a.reshape(2, GLA_HEADS, 1, dk)

    def specs(rows):
        return [pl.BlockSpec((rows, kw), lambda b, h: (b, C_GLA_Q // kw + h)),
                pl.BlockSpec((rows, kw), lambda b, h: (b, C_GLA_K // kw + h)),
                pl.BlockSpec((rows, vw), lambda b, h: (b, C_GLA_V // vw + h)),
                pl.BlockSpec((rows, vw), lambda b, h: (b, C_GLA_OG // vw + h)),
                pl.BlockSpec((rows, V7X_LANES), lambda b, h: (b, C_LOW // V7X_LANES))]

    n_rows = CTX_LEN + SEQ
    return pl.pallas_call(
        _gla_kernel,
        grid=(n_batch, GLA_HEADS // hps),
        in_specs=specs(SEQ) + specs(CTX_LEN) + [
            pl.BlockSpec((2, hps, V7X_LANES, dk), lambda b, h: (0, h, 0, 0)),
            pl.BlockSpec((2, hps, 1, dk), lambda b, h: (0, h, 0, 0)),
            pl.BlockSpec((1, dv), lambda b, h: (0, 0))],
        out_specs=[pl.BlockSpec((SEQ, vw), lambda b, h: (b, h)),
                   pl.BlockSpec((CTX_LEN, vw), lambda b, h: (b, h))],
        out_shape=[jax.ShapeDtypeStruct((n_batch * SEQ, GLA_HEADS * dv), BF),
                   jax.ShapeDtypeStruct((n_batch * CTX_LEN, GLA_HEADS * dv), BF)],
        scratch_shapes=[pltpu.VMEM((n_rows, kw), BF), pltpu.VMEM((n_rows, kw), BF),
                        pltpu.VMEM((n_rows, vw), BF), pltpu.VMEM((n_rows, V7X_LANES), BF)]
                       + [pltpu.VMEM((n_rows, dv), F32)] * (2 * hps)
                       + [pltpu.VMEM((dv, dk), F32)] * (2 * hps)
                       + [pltpu.VMEM((GLA_STEP, dk), F32)] * (2 * hps),
        compiler_params=_cparams("parallel", "parallel"),
        name="gla_bidirectional",
    )(*([head_l] * 4 + [tail_l] + [head_c] * 4 + [tail_c]), wa, ba, norm_g.reshape(1, dv))


def _router_kernel(x_ref, g_ref, mod_ref, rw_ref, rb_ref, h_ref, idx_ref, wt_ref):
    h = _modulated_norm(x_ref[...], g_ref[...], mod_ref[...], 3, 4)
    h_ref[...] = h.astype(h_ref.dtype)
    logits = jnp.dot(h, rw_ref[...], preferred_element_type=F32,
                     precision=lax.Precision.HIGHEST) + rb_ref[...]
    lane = lax.broadcasted_iota(jnp.int32, logits.shape, 1)
    neg_inf = jnp.float32(-jnp.inf)
    logits = jnp.where(lane < N_EXPERTS, logits, neg_inf)
    m1 = jnp.max(logits, axis=-1, keepdims=True)
    lane_f = lane.astype(F32)
    i1 = jnp.min(jnp.where(logits == m1, lane_f, float(V7X_LANES)), axis=-1, keepdims=True)
    rest = jnp.where(lane_f == i1, neg_inf, logits)
    m2 = jnp.max(rest, axis=-1, keepdims=True)
    i2 = jnp.min(jnp.where(rest == m2, lane_f, float(V7X_LANES)), axis=-1, keepdims=True)
    e = jnp.exp(m2 - m1)
    w1 = 1.0 / (1.0 + e)
    idx_ref[...] = jnp.where(lane == 0, i1, jnp.where(lane == 1, i2, 0.0)).astype(jnp.int32)
    wt_ref[...] = jnp.where(lane == 0, w1, jnp.where(lane == 1, e * w1, 0.0))


def moe_router(x, g, mod, mod_row, router_w, router_b, tm=512):
    m = x.shape[0]
    rw = jnp.zeros((D_MODEL, V7X_LANES), F32).at[:, :N_EXPERTS].set(router_w)
    rb = jnp.zeros((1, V7X_LANES), F32).at[0, :N_EXPERTS].set(router_b)
    return pl.pallas_call(
        _router_kernel,
        grid=(m // tm,),
        in_specs=[pl.BlockSpec((tm, D_MODEL), lambda t: (t, 0)),
                  pl.BlockSpec((1, D_MODEL), lambda t: (0, 0)),
                  pl.BlockSpec((None, 6, D_MODEL), lambda t: (mod_row(t), 0, 0)),
                  pl.BlockSpec((D_MODEL, V7X_LANES), lambda t: (0, 0)),
                  pl.BlockSpec((1, V7X_LANES), lambda t: (0, 0))],
        out_specs=[pl.BlockSpec((tm, D_MODEL), lambda t: (t, 0)),
                   pl.BlockSpec((tm, V7X_LANES), lambda t: (t, 0)),
                   pl.BlockSpec((tm, V7X_LANES), lambda t: (t, 0))],
        out_shape=[jax.ShapeDtypeStruct((m, D_MODEL), F32),
                   jax.ShapeDtypeStruct((m, V7X_LANES), jnp.int32),
                   jax.ShapeDtypeStruct((m, V7X_LANES), F32)],
        compiler_params=_cparams("parallel"),
        name="moe_router",
    )(x, g.reshape(1, D_MODEL), mod, rw, rb)


def _row_copy(src_hbm, src_row, dst, dst_row, sem):
    return pltpu.make_async_copy(src_hbm.at[pl.ds(src_row, 1)], dst.at[pl.ds(dst_row, 1)], sem)


def _gather_tile(src_hbm, idx_ref, n_rows, per_row, buf, sem, action):
    def body(i, carry):
        for k in range(per_row):
            cp = _row_copy(src_hbm, idx_ref[0, per_row * i + k], buf.at[k], i, sem)
            cp.start() if action == "start" else cp.wait()
        return carry

    lax.fori_loop(0, n_rows, body, 0)


def _dispatch_kernel(nv_ref, idx_ref, idx_next_ref, h_hbm, o_ref, buf, sem):
    t = pl.program_id(0)
    rows = o_ref.shape[0]
    slot = t % 2

    def start_tile(tile_idx_ref, tile, s):
        nv = nv_ref[tile]

        @pl.when(nv < rows)
        def _():
            buf[s] = jnp.zeros(buf.shape[1:], buf.dtype)

        _gather_tile(h_hbm, tile_idx_ref, nv, 1, buf.at[s], sem.at[s], "start")

    @pl.when(t == 0)
    def _():
        start_tile(idx_ref, 0, 0)

    @pl.when(t + 1 < pl.num_programs(0))
    def _():
        start_tile(idx_next_ref, t + 1, 1 - slot)

    _gather_tile(h_hbm, idx_ref, nv_ref[t], 1, buf.at[slot], sem.at[slot], "wait")
    o_ref[...] = buf[slot, 0].astype(o_ref.dtype)


def moe_dispatch(h, slot_t, tile_valid):
    n_slots = slot_t.shape[0]
    rows = MOE_GATHER_ROWS
    n_tiles = n_slots // rows
    idx = slot_t.reshape(n_tiles, 1, rows)
    grid_spec = pltpu.PrefetchScalarGridSpec(
        num_scalar_prefetch=1,
        grid=(n_tiles,),
        in_specs=[pl.BlockSpec((None, 1, rows), lambda t, nv: (t, 0, 0), memory_space=pltpu.SMEM),
                  pl.BlockSpec((None, 1, rows), lambda t, nv: (jnp.minimum(t + 1, n_tiles - 1), 0, 0),
                               memory_space=pltpu.SMEM),
                  pl.BlockSpec(memory_space=pl.ANY)],
        out_specs=pl.BlockSpec((rows, D_MODEL), lambda t, nv: (t, 0)),
        scratch_shapes=[pltpu.VMEM((2, 1, rows, D_MODEL), F32), pltpu.SemaphoreType.DMA((2,))],
    )
    return pl.pallas_call(
        _dispatch_kernel,
        grid_spec=grid_spec,
        out_shape=jax.ShapeDtypeStruct((n_slots, D_MODEL), BF),
        compiler_params=_cparams("arbitrary"),
        name="moe_dispatch",
    )(tile_valid, idx, idx, h)


def _moe_ffn_kernel(be_ref, nv_ref, nu_ref, x_ref, wg_ref, wu_ref, wd_ref, o_ref, act):
    s = pl.program_id(1)
    nv = nv_ref[pl.program_id(0)]
    n_up = D_FF // MOE_UP_TILE
    n_sub = (nv + MOE_SUB - 1) // MOE_SUB

    @pl.when(s < n_up)
    def _():
        cols = pl.ds(pl.multiple_of(s * MOE_UP_TILE, MOE_UP_TILE), MOE_UP_TILE)

        def up(n_rows):
            rows = slice(0, n_rows)
            g = _dot_cast(x_ref, rows, wg_ref, MOE_CAST_CHUNK)
            u = _dot_cast(x_ref, rows, wu_ref, MOE_CAST_CHUNK)
            act[rows, cols] = (_silu(g) * u).astype(BF)

        for parts in range(1, MOE_ROWS // MOE_SUB + 1):
            pl.when(n_sub == parts)(functools.partial(up, parts * MOE_SUB))

    @pl.when(s >= n_up)
    def _():
        def down(n_rows):
            if n_rows:
                o_ref[0:n_rows, :] = _dot_cast(act, slice(0, n_rows), wd_ref, MOE_CAST_CHUNK)
            if n_rows < MOE_ROWS:
                o_ref[n_rows:, :] = jnp.zeros((MOE_ROWS - n_rows, MOE_DOWN_TILE), F32)

        for parts in range(0, MOE_ROWS // MOE_SUB + 1):
            pl.when(n_sub == parts)(functools.partial(down, parts * MOE_SUB))


def moe_expert_ffn(xb, block_e, n_valid, n_used, w_gate, w_up, w_down):
    n_blk = xb.shape[0] // MOE_ROWS
    n_up = D_FF // MOE_UP_TILE
    n_down = D_MODEL // MOE_DOWN_TILE

    def last_used(b, nu):
        return jnp.minimum(b, nu[0] - 1)

    def up_idx(b, s, nu):
        return jnp.where(b < nu[0], jnp.minimum(s, n_up - 1), n_up - 1)

    def down_idx(b, s, nu):
        return jnp.where(b < nu[0], jnp.maximum(s - n_up, 0), n_down - 1)

    grid_spec = pltpu.PrefetchScalarGridSpec(
        num_scalar_prefetch=3,
        grid=(n_blk, n_up + n_down),
        in_specs=[
            pl.BlockSpec((MOE_ROWS, D_MODEL), lambda b, s, be, nv, nu: (last_used(b, nu), 0),
                         pipeline_mode=pl.Buffered(1)),
            pl.BlockSpec((None, D_MODEL, MOE_UP_TILE),
                         lambda b, s, be, nv, nu: (be[last_used(b, nu)], 0, up_idx(b, s, nu))),
            pl.BlockSpec((None, D_MODEL, MOE_UP_TILE),
                         lambda b, s, be, nv, nu: (be[last_used(b, nu)], 0, up_idx(b, s, nu))),
            pl.BlockSpec((None, D_FF, MOE_DOWN_TILE),
                         lambda b, s, be, nv, nu: (be[last_used(b, nu)], 0, down_idx(b, s, nu))),
        ],
        out_specs=pl.BlockSpec((MOE_ROWS, MOE_DOWN_TILE),
                               lambda b, s, be, nv, nu: (b, jnp.maximum(s - n_up, 0))),
        scratch_shapes=[pltpu.VMEM((MOE_ROWS, D_FF), BF)],
    )
    return pl.pallas_call(
        _moe_ffn_kernel,
        grid_spec=grid_spec,
        out_shape=jax.ShapeDtypeStruct((n_blk * MOE_ROWS, D_MODEL), F32),
        compiler_params=_cparams("arbitrary", "arbitrary"),
        name="moe_expert_ffn",
    )(block_e, n_valid, n_used, xb, w_gate, w_up, w_down)


def _moe_combine_kernel(idx_ref, idx_next_ref, x_ref, wt_ref, mod_ref, g_ref, yb_hbm, o_ref, buf, sem,
                        *, final_norm):
    t = pl.program_id(0)
    tm = x_ref.shape[0]
    slot = t % 2

    @pl.when(t == 0)
    def _():
        _gather_tile(yb_hbm, idx_ref, tm, TOP_K, buf.at[0], sem.at[0], "start")

    @pl.when(t + 1 < pl.num_programs(0))
    def _():
        _gather_tile(yb_hbm, idx_next_ref, tm, TOP_K, buf.at[1 - slot], sem.at[1 - slot], "start")

    _gather_tile(yb_hbm, idx_ref, tm, TOP_K, buf.at[slot], sem.at[slot], "wait")
    w = wt_ref[...]
    y = w[:, 0:1] * buf[slot, 0] + w[:, 1:2] * buf[slot, 1]
    out = x_ref[...] + mod_ref[5:6, :] * y
    if final_norm:
        ms = jnp.mean(out * out, axis=-1, keepdims=True)
        out = out * lax.rsqrt(ms + NORM_EPS) * g_ref[...]
    o_ref[...] = out


def moe_combine(x, yb, dest, wt, mod, mod_row, final_g, tm=256):
    m = x.shape[0]
    tm = min(tm, m)
    n_tiles = m // tm
    row = pl.BlockSpec((tm, D_MODEL), lambda t: (t, 0))
    final_norm = final_g is not None
    g = final_g if final_norm else jnp.ones((D_MODEL,), F32)
    idx = dest.reshape(n_tiles, 1, TOP_K * tm)
    return pl.pallas_call(
        functools.partial(_moe_combine_kernel, final_norm=final_norm),
        grid=(n_tiles,),
        in_specs=[pl.BlockSpec((None, 1, TOP_K * tm), lambda t: (t, 0, 0), memory_space=pltpu.SMEM),
                  pl.BlockSpec((None, 1, TOP_K * tm), lambda t: (jnp.minimum(t + 1, n_tiles - 1), 0, 0),
                               memory_space=pltpu.SMEM),
                  row,
                  pl.BlockSpec((tm, V7X_LANES), lambda t: (t, 0)),
                  pl.BlockSpec((None, 6, D_MODEL), lambda t: (mod_row(t), 0, 0)),
                  pl.BlockSpec((1, D_MODEL), lambda t: (0, 0)),
                  pl.BlockSpec(memory_space=pl.ANY)],
        out_specs=row,
        out_shape=jax.ShapeDtypeStruct((m, D_MODEL), F32),
        scratch_shapes=[pltpu.VMEM((2, TOP_K, tm, D_MODEL), F32), pltpu.SemaphoreType.DMA((2,))],
        compiler_params=_cparams("arbitrary"),
        name="moe_combine",
    )(idx, idx, x, wt, mod, g.reshape(1, D_MODEL), yb)


def moe_layout(top_i):
    t = top_i.shape[0]
    n_assign = t * TOP_K
    flat_e = top_i.reshape(-1)
    onehot = (flat_e[:, None] == jnp.arange(N_EXPERTS, dtype=jnp.int32)[None, :]).astype(jnp.int32)
    csum = jnp.cumsum(onehot, axis=0)
    rank = jnp.take_along_axis(csum, flat_e[:, None], axis=1)[:, 0] - 1
    counts = csum[-1]
    padded = (counts + MOE_ROWS - 1) // MOE_ROWS * MOE_ROWS
    pends = jnp.cumsum(padded)
    pstarts = pends - padded
    dest = pstarts[flat_e] + rank
    n_blk = n_assign // MOE_ROWS + N_EXPERTS
    flat_t = jnp.repeat(jnp.arange(t, dtype=jnp.int32), TOP_K)
    slot_t = jnp.zeros((n_blk * MOE_ROWS,), jnp.int32).at[dest].set(flat_t)
    blk = jnp.arange(n_blk, dtype=jnp.int32)
    blk_start = blk * MOE_ROWS
    block_e = jnp.minimum(jnp.sum((pends[None, :] <= blk_start[:, None]).astype(jnp.int32), axis=1),
                          N_EXPERTS - 1)
    n_used = pends[-1] // MOE_ROWS
    n_valid = jnp.clip(counts[block_e] - (blk_start - pstarts[block_e]), 0, MOE_ROWS)
    n_valid = jnp.where(blk < n_used, n_valid, 0)
    tiles_per_blk = MOE_ROWS // MOE_GATHER_ROWS
    tile_off = jnp.arange(tiles_per_blk, dtype=jnp.int32) * MOE_GATHER_ROWS
    tile_valid = jnp.clip(n_valid[:, None] - tile_off[None, :], 0, MOE_GATHER_ROWS).reshape(-1)
    return (slot_t, dest.reshape(t, TOP_K).astype(jnp.int32), block_e.astype(jnp.int32),
            n_valid.astype(jnp.int32), n_used.astype(jnp.int32).reshape(1), tile_valid.astype(jnp.int32))


def moe_layer(x, norm_g, mod, mod_row_tm, router_w, router_b, w_gate, w_up, w_down, final_g):
    router_tm, combine_tm = 512, 256
    h, idx, wt = moe_router(x, norm_g, mod, mod_row_tm(router_tm), router_w, router_b, router_tm)
    slot_t, dest, block_e, n_valid, n_used, tile_valid = moe_layout(idx[:, :TOP_K])
    xb = moe_dispatch(h, slot_t, tile_valid)
    yb = moe_expert_ffn(xb, block_e, n_valid, n_used, w_gate, w_up, w_down)
    return moe_combine(x, yb, dest, wt, mod, mod_row_tm(combine_tm), final_g, combine_tm)


def split_w_in(w_in):
    head = w_in[:, :, :D_HEAD].astype(BF)
    body = jnp.pad(w_in[:, :, R_GQA_Q:R_END], ((0, 0), (0, 0), (0, D_TAIL - C_LOW)))
    low = jnp.pad(w_in[:, :, R_LOW:R_GQA_Q], ((0, 0), (0, 0), (C_LOW, D_TAIL - C_LOW - (R_GQA_Q - R_LOW))))
    return head, (body + low).astype(BF)


def kernel(x, c, ctx, c_ctx, w_ada, b_ada, norm1_g, norm2_g, w_in, na_rpb, gla_w_a2, gla_b_a, gla_norm_g, gqa_qn_g, gqa_kn_g, w_pa, w_pb, w_pc, w_out, dense_w_gate, dense_w_up, dense_w_down, router_w, router_b, moe_w_gate, moe_w_up, moe_w_down, final_norm_g):
    n_batch = x.shape[0]
    xl = x.reshape(n_batch * SEQ, D_MODEL)
    xc = ctx.reshape(n_batch * CTX_LEN, D_MODEL)
    cvec = jnp.zeros((8, D_MODEL), F32).at[:n_batch].set(c).at[n_batch].set(c_ctx)
    mods = ada_modulation(cvec, w_ada, b_ada)
    cos_t, sin_t = rope_tables()
    lat_row = _latent_mod_row
    ctx_row = _ctx_mod_row(n_batch)
    prep_tm = 256
    lat_table = lambda t: t % (SEQ // prep_tm)
    ctx_table = lambda t: SEQ // prep_tm

    w_head, w_tail = split_w_in(w_in)
    w_pa, w_pb, w_pc, w_out = (w.astype(BF) for w in (w_pa, w_pb, w_pc, w_out))
    dense_w_down = dense_w_down.astype(BF)

    for i in range(DEPTH):
        last = i == DEPTH - 1
        mod = mods[i]
        head_l, tail_l = in_proj(xl, norm1_g[i], mod, lat_row(1024), w_head, w_tail, i)
        head_c, tail_c = in_proj(xc, norm1_g[i], mod, ctx_row, w_head, w_tail, i)

        a_l = na_latent(head_l, head_c, na_bias_table(na_rpb[i]), n_batch)
        b_l, b_c = gla_bidirectional(head_l, head_c, tail_l, tail_c, gla_w_a2[i], gla_b_a[i],
                                     gla_norm_g[i], n_batch)
        q_l, k_l = gqa_prep(tail_l, cos_t, sin_t, gqa_qn_g[i], gqa_kn_g[i], lat_table, prep_tm)
        q_c, k_c = gqa_prep(tail_c, cos_t, sin_t, gqa_qn_g[i], gqa_kn_g[i], ctx_table, prep_tm)
        c_l = gqa_latent(q_l, k_l, k_c, tail_l, tail_c, n_batch)

        m_l = merge_branches(a_l, b_l, c_l, tail_l, w_pa, w_pb, w_pc, i)
        xl = matmul_residual(m_l, w_out, i, xl, mod, lat_row(1024), 2, tn=512)
        if not last:
            a_c = na_context(head_c, n_batch)
            c_c = gqa_context(q_c, k_c, tail_c, n_batch)
            m_c = merge_branches(a_c, b_c, c_c, tail_c, w_pa, w_pb, w_pc, i)
            xc = matmul_residual(m_c, w_out, i, xc, mod, ctx_row, 2, tn=512)

        j = i // 2
        if i % 2 == 0:
            def ffn(xs, mod_row_tm):
                u = ffn_up(xs, norm2_g[i], mod, mod_row_tm(1024), dense_w_gate[j], dense_w_up[j])
                return matmul_residual(u, dense_w_down, j, xs, mod, mod_row_tm(1024), 5)
            xl = ffn(xl, lat_row)
            if not last:
                xc = ffn(xc, lambda tm: ctx_row)
        else:
            fin = final_norm_g if last else None
            xl = moe_layer(xl, norm2_g[i], mod, lat_row, router_w[j], router_b[j],
                           moe_w_gate[j], moe_w_up[j], moe_w_down[j], fin)
            if not last:
                xc = moe_layer(xc, norm2_g[i], mod, lambda tm: ctx_row, router_w[j], router_b[j],
                               moe_w_gate[j], moe_w_up[j], moe_w_down[j], None)
    if (DEPTH - 1) % 2 == 0:
        xl = final_rmsnorm(xl, final_norm_g)
    return xl.reshape(n_batch, SEQ, D_MODEL)
```

```python
import functools

import jax
import jax.numpy as jnp
from jax import lax
from jax.experimental import pallas as pl
from jax.experimental.pallas import tpu as pltpu

BF = jnp.bfloat16
F32 = jnp.float32

D_MODEL = 2048
SEQ = 2048
CTX_LEN = 256
DEPTH = 2
GRID_W = 64
GRID_ROWS = SEQ // GRID_W
NA_HEADS = 16
NA_HEAD_DIM = 64
NA_WIN_ROWS = 8
NA_WIN_COLS = 16
GLA_HEADS = 4
GLA_DK = 128
GLA_DV = 256
GLA_GATE_RANK = 16
GLA_TAU = 16.0
GQA_HEADS = 8
GQA_KV_HEADS = 2
GQA_HEAD_DIM = 128
GQA_GROUP = GQA_HEADS // GQA_KV_HEADS
ROPE_THETA = 10000.0
D_FF = 5632
N_EXPERTS = 8
TOP_K = 2
NORM_EPS = 1e-6
BRANCH_W = 1024
LOG2_E = 1.4426950408889634

C_NA_Q, C_NA_K, C_NA_V = 0, 1024, 2048
C_GLA_Q, C_GLA_K, C_GLA_V, C_GLA_OG = 3072, 3584, 4096, 5120
D_HEAD = 6144
C_GQA_Q, C_GQA_K, C_GQA_V = 0, 1024, 1280
C_GATE_A, C_GATE_B, C_GATE_C = 1536, 3584, 5632
C_LOW = 7680
D_TAIL = 8192
R_LOW = 6144
R_GQA_Q = 6176
R_END = 13856

V7X_LANES = 128
V7X_VMEM_LIMIT_BYTES = 56 * 1024 * 1024

NA_ROWS_PER_STEP = 4
GLA_STEP = 128
GLA_SUB = 16
GLA_HEADS_PER_STEP = 2
MOE_ROWS = 1536
MOE_SUB = 512
MOE_UP_TILE = 256
MOE_DOWN_TILE = 256
MOE_GATHER_ROWS = 512
MOE_CAST_CHUNK = 512


def _cparams(*sem):
    return pltpu.CompilerParams(dimension_semantics=sem,
                                vmem_limit_bytes=V7X_VMEM_LIMIT_BYTES)


def _dot(a, b):
    return jnp.dot(a, b, preferred_element_type=F32)


def _dot_nt(a, b):
    return lax.dot_general(a, b, (((1,), (1,)), ((), ())), preferred_element_type=F32)


def _dot_cast(a_ref, rows, w_ref, chunk):
    acc = None
    for k0 in range(0, w_ref.shape[0], chunk):
        part = _dot(a_ref[rows, k0:k0 + chunk], w_ref[k0:k0 + chunk, :].astype(BF))
        acc = part if acc is None else acc + part
    return acc


def _silu(x):
    return x * jax.nn.sigmoid(x)


def _latent_mod_row(tm):
    return lambda t: (t * tm) // SEQ


def _ctx_mod_row(n_batch):
    return lambda t: n_batch


def _ada_kernel(c_ref, w_ref, b_ref, o_ref):
    a = _silu(c_ref[...])
    o_ref[...] = _dot(a.astype(BF), w_ref[...].astype(BF)) + b_ref[...]


def ada_modulation(cvec, w_ada, b_ada):
    tn = 1024
    n6 = 6 * D_MODEL
    out = pl.pallas_call(
        _ada_kernel,
        grid=(DEPTH, n6 // tn),
        in_specs=[
            pl.BlockSpec((8, D_MODEL), lambda l, j: (0, 0)),
            pl.BlockSpec((None, D_MODEL, tn), lambda l, j: (l, 0, j)),
            pl.BlockSpec((None, 1, tn), lambda l, j: (l, 0, j)),
        ],
        out_specs=pl.BlockSpec((None, 8, tn), lambda l, j: (l, 0, j)),
        out_shape=jax.ShapeDtypeStruct((DEPTH, 8, n6), F32),
        compiler_params=_cparams("parallel", "parallel"),
        name="ada_modulation",
    )(cvec, w_ada, b_ada.reshape(DEPTH, 1, n6))
    return out.reshape(DEPTH, 8, 6, D_MODEL)


def _modulated_norm(x, g, mod, shift_idx, scale_idx):
    ms = jnp.mean(x * x, axis=-1, keepdims=True)
    y = x * lax.rsqrt(ms + NORM_EPS) * g
    return y * (1.0 + mod[scale_idx:scale_idx + 1]) + mod[shift_idx:shift_idx + 1]


def _rmsnorm_kernel(x_ref, g_ref, o_ref):
    x = x_ref[...]
    ms = jnp.mean(x * x, axis=-1, keepdims=True)
    o_ref[...] = x * lax.rsqrt(ms + NORM_EPS) * g_ref[...]


def final_rmsnorm(x, g, tm=512):
    m = x.shape[0]
    return pl.pallas_call(
        _rmsnorm_kernel,
        grid=(m // tm,),
        in_specs=[pl.BlockSpec((tm, D_MODEL), lambda t: (t, 0)),
                  pl.BlockSpec((1, D_MODEL), lambda t: (0, 0))],
        out_specs=pl.BlockSpec((tm, D_MODEL), lambda t: (t, 0)),
        out_shape=jax.ShapeDtypeStruct((m, D_MODEL), F32),
        compiler_params=_cparams("parallel"),
        name="final_rmsnorm",
    )(x, g.reshape(1, D_MODEL))


def _in_proj_kernel(x_ref, g_ref, mod_ref, wh_ref, wt_ref, oh_ref, ot_ref, h_s, *, n_head):
    j = pl.program_id(1)

    @pl.when(j == 0)
    def _():
        h_s[...] = _modulated_norm(x_ref[...], g_ref[...], mod_ref[...], 0, 1).astype(h_s.dtype)

    @pl.when(j < n_head)
    def _():
        oh_ref[...] = _dot(h_s[...], wh_ref[...]).astype(oh_ref.dtype)

    @pl.when(j >= n_head)
    def _():
        ot_ref[...] = _dot(h_s[...], wt_ref[...]).astype(ot_ref.dtype)


def in_proj(x, g, mod, mod_row, w_packed, layer, tm=1024, tn=512):
    m = x.shape[0]
    tm = min(tm, m)
    n_head, n_tail = D_HEAD // tn, D_TAIL // tn
    head_col = lambda j: jnp.minimum(j, n_head - 1)
    tail_col = lambda j: jnp.maximum(j - n_head, 0)
    w_head = w_tail = w_packed
    return pl.pallas_call(
        functools.partial(_in_proj_kernel, n_head=n_head),
        grid=(m // tm, n_head + n_tail),
        in_specs=[pl.BlockSpec((tm, D_MODEL), lambda i, j: (i, 0)),
                  pl.BlockSpec((1, D_MODEL), lambda i, j: (0, 0)),
                  pl.BlockSpec((None, 6, D_MODEL), lambda i, j: (mod_row(i), 0, 0)),
                  pl.BlockSpec((None, D_MODEL, tn), lambda i, j: (layer, 0, head_col(j))),
                  pl.BlockSpec((None, D_MODEL, tn), lambda i, j: (layer, 0, n_head + tail_col(j)))],
        out_specs=[pl.BlockSpec((tm, tn), lambda i, j: (i, head_col(j))),
                   pl.BlockSpec((tm, tn), lambda i, j: (i, tail_col(j)))],
        out_shape=[jax.ShapeDtypeStruct((m, D_HEAD), BF), jax.ShapeDtypeStruct((m, D_TAIL), BF)],
        scratch_shapes=[pltpu.VMEM((tm, D_MODEL), BF)],
        compiler_params=_cparams("parallel", "arbitrary"),
        name="in_proj",
    )(x, g.reshape(1, D_MODEL), mod, w_head, w_tail)


def _mm_res_kernel(a_ref, w_ref, x_ref, mod_ref, o_ref, *, gate_idx):
    y = _dot(a_ref[...], w_ref[...].astype(BF))
    o_ref[...] = x_ref[...] + mod_ref[gate_idx:gate_idx + 1, :] * y


def matmul_residual(a, w, layer, x, mod, mod_row, gate_idx, tm=1024, tn=256):
    m, k = a.shape
    n = w.shape[2]
    tm = min(tm, m)
    return pl.pallas_call(
        functools.partial(_mm_res_kernel, gate_idx=gate_idx),
        grid=(m // tm, n // tn),
        in_specs=[pl.BlockSpec((tm, k), lambda i, j: (i, 0)),
                  pl.BlockSpec((None, k, tn), lambda i, j: (layer, 0, j)),
                  pl.BlockSpec((tm, tn), lambda i, j: (i, j)),
                  pl.BlockSpec((None, 6, tn), lambda i, j: (mod_row(i), 0, j))],
        out_specs=pl.BlockSpec((tm, tn), lambda i, j: (i, j)),
        out_shape=jax.ShapeDtypeStruct((m, n), F32),
        compiler_params=_cparams("parallel", "parallel"),
        name="matmul_residual",
    )(a, w, x, mod)


def _merge_kernel(a_ref, b_ref, c_ref, ga_ref, gb_ref, gc_ref, wa_ref, wb_ref, wc_ref, o_ref):
    def branch(x_ref, g_ref, w_ref):
        return jax.nn.sigmoid(g_ref[...].astype(F32)) * _dot(x_ref[...], w_ref[...].astype(BF))

    o_ref[...] = (branch(a_ref, ga_ref, wa_ref) + branch(b_ref, gb_ref, wb_ref)
                  + branch(c_ref, gc_ref, wc_ref)).astype(o_ref.dtype)


def merge_branches(a, b, c, proj, w_pa, w_pb, w_pc, layer, tm=1024, tn=512):
    m = a.shape[0]
    tm = min(tm, m)
    x_spec = pl.BlockSpec((tm, BRANCH_W), lambda i, j: (i, 0))
    w_spec = pl.BlockSpec((None, BRANCH_W, tn), lambda i, j: (layer, 0, j))

    def gate_spec(col):
        return pl.BlockSpec((tm, tn), lambda i, j: (i, col // tn + j))

    return pl.pallas_call(
        _merge_kernel,
        grid=(m // tm, D_MODEL // tn),
        in_specs=[x_spec, x_spec, x_spec,
                  gate_spec(C_GATE_A), gate_spec(C_GATE_B), gate_spec(C_GATE_C),
                  w_spec, w_spec, w_spec],
        out_specs=pl.BlockSpec((tm, tn), lambda i, j: (i, j)),
        out_shape=jax.ShapeDtypeStruct((m, D_MODEL), BF),
        compiler_params=_cparams("parallel", "parallel"),
        name="merge_branches",
    )(a, b, c, proj, proj, proj, w_pa, w_pb, w_pc)


def _ffn_up_kernel(x_ref, g_ref, mod_ref, wg_ref, wu_ref, o_ref, h_s):
    @pl.when(pl.program_id(1) == 0)
    def _():
        h_s[...] = _modulated_norm(x_ref[...], g_ref[...], mod_ref[...], 3, 4).astype(h_s.dtype)

    g = _dot_cast(h_s, slice(None), wg_ref, 512)
    u = _dot_cast(h_s, slice(None), wu_ref, 512)
    o_ref[...] = (_silu(g) * u).astype(o_ref.dtype)


def ffn_up(x, norm_g, mod, mod_row, w_gate, w_up, tm=1024, tn=512):
    m = x.shape[0]
    tm = min(tm, m)
    return pl.pallas_call(
        _ffn_up_kernel,
        grid=(m // tm, D_FF // tn),
        in_specs=[pl.BlockSpec((tm, D_MODEL), lambda i, j: (i, 0)),
                  pl.BlockSpec((1, D_MODEL), lambda i, j: (0, 0)),
                  pl.BlockSpec((None, 6, D_MODEL), lambda i, j: (mod_row(i), 0, 0)),
                  pl.BlockSpec((D_MODEL, tn), lambda i, j: (0, j)),
                  pl.BlockSpec((D_MODEL, tn), lambda i, j: (0, j))],
        out_specs=pl.BlockSpec((tm, tn), lambda i, j: (i, j)),
        out_shape=jax.ShapeDtypeStruct((m, D_FF), BF),
        scratch_shapes=[pltpu.VMEM((tm, D_MODEL), BF)],
        compiler_params=_cparams("parallel", "arbitrary"),
        name="ffn_up",
    )(x, norm_g.reshape(1, D_MODEL), mod, w_gate, w_up)


def _gqa_prep_kernel(q_ref, k_ref, cos_ref, sin_ref, qg_ref, kg_ref, qo_ref, ko_ref):
    cos = cos_ref[...]
    sin = sin_ref[...]
    lane = lax.broadcasted_iota(jnp.int32, cos.shape, 1)
    even = (lane & 1) == 0

    def prep(x_ref, g_ref, o_ref, heads):
        for h in range(heads):
            sl = slice(h * GQA_HEAD_DIM, (h + 1) * GQA_HEAD_DIM)
            x = x_ref[:, sl].astype(F32)
            ms = jnp.mean(x * x, axis=-1, keepdims=True)
            y = x * lax.rsqrt(ms + NORM_EPS) * g_ref[...]
            swapped = jnp.where(even, pltpu.roll(y, GQA_HEAD_DIM - 1, 1), pltpu.roll(y, 1, 1))
            o_ref[:, sl] = (y * cos + swapped * sin).astype(o_ref.dtype)

    prep(q_ref, qg_ref, qo_ref, GQA_HEADS)
    prep(k_ref, kg_ref, ko_ref, GQA_KV_HEADS)


def gqa_prep(proj, cos_t, sin_t, qn_g, kn_g, table_block, tm=256):
    m = proj.shape[0]
    qw = GQA_HEADS * GQA_HEAD_DIM
    kw = GQA_KV_HEADS * GQA_HEAD_DIM
    return pl.pallas_call(
        _gqa_prep_kernel,
        grid=(m // tm,),
        in_specs=[pl.BlockSpec((tm, qw), lambda t: (t, C_GQA_Q // qw)),
                  pl.BlockSpec((tm, kw), lambda t: (t, C_GQA_K // kw)),
                  pl.BlockSpec((tm, GQA_HEAD_DIM), lambda t: (table_block(t), 0)),
                  pl.BlockSpec((tm, GQA_HEAD_DIM), lambda t: (table_block(t), 0)),
                  pl.BlockSpec((1, GQA_HEAD_DIM), lambda t: (0, 0)),
                  pl.BlockSpec((1, GQA_HEAD_DIM), lambda t: (0, 0))],
        out_specs=[pl.BlockSpec((tm, qw), lambda t: (t, 0)),
                   pl.BlockSpec((tm, kw), lambda t: (t, 0))],
        out_shape=[jax.ShapeDtypeStruct((m, qw), BF), jax.ShapeDtypeStruct((m, kw), BF)],
        compiler_params=_cparams("parallel"),
        name="gqa_prep",
    )(proj, proj, cos_t, sin_t, qn_g.reshape(1, -1), kn_g.reshape(1, -1))


def rope_tables():
    half = GQA_HEAD_DIM // 2
    freqs = ROPE_THETA ** (-jnp.arange(0, half, 2, dtype=F32) / half)
    t = jnp.arange(SEQ)
    row = (t // GRID_W).astype(F32)
    col = (t % GRID_W).astype(F32)
    ang = jnp.concatenate([row[:, None] * freqs, col[:, None] * freqs], axis=-1)
    cos = jnp.repeat(jnp.cos(ang), 2, axis=-1)
    sin = jnp.repeat(jnp.sin(ang), 2, axis=-1)
    sign = jnp.tile(jnp.array([-1.0, 1.0], F32), half)
    cos = jnp.concatenate([cos, jnp.ones((256, GQA_HEAD_DIM), F32)], axis=0)
    sin = jnp.concatenate([sin * sign, jnp.zeros((256, GQA_HEAD_DIM), F32)], axis=0)
    return cos, sin


def _attend(q, kv_list, scale):
    scores = []
    for k, _, bias in kv_list:
        s = _dot_nt(q, k)
        if bias is not None:
            s = s + bias
        scores.append(s)

    def lane_chunks(xs):
        return [x[:, c:c + V7X_LANES] for x in xs for c in range(0, x.shape[1], V7X_LANES)]

    m = jnp.max(functools.reduce(jnp.maximum, lane_chunks(scores)), axis=-1, keepdims=True)
    ps = [jnp.exp2((s - m) * (scale * LOG2_E)) for s in scores]
    den = jnp.sum(functools.reduce(jnp.add, lane_chunks(ps)), axis=-1, keepdims=True)
    o = functools.reduce(jnp.add, [_dot(p.astype(BF), v) for p, (_, v, _) in zip(ps, kv_list)])
    return o / den


def _head_pair_rows(q2):
    lane = lax.broadcasted_iota(jnp.int32, q2.shape, 1)
    first = lane < NA_HEAD_DIM
    zero = jnp.zeros_like(q2)
    return jnp.concatenate([jnp.where(first, q2, zero), jnp.where(first, zero, q2)], axis=0)


def _head_pair_merge(o, m):
    lane = lax.broadcasted_iota(jnp.int32, (m, 2 * NA_HEAD_DIM), 1)
    return jnp.where(lane < NA_HEAD_DIM, o[:m], o[m:])


def _na_kernel(q_ref, kl_ref, vl_ref, kc_ref, vc_ref, *rest):
    bias_refs, o_ref = rest[:NA_ROWS_PER_STEP], rest[NA_ROWS_PER_STEP]
    n_loc = NA_WIN_ROWS * GRID_W
    scale = NA_HEAD_DIM ** -0.5
    for rr in range(NA_ROWS_PER_STEP):
        r = pl.program_id(1) * NA_ROWS_PER_STEP + rr
        start = jnp.clip(r - NA_WIN_ROWS // 2, 0, GRID_ROWS - NA_WIN_ROWS) * GRID_W
        start = pl.multiple_of(start, GRID_W)
        q_rows = slice(rr * GRID_W, (rr + 1) * GRID_W)
        for hp in range(NA_HEADS // 2):
            sl = slice(hp * 128, (hp + 1) * 128)
            qq = _head_pair_rows(q_ref[q_rows, sl])
            bias = jnp.concatenate([bias_refs[rr][2 * hp], bias_refs[rr][2 * hp + 1]], axis=0)
            o = _attend(qq, [(kl_ref[pl.ds(start, n_loc), sl], vl_ref[pl.ds(start, n_loc), sl], bias),
                             (kc_ref[:, sl], vc_ref[:, sl], None)], scale)
            o_ref[q_rows, sl] = _head_pair_merge(o, GRID_W).astype(o_ref.dtype)


def na_latent(proj_l, proj_c, bias_tab, n_batch):
    w = NA_HEADS * NA_HEAD_DIM
    rps = NA_ROWS_PER_STEP
    steps = GRID_ROWS // rps

    def pattern(r):
        return jnp.where(r < 4, r, jnp.where(r > GRID_ROWS - 4, r - (GRID_ROWS - NA_WIN_ROWS), 4))

    def bias_spec(rr):
        return pl.BlockSpec((None, NA_HEADS, GRID_W, NA_WIN_ROWS * GRID_W),
                            lambda b, i: (pattern(i * rps + rr), 0, 0, 0))

    return pl.pallas_call(
        _na_kernel,
        grid=(n_batch, steps),
        in_specs=[pl.BlockSpec((rps * GRID_W, w), lambda b, i: (b * steps + i, C_NA_Q // w)),
                  pl.BlockSpec((SEQ, w), lambda b, i: (b, C_NA_K // w)),
                  pl.BlockSpec((SEQ, w), lambda b, i: (b, C_NA_V // w)),
                  pl.BlockSpec((CTX_LEN, w), lambda b, i: (b, C_NA_K // w)),
                  pl.BlockSpec((CTX_LEN, w), lambda b, i: (b, C_NA_V // w))]
                 + [bias_spec(rr) for rr in range(rps)],
        out_specs=pl.BlockSpec((rps * GRID_W, w), lambda b, i: (b * steps + i, 0)),
        out_shape=jax.ShapeDtypeStruct((n_batch * SEQ, w), BF),
        compiler_params=_cparams("parallel", "arbitrary"),
        name="na_latent",
    )(proj_l, proj_l, proj_l, proj_c, proj_c, *([bias_tab] * rps))


def na_bias_table(rpb):
    cols = jnp.arange(GRID_W)
    col_start = jnp.clip(cols - NA_WIN_COLS // 2, 0, GRID_W - NA_WIN_COLS)
    in_win = (cols[None, :] >= col_start[:, None]) & (cols[None, :] < col_start[:, None] + NA_WIN_COLS)
    dc = jnp.clip(cols[None, :] - cols[:, None] + NA_WIN_COLS - 1, 0, 2 * NA_WIN_COLS - 2)
    onehot = (dc[None] == jnp.arange(2 * NA_WIN_COLS - 1)[:, None, None]).astype(F32)
    by_col = jnp.einsum('hdc,cqk->hdqk', rpb.astype(F32), onehot, precision=lax.Precision.HIGHEST)
    by_col = jnp.where(in_win[None, None], by_col * (NA_HEAD_DIM ** 0.5), -jnp.inf)
    last = NA_WIN_ROWS - 1
    bias = jnp.stack([by_col[:, last - p:last - p + NA_WIN_ROWS] for p in range(NA_WIN_ROWS)])
    bias = bias.transpose(0, 1, 3, 2, 4)
    return bias.reshape(NA_WIN_ROWS, NA_HEADS, GRID_W, NA_WIN_ROWS * GRID_W)


def _na_ctx_kernel(q_ref, k_ref, v_ref, o_ref):
    scale = NA_HEAD_DIM ** -0.5
    for hp in range(NA_HEADS // 2):
        sl = slice(hp * 128, (hp + 1) * 128)
        o = _attend(_head_pair_rows(q_ref[:, sl]), [(k_ref[:, sl], v_ref[:, sl], None)], scale)
        o_ref[:, sl] = _head_pair_merge(o, CTX_LEN).astype(o_ref.dtype)


def na_context(proj_c, n_batch):
    w = NA_HEADS * NA_HEAD_DIM
    return pl.pallas_call(
        _na_ctx_kernel,
        grid=(n_batch,),
        in_specs=[pl.BlockSpec((CTX_LEN, w), lambda b: (b, C_NA_Q // w)),
                  pl.BlockSpec((CTX_LEN, w), lambda b: (b, C_NA_K // w)),
                  pl.BlockSpec((CTX_LEN, w), lambda b: (b, C_NA_V // w))],
        out_specs=pl.BlockSpec((CTX_LEN, w), lambda b: (b, 0)),
        out_shape=jax.ShapeDtypeStruct((n_batch * CTX_LEN, w), BF),
        compiler_params=_cparams("parallel"),
        name="na_context",
    )(proj_c, proj_c, proj_c)


def _gqa_rows(q_ref):
    return jnp.concatenate([q_ref[:, g * GQA_HEAD_DIM:(g + 1) * GQA_HEAD_DIM]
                            for g in range(GQA_GROUP)], axis=0)


def _gqa_store(o, o_ref):
    tq = o_ref.shape[0]
    for g in range(GQA_GROUP):
        o_ref[:, g * GQA_HEAD_DIM:(g + 1) * GQA_HEAD_DIM] = o[g * tq:(g + 1) * tq].astype(o_ref.dtype)


def _gqa_kernel(q_ref, kl_ref, vl_ref, kc_ref, vc_ref, o_ref):
    for g in range(GQA_GROUP):
        sl = slice(g * GQA_HEAD_DIM, (g + 1) * GQA_HEAD_DIM)
        o = _attend(q_ref[:, sl], [(kl_ref[...], vl_ref[...], None), (kc_ref[...], vc_ref[...], None)],
                    GQA_HEAD_DIM ** -0.5)
        o_ref[:, sl] = o.astype(o_ref.dtype)


def _gqa_ctx_kernel(q_ref, kc_ref, vc_ref, o_ref):
    o = _attend(_gqa_rows(q_ref), [(kc_ref[...], vc_ref[...], None)], GQA_HEAD_DIM ** -0.5)
    _gqa_store(o, o_ref)


def gqa_latent(q_l, k_l, k_c, proj_l, proj_c, n_batch, tq=512):
    gw = GQA_GROUP * GQA_HEAD_DIM
    nq = SEQ // tq
    dh = GQA_HEAD_DIM
    return pl.pallas_call(
        _gqa_kernel,
        grid=(n_batch, GQA_KV_HEADS, nq),
        in_specs=[pl.BlockSpec((tq, gw), lambda b, h, i: (b * nq + i, h)),
                  pl.BlockSpec((SEQ, dh), lambda b, h, i: (b, h)),
                  pl.BlockSpec((SEQ, dh), lambda b, h, i: (b, C_GQA_V // dh + h)),
                  pl.BlockSpec((CTX_LEN, dh), lambda b, h, i: (b, h)),
                  pl.BlockSpec((CTX_LEN, dh), lambda b, h, i: (b, C_GQA_V // dh + h))],
        out_specs=pl.BlockSpec((tq, gw), lambda b, h, i: (b * nq + i, h)),
        out_shape=jax.ShapeDtypeStruct((n_batch * SEQ, GQA_HEADS * dh), BF),
        compiler_params=_cparams("parallel", "parallel", "arbitrary"),
        name="gqa_latent",
    )(q_l, k_l, proj_l, k_c, proj_c)


def gqa_context(q_c, k_c, proj_c, n_batch):
    gw = GQA_GROUP * GQA_HEAD_DIM
    dh = GQA_HEAD_DIM
    return pl.pallas_call(
        _gqa_ctx_kernel,
        grid=(n_batch, GQA_KV_HEADS),
        in_specs=[pl.BlockSpec((CTX_LEN, gw), lambda b, h: (b, h)),
                  pl.BlockSpec((CTX_LEN, dh), lambda b, h: (b, h)),
                  pl.BlockSpec((CTX_LEN, dh), lambda b, h: (b, C_GQA_V // dh + h))],
        out_specs=pl.BlockSpec((CTX_LEN, gw), lambda b, h: (b, h)),
        out_shape=jax.ShapeDtypeStruct((n_batch * CTX_LEN, GQA_HEADS * dh), BF),
        compiler_params=_cparams("parallel", "parallel"),
        name="gqa_context",
    )(q_c, k_c, proj_c)


def _log_sigmoid(z):
    return jnp.minimum(z, 0.0) - jnp.log1p(jnp.exp(-jnp.abs(z)))


def _gla_decay(c, direction, head, low_s, wa_ref, ba_ref):
    n = GLA_STEP
    rows = pl.ds(pl.multiple_of(c * n, n), n)
    z = _dot(low_s[rows, :], wa_ref[direction, head].astype(BF)) + ba_ref[direction, head]
    g = _log_sigmoid(z) * (1.0 / GLA_TAU)
    ri = lax.broadcasted_iota(jnp.int32, (n, n), 0)
    ci = lax.broadcasted_iota(jnp.int32, (n, n), 1)
    tri = (ci <= ri) if direction == 0 else (ci >= ri)
    tri = jnp.where(tri, 1.0, 0.0).astype(BF)
    g_hi = g.astype(BF)
    g_lo = (g - g_hi.astype(F32)).astype(BF)
    return (_dot(tri, g_hi) + _dot(tri, g_lo)) * LOG2_E


def _gla_step(c, direction, head, bc, q_s, k_s, v_s, st_ref, o_s):
    n = GLA_STEP
    r0 = pl.multiple_of(c * n, n)
    rows = pl.ds(r0, n)
    k_lanes = slice(head * GLA_DK, (head + 1) * GLA_DK)
    v_lanes = slice(head * GLA_DV, (head + 1) * GLA_DV)
    qf = q_s[rows, k_lanes].astype(F32) * (GLA_DK ** -0.5)
    k_bf = k_s[rows, k_lanes]
    kf = k_bf.astype(F32)
    v = v_s[rows, v_lanes]
    btot = bc[n - 1:n] if direction == 0 else bc[0:1]

    state = st_ref[...]
    q_in = (qf * jnp.exp2(bc)).astype(BF)
    o = _dot_nt(q_in, state.astype(BF))
    k_out = (kf * jnp.exp2(btot - bc)).astype(BF)
    v_t = v.astype(F32).T.astype(BF)
    st_ref[...] = jnp.exp2(btot) * state + _dot(v_t, k_out)

    across = None
    size = n // 2
    while size >= GLA_SUB:
        pieces = []
        for base in range(0, n, 2 * size):
            mid = base + size
            if direction == 0:
                early, late, ref = slice(base, mid), slice(mid, mid + size), bc[mid - 1:mid]
            else:
                early, late, ref = slice(mid, mid + size), slice(base, mid), bc[mid:mid + 1]
            k_dec = (kf[early] * jnp.exp2(ref - bc[early])).astype(BF)
            k_rows = [jnp.zeros((early.start, GLA_DK), BF)] if early.start else []
            k_rows.append(k_dec)
            if early.stop < n:
                k_rows.append(jnp.zeros((n - early.stop, GLA_DK), BF))
            part = _dot_nt((qf[late] * jnp.exp2(bc[late] - ref)).astype(BF),
                           jnp.concatenate(k_rows, axis=0))
            none = jnp.zeros((size, n), F32)
            pieces += [none, part] if direction == 0 else [part, none]
        level = jnp.concatenate(pieces, axis=0)
        across = level if across is None else across + level
        size //= 2

    lane = lax.broadcasted_iota(jnp.int32, (GLA_SUB, n), 1)
    sub_row = lax.broadcasted_iota(jnp.int32, (GLA_SUB, 1), 0)
    neg_inf = jnp.float32(-jnp.inf)
    blocks = []
    for blk in range(n // GLA_SUB):
        lo, hi = blk * GLA_SUB, (blk + 1) * GLA_SUB
        q_b, b_b, a_b = qf[lo:hi], bc[lo:hi], across[lo:hi]
        decayed = []
        for jl in range(GLA_SUB):
            keep = (sub_row >= jl) if direction == 0 else (sub_row <= jl)
            dec = jnp.exp2(jnp.where(keep, b_b - bc[lo + jl:lo + jl + 1], neg_inf))
            decayed.append((q_b * dec).astype(BF))
        pair = _dot_nt(jnp.concatenate(decayed, axis=0), k_bf)
        for jl in range(GLA_SUB):
            a_b = jnp.where(lane == lo + jl, pair[jl * GLA_SUB:(jl + 1) * GLA_SUB], a_b)
        blocks.append(a_b)
    attn = jnp.concatenate(blocks, axis=0).astype(BF)
    o_s[rows, :] = o + _dot(attn, v)


def _gla_kernel(ql, kl, vl, ogl, lowl, qc, kc, vc, ogc, lowc, wa_ref, ba_ref, ng_ref,
                yl_ref, yc_ref, q_s, k_s, v_s, low_s, *per_head):
    hps = GLA_HEADS_PER_STEP
    of_s, ob_s, stf, stb, bcf, bcb = (per_head[g * hps:(g + 1) * hps] for g in range(6))
    nc = CTX_LEN
    q_s[0:nc, :] = qc[...]
    q_s[nc:, :] = ql[...]
    k_s[0:nc, :] = kc[...]
    k_s[nc:, :] = kl[...]
    v_s[0:nc, :] = vc[...]
    v_s[nc:, :] = vl[...]
    low_s[0:nc, :] = lowc[...]
    low_s[nc:, :] = lowl[...]
    for st in (*stf, *stb):
        st[...] = jnp.zeros_like(st)
    n_ctx = CTX_LEN // GLA_STEP
    n_all = (CTX_LEN + SEQ) // GLA_STEP

    def backward_step(i):
        return jnp.where(i < n_ctx, n_ctx - 1 - i, n_all + n_ctx - 1 - i)

    def store_decays(i):
        for head in range(hps):
            bcf[head][...] = _gla_decay(i, 0, head, low_s, wa_ref, ba_ref)
            bcb[head][...] = _gla_decay(backward_step(i), 1, head, low_s, wa_ref, ba_ref)

    store_decays(0)

    def body(i, carry):
        for head in range(hps):
            _gla_step(i, 0, head, bcf[head][...], q_s, k_s, v_s, stf[head], of_s[head])
            _gla_step(backward_step(i), 1, head, bcb[head][...], q_s, k_s, v_s, stb[head], ob_s[head])
        store_decays(jnp.minimum(i + 1, n_all - 1))
        return carry

    lax.fori_loop(0, n_all, body, 0)

    def finish(lo, hi, og_ref, y_ref):
        for head in range(GLA_HEADS_PER_STEP):
            lanes = slice(head * GLA_DV, (head + 1) * GLA_DV)
            o = of_s[head][lo:hi, :] + ob_s[head][lo:hi, :]
            ms = jnp.mean(o * o, axis=-1, keepdims=True)
            y = o * lax.rsqrt(ms + NORM_EPS) * ng_ref[...]
            y_ref[:, lanes] = (y * _silu(og_ref[:, lanes].astype(F32))).astype(y_ref.dtype)

    finish(0, nc, ogc, yc_ref)
    finish(nc, nc + SEQ, ogl, yl_ref)


def gla_bidirectional(head_l, head_c, tail_l, tail_c, w_a2, b_a, norm_g, n_batch):
    hps = GLA_HEADS_PER_STEP
    dk, dv = GLA_DK, GLA_DV
    kw, vw = hps * dk, hps * dv
    wa = jnp.zeros((2, GLA_HEADS, V7X_LANES, dk), F32)
    for d in range(2):
        wa = wa.at[d, :, d * GLA_GATE_RANK:(d + 1) * GLA_GATE_RANK, :].set(
            w_a2[d].reshape(GLA_GATE_RANK, GLA_HEADS, dk).transpose(1, 0, 2))
    ba = b_a.reshape(2, GLA_HEADS, 1, dk)

    def specs(rows):
        return [pl.BlockSpec((rows, kw), lambda b, h: (b, C_GLA_Q // kw + h)),
                pl.BlockSpec((rows, kw), lambda b, h: (b, C_GLA_K // kw + h)),
                pl.BlockSpec((rows, vw), lambda b, h: (b, C_GLA_V // vw + h)),
                pl.BlockSpec((rows, vw), lambda b, h: (b, C_GLA_OG // vw + h)),
                pl.BlockSpec((rows, V7X_LANES), lambda b, h: (b, C_LOW // V7X_LANES))]

    n_rows = CTX_LEN + SEQ
    return pl.pallas_call(
        _gla_kernel,
        grid=(n_batch, GLA_HEADS // hps),
        in_specs=specs(SEQ) + specs(CTX_LEN) + [
            pl.BlockSpec((2, hps, V7X_LANES, dk), lambda b, h: (0, h, 0, 0)),
            pl.BlockSpec((2, hps, 1, dk), lambda b, h: (0, h, 0, 0)),
            pl.BlockSpec((1, dv), lambda b, h: (0, 0))],
        out_specs=[pl.BlockSpec((SEQ, vw), lambda b, h: (b, h)),
                   pl.BlockSpec((CTX_LEN, vw), lambda b, h: (b, h))],
        out_shape=[jax.ShapeDtypeStruct((n_batch * SEQ, GLA_HEADS * dv), BF),
                   jax.ShapeDtypeStruct((n_batch * CTX_LEN, GLA_HEADS * dv), BF)],
        scratch_shapes=[pltpu.VMEM((n_rows, kw), BF), pltpu.VMEM((n_rows, kw), BF),
                        pltpu.VMEM((n_rows, vw), BF), pltpu.VMEM((n_rows, V7X_LANES), BF)]
                       + [pltpu.VMEM((n_rows, dv), F32)] * (2 * hps)
                       + [pltpu.VMEM((dv, dk), F32)] * (2 * hps)
                       + [pltpu.VMEM((GLA_STEP, dk), F32)] * (2 * hps),
        compiler_params=_cparams("parallel", "parallel"),
        name="gla_bidirectional",
    )(*([head_l] * 4 + [tail_l] + [head_c] * 4 + [tail_c]), wa, ba, norm_g.reshape(1, dv))


def _router_kernel(x_ref, g_ref, mod_ref, rw_ref, rb_ref, h_ref, idx_ref, wt_ref):
    h = _modulated_norm(x_ref[...], g_ref[...], mod_ref[...], 3, 4)
    h_ref[...] = h.astype(h_ref.dtype)
    logits = jnp.dot(h, rw_ref[...], preferred_element_type=F32,
                     precision=lax.Precision.HIGHEST) + rb_ref[...]
    lane = lax.broadcasted_iota(jnp.int32, logits.shape, 1)
    neg_inf = jnp.float32(-jnp.inf)
    logits = jnp.where(lane < N_EXPERTS, logits, neg_inf)
    m1 = jnp.max(logits, axis=-1, keepdims=True)
    lane_f = lane.astype(F32)
    i1 = jnp.min(jnp.where(logits == m1, lane_f, float(V7X_LANES)), axis=-1, keepdims=True)
    rest = jnp.where(lane_f == i1, neg_inf, logits)
    m2 = jnp.max(rest, axis=-1, keepdims=True)
    i2 = jnp.min(jnp.where(rest == m2, lane_f, float(V7X_LANES)), axis=-1, keepdims=True)
    e = jnp.exp(m2 - m1)
    w1 = 1.0 / (1.0 + e)
    idx_ref[...] = jnp.where(lane == 0, i1, jnp.where(lane == 1, i2, 0.0)).astype(jnp.int32)
    wt_ref[...] = jnp.where(lane == 0, w1, jnp.where(lane == 1, e * w1, 0.0))


def moe_router(x, g, mod, mod_row, router_w, router_b, tm=512):
    m = x.shape[0]
    rw = jnp.zeros((D_MODEL, V7X_LANES), F32).at[:, :N_EXPERTS].set(router_w)
    rb = jnp.zeros((1, V7X_LANES), F32).at[0, :N_EXPERTS].set(router_b)
    return pl.pallas_call(
        _router_kernel,
        grid=(m // tm,),
        in_specs=[pl.BlockSpec((tm, D_MODEL), lambda t: (t, 0)),
                  pl.BlockSpec((1, D_MODEL), lambda t: (0, 0)),
                  pl.BlockSpec((None, 6, D_MODEL), lambda t: (mod_row(t), 0, 0)),
                  pl.BlockSpec((D_MODEL, V7X_LANES), lambda t: (0, 0)),
                  pl.BlockSpec((1, V7X_LANES), lambda t: (0, 0))],
        out_specs=[pl.BlockSpec((tm, D_MODEL), lambda t: (t, 0)),
                   pl.BlockSpec((tm, V7X_LANES), lambda t: (t, 0)),
                   pl.BlockSpec((tm, V7X_LANES), lambda t: (t, 0))],
        out_shape=[jax.ShapeDtypeStruct((m, D_MODEL), F32),
                   jax.ShapeDtypeStruct((m, V7X_LANES), jnp.int32),
                   jax.ShapeDtypeStruct((m, V7X_LANES), F32)],
        compiler_params=_cparams("parallel"),
        name="moe_router",
    )(x, g.reshape(1, D_MODEL), mod, rw, rb)


def _row_copy(src_hbm, src_row, dst, dst_row, sem):
    return pltpu.make_async_copy(src_hbm.at[pl.ds(src_row, 1)], dst.at[pl.ds(dst_row, 1)], sem)


def _gather_tile(src_hbm, idx_ref, n_rows, per_row, buf, sem, action):
    def body(i, carry):
        for k in range(per_row):
            cp = _row_copy(src_hbm, idx_ref[0, per_row * i + k], buf.at[k], i, sem)
            cp.start() if action == "start" else cp.wait()
        return carry

    lax.fori_loop(0, n_rows, body, 0)


def _dispatch_kernel(nv_ref, idx_ref, idx_next_ref, h_hbm, o_ref, buf, sem):
    t = pl.program_id(0)
    rows = o_ref.shape[0]
    slot = t % 2

    def start_tile(tile_idx_ref, tile, s):
        nv = nv_ref[tile]

        @pl.when(nv < rows)
        def _():
            buf[s] = jnp.zeros(buf.shape[1:], buf.dtype)

        _gather_tile(h_hbm, tile_idx_ref, nv, 1, buf.at[s], sem.at[s], "start")

    @pl.when(t == 0)
    def _():
        start_tile(idx_ref, 0, 0)

    @pl.when(t + 1 < pl.num_programs(0))
    def _():
        start_tile(idx_next_ref, t + 1, 1 - slot)

    _gather_tile(h_hbm, idx_ref, nv_ref[t], 1, buf.at[slot], sem.at[slot], "wait")
    o_ref[...] = buf[slot, 0].astype(o_ref.dtype)


def moe_dispatch(h, slot_t, tile_valid):
    n_slots = slot_t.shape[0]
    rows = MOE_GATHER_ROWS
    n_tiles = n_slots // rows
    idx = slot_t.reshape(n_tiles, 1, rows)
    grid_spec = pltpu.PrefetchScalarGridSpec(
        num_scalar_prefetch=1,
        grid=(n_tiles,),
        in_specs=[pl.BlockSpec((None, 1, rows), lambda t, nv: (t, 0, 0), memory_space=pltpu.SMEM),
                  pl.BlockSpec((None, 1, rows), lambda t, nv: (jnp.minimum(t + 1, n_tiles - 1), 0, 0),
                               memory_space=pltpu.SMEM),
                  pl.BlockSpec(memory_space=pl.ANY)],
        out_specs=pl.BlockSpec((rows, D_MODEL), lambda t, nv: (t, 0)),
        scratch_shapes=[pltpu.VMEM((2, 1, rows, D_MODEL), F32), pltpu.SemaphoreType.DMA((2,))],
    )
    return pl.pallas_call(
        _dispatch_kernel,
        grid_spec=grid_spec,
        out_shape=jax.ShapeDtypeStruct((n_slots, D_MODEL), BF),
        compiler_params=_cparams("arbitrary"),
        name="moe_dispatch",
    )(tile_valid, idx, idx, h)


def _moe_ffn_kernel(be_ref, nv_ref, nu_ref, x_ref, wg_ref, wu_ref, wd_ref, o_ref, act):
    s = pl.program_id(1)
    nv = nv_ref[pl.program_id(0)]
    n_up = D_FF // MOE_UP_TILE
    n_sub = (nv + MOE_SUB - 1) // MOE_SUB

    @pl.when(s < n_up)
    def _():
        cols = pl.ds(pl.multiple_of(s * MOE_UP_TILE, MOE_UP_TILE), MOE_UP_TILE)

        def up(n_rows):
            rows = slice(0, n_rows)
            g = _dot_cast(x_ref, rows, wg_ref, MOE_CAST_CHUNK)
            u = _dot_cast(x_ref, rows, wu_ref, MOE_CAST_CHUNK)
            act[rows, cols] = (_silu(g) * u).astype(BF)

        for parts in range(1, MOE_ROWS // MOE_SUB + 1):
            pl.when(n_sub == parts)(functools.partial(up, parts * MOE_SUB))

    @pl.when(s >= n_up)
    def _():
        def down(n_rows):
            if n_rows:
                o_ref[0:n_rows, :] = _dot_cast(act, slice(0, n_rows), wd_ref, MOE_CAST_CHUNK)
            if n_rows < MOE_ROWS:
                o_ref[n_rows:, :] = jnp.zeros((MOE_ROWS - n_rows, MOE_DOWN_TILE), F32)

        for parts in range(0, MOE_ROWS // MOE_SUB + 1):
            pl.when(n_sub == parts)(functools.partial(down, parts * MOE_SUB))


def moe_expert_ffn(xb, block_e, n_valid, n_used, w_gate, w_up, w_down):
    n_blk = xb.shape[0] // MOE_ROWS
    n_up = D_FF // MOE_UP_TILE
    n_down = D_MODEL // MOE_DOWN_TILE

    def last_used(b, nu):
        return jnp.minimum(b, nu[0] - 1)

    def up_idx(b, s, nu):
        return jnp.where(b < nu[0], jnp.minimum(s, n_up - 1), n_up - 1)

    def down_idx(b, s, nu):
        return jnp.where(b < nu[0], jnp.maximum(s - n_up, 0), n_down - 1)

    grid_spec = pltpu.PrefetchScalarGridSpec(
        num_scalar_prefetch=3,
        grid=(n_blk, n_up + n_down),
        in_specs=[
            pl.BlockSpec((MOE_ROWS, D_MODEL), lambda b, s, be, nv, nu: (last_used(b, nu), 0),
                         pipeline_mode=pl.Buffered(1)),
            pl.BlockSpec((None, D_MODEL, MOE_UP_TILE),
                         lambda b, s, be, nv, nu: (be[last_used(b, nu)], 0, up_idx(b, s, nu))),
            pl.BlockSpec((None, D_MODEL, MOE_UP_TILE),
                         lambda b, s, be, nv, nu: (be[last_used(b, nu)], 0, up_idx(b, s, nu))),
            pl.BlockSpec((None, D_FF, MOE_DOWN_TILE),
                         lambda b, s, be, nv, nu: (be[last_used(b, nu)], 0, down_idx(b, s, nu))),
        ],
        out_specs=pl.BlockSpec((MOE_ROWS, MOE_DOWN_TILE),
                               lambda b, s, be, nv, nu: (b, jnp.maximum(s - n_up, 0))),
        scratch_shapes=[pltpu.VMEM((MOE_ROWS, D_FF), BF)],
    )
    return pl.pallas_call(
        _moe_ffn_kernel,
        grid_spec=grid_spec,
        out_shape=jax.ShapeDtypeStruct((n_blk * MOE_ROWS, D_MODEL), F32),
        compiler_params=_cparams("arbitrary", "arbitrary"),
        name="moe_expert_ffn",
    )(block_e, n_valid, n_used, xb, w_gate, w_up, w_down)


def _moe_combine_kernel(idx_ref, idx_next_ref, x_ref, wt_ref, mod_ref, g_ref, yb_hbm, o_ref, buf, sem,
                        *, final_norm):
    t = pl.program_id(0)
    tm = x_ref.shape[0]
    slot = t % 2

    @pl.when(t == 0)
    def _():
        _gather_tile(yb_hbm, idx_ref, tm, TOP_K, buf.at[0], sem.at[0], "start")

    @pl.when(t + 1 < pl.num_programs(0))
    def _():
        _gather_tile(yb_hbm, idx_next_ref, tm, TOP_K, buf.at[1 - slot], sem.at[1 - slot], "start")

    _gather_tile(yb_hbm, idx_ref, tm, TOP_K, buf.at[slot], sem.at[slot], "wait")
    w = wt_ref[...]
    y = w[:, 0:1] * buf[slot, 0] + w[:, 1:2] * buf[slot, 1]
    out = x_ref[...] + mod_ref[5:6, :] * y
    if final_norm:
        ms = jnp.mean(out * out, axis=-1, keepdims=True)
        out = out * lax.rsqrt(ms + NORM_EPS) * g_ref[...]
    o_ref[...] = out


def moe_combine(x, yb, dest, wt, mod, mod_row, final_g, tm=256):
    m = x.shape[0]
    tm = min(tm, m)
    n_tiles = m // tm
    row = pl.BlockSpec((tm, D_MODEL), lambda t: (t, 0))
    final_norm = final_g is not None
    g = final_g if final_norm else jnp.ones((D_MODEL,), F32)
    idx = dest.reshape(n_tiles, 1, TOP_K * tm)
    return pl.pallas_call(
        functools.partial(_moe_combine_kernel, final_norm=final_norm),
        grid=(n_tiles,),
        in_specs=[pl.BlockSpec((None, 1, TOP_K * tm), lambda t: (t, 0, 0), memory_space=pltpu.SMEM),
                  pl.BlockSpec((None, 1, TOP_K * tm), lambda t: (jnp.minimum(t + 1, n_tiles - 1), 0, 0),
                               memory_space=pltpu.SMEM),
                  row,
                  pl.BlockSpec((tm, V7X_LANES), lambda t: (t, 0)),
                  pl.BlockSpec((None, 6, D_MODEL), lambda t: (mod_row(t), 0, 0)),
                  pl.BlockSpec((1, D_MODEL), lambda t: (0, 0)),
                  pl.BlockSpec(memory_space=pl.ANY)],
        out_specs=row,
        out_shape=jax.ShapeDtypeStruct((m, D_MODEL), F32),
        scratch_shapes=[pltpu.VMEM((2, TOP_K, tm, D_MODEL), F32), pltpu.SemaphoreType.DMA((2,))],
        compiler_params=_cparams("arbitrary"),
        name="moe_combine",
    )(idx, idx, x, wt, mod, g.reshape(1, D_MODEL), yb)


def moe_layout(top_i):
    t = top_i.shape[0]
    n_assign = t * TOP_K
    flat_e = top_i.reshape(-1)
    onehot = (flat_e[:, None] == jnp.arange(N_EXPERTS, dtype=jnp.int32)[None, :]).astype(jnp.int32)
    csum = jnp.cumsum(onehot, axis=0)
    rank = jnp.take_along_axis(csum, flat_e[:, None], axis=1)[:, 0] - 1
    counts = csum[-1]
    padded = (counts + MOE_ROWS - 1) // MOE_ROWS * MOE_ROWS
    pends = jnp.cumsum(padded)
    pstarts = pends - padded
    dest = pstarts[flat_e] + rank
    n_blk = n_assign // MOE_ROWS + N_EXPERTS
    flat_t = jnp.repeat(jnp.arange(t, dtype=jnp.int32), TOP_K)
    slot_t = jnp.zeros((n_blk * MOE_ROWS,), jnp.int32).at[dest].set(flat_t)
    blk = jnp.arange(n_blk, dtype=jnp.int32)
    blk_start = blk * MOE_ROWS
    block_e = jnp.minimum(jnp.sum((pends[None, :] <= blk_start[:, None]).astype(jnp.int32), axis=1),
                          N_EXPERTS - 1)
    n_used = pends[-1] // MOE_ROWS
    n_valid = jnp.clip(counts[block_e] - (blk_start - pstarts[block_e]), 0, MOE_ROWS)
    n_valid = jnp.where(blk < n_used, n_valid, 0)
    tiles_per_blk = MOE_ROWS // MOE_GATHER_ROWS
    tile_off = jnp.arange(tiles_per_blk, dtype=jnp.int32) * MOE_GATHER_ROWS
    tile_valid = jnp.clip(n_valid[:, None] - tile_off[None, :], 0, MOE_GATHER_ROWS).reshape(-1)
    return (slot_t, dest.reshape(t, TOP_K).astype(jnp.int32), block_e.astype(jnp.int32),
            n_valid.astype(jnp.int32), n_used.astype(jnp.int32).reshape(1), tile_valid.astype(jnp.int32))


def moe_layer(x, norm_g, mod, mod_row_tm, router_w, router_b, w_gate, w_up, w_down, final_g):
    router_tm, combine_tm = 512, 512
    h, idx, wt = moe_router(x, norm_g, mod, mod_row_tm(router_tm), router_w, router_b, router_tm)
    slot_t, dest, block_e, n_valid, n_used, tile_valid = moe_layout(idx[:, :TOP_K])
    xb = moe_dispatch(h, slot_t, tile_valid)
    yb = moe_expert_ffn(xb, block_e, n_valid, n_used, w_gate, w_up, w_down)
    return moe_combine(x, yb, dest, wt, mod, mod_row_tm(combine_tm), final_g, combine_tm)


def pack_w_in(w_in):
    width = D_HEAD + D_TAIL
    n_low = R_GQA_Q - R_LOW

    def placed(lo, hi, at):
        return jnp.pad(w_in[:, :, lo:hi], ((0, 0), (0, 0), (at, width - at - (hi - lo))))

    packed = placed(0, D_HEAD, 0) + placed(R_GQA_Q, R_END, D_HEAD) + placed(R_LOW, R_GQA_Q, D_HEAD + C_LOW)
    assert D_HEAD + C_LOW + n_low <= width
    return packed.astype(BF)


def kernel(x, c, ctx, c_ctx, w_ada, b_ada, norm1_g, norm2_g, w_in, na_rpb, gla_w_a2, gla_b_a, gla_norm_g, gqa_qn_g, gqa_kn_g, w_pa, w_pb, w_pc, w_out, dense_w_gate, dense_w_up, dense_w_down, router_w, router_b, moe_w_gate, moe_w_up, moe_w_down, final_norm_g):
    n_batch = x.shape[0]
    xl = x.reshape(n_batch * SEQ, D_MODEL)
    xc = ctx.reshape(n_batch * CTX_LEN, D_MODEL)
    cvec = jnp.zeros((8, D_MODEL), F32).at[:n_batch].set(c).at[n_batch].set(c_ctx)
    mods = ada_modulation(cvec, w_ada, b_ada)
    cos_t, sin_t = rope_tables()
    lat_row = _latent_mod_row
    ctx_row = _ctx_mod_row(n_batch)
    prep_tm = 256
    lat_table = lambda t: t % (SEQ // prep_tm)
    ctx_table = lambda t: SEQ // prep_tm

    w_packed = pack_w_in(w_in)
    w_pa, w_pb, w_pc, w_out = (w.astype(BF) for w in (w_pa, w_pb, w_pc, w_out))
    dense_w_down = dense_w_down.astype(BF)

    for i in range(DEPTH):
        last = i == DEPTH - 1
        mod = mods[i]
        head_l, tail_l = in_proj(xl, norm1_g[i], mod, lat_row(1024), w_packed, i)
        head_c, tail_c = in_proj(xc, norm1_g[i], mod, ctx_row, w_packed, i)

        a_l = na_latent(head_l, head_c, na_bias_table(na_rpb[i]), n_batch)
        b_l, b_c = gla_bidirectional(head_l, head_c, tail_l, tail_c, gla_w_a2[i], gla_b_a[i],
                                     gla_norm_g[i], n_batch)
        q_l, k_l = gqa_prep(tail_l, cos_t, sin_t, gqa_qn_g[i], gqa_kn_g[i], lat_table, prep_tm)
        q_c, k_c = gqa_prep(tail_c, cos_t, sin_t, gqa_qn_g[i], gqa_kn_g[i], ctx_table, prep_tm)
        c_l = gqa_latent(q_l, k_l, k_c, tail_l, tail_c, n_batch)

        m_l = merge_branches(a_l, b_l, c_l, tail_l, w_pa, w_pb, w_pc, i)
        xl = matmul_residual(m_l, w_out, i, xl, mod, lat_row(1024), 2, tn=512)
        if not last:
            a_c = na_context(head_c, n_batch)
            c_c = gqa_context(q_c, k_c, tail_c, n_batch)
            m_c = merge_branches(a_c, b_c, c_c, tail_c, w_pa, w_pb, w_pc, i)
            xc = matmul_residual(m_c, w_out, i, xc, mod, ctx_row, 2, tn=512)

        j = i // 2
        if i % 2 == 0:
            def ffn(xs, mod_row_tm):
                u = ffn_up(xs, norm2_g[i], mod, mod_row_tm(1024), dense_w_gate[j], dense_w_up[j])
                return matmul_residual(u, dense_w_down, j, xs, mod, mod_row_tm(1024), 5)
            xl = ffn(xl, lat_row)
            if not last:
                xc = ffn(xc, lambda tm: ctx_row)
        else:
            fin = final_norm_g if last else None
            xl = moe_layer(xl, norm2_g[i], mod, lat_row, router_w[j], router_b[j],
                           moe_w_gate[j], moe_w_up[j], moe_w_down[j], fin)
            if not last:
                xc = moe_layer(xc, norm2_g[i], mod, lambda tm: ctx_row, router_w[j], router_b[j],
                               moe_w_gate[j], moe_w_up[j], moe_w_down[j], None)
    if (DEPTH - 1) % 2 == 0:
        xl = final_rmsnorm(xl, final_norm_g)
    return xl.reshape(n_batch, SEQ, D_MODEL)
```

```python
import functools

import jax
import jax.numpy as jnp
from jax import lax
from jax.experimental import pallas as pl
from jax.experimental.pallas import tpu as pltpu

BF = jnp.bfloat16
F32 = jnp.float32

D_MODEL = 2048
SEQ = 2048
CTX_LEN = 256
DEPTH = 2
GRID_W = 64
GRID_ROWS = SEQ // GRID_W
NA_HEADS = 16
NA_HEAD_DIM = 64
NA_WIN_ROWS = 8
NA_WIN_COLS = 16
GLA_HEADS = 4
GLA_DK = 128
GLA_DV = 256
GLA_GATE_RANK = 16
GLA_TAU = 16.0
GQA_HEADS = 8
GQA_KV_HEADS = 2
GQA_HEAD_DIM = 128
GQA_GROUP = GQA_HEADS // GQA_KV_HEADS
ROPE_THETA = 10000.0
D_FF = 5632
N_EXPERTS = 8
TOP_K = 2
NORM_EPS = 1e-6
BRANCH_W = 1024
LOG2_E = 1.4426950408889634

C_NA_Q, C_NA_K, C_NA_V = 0, 1024, 2048
C_GLA_Q, C_GLA_K, C_GLA_V, C_GLA_OG = 3072, 3584, 4096, 5120
D_HEAD = 6144
C_GQA_Q, C_GQA_K, C_GQA_V = 0, 1024, 1280
C_GATE_A, C_GATE_B, C_GATE_C = 1536, 3584, 5632
C_LOW = 7680
D_TAIL = 8192
R_LOW = 6144
R_GQA_Q = 6176
R_END = 13856

V7X_LANES = 128
V7X_VMEM_LIMIT_BYTES = 56 * 1024 * 1024

NA_ROWS_PER_STEP = 4
GLA_STEP = 128
GLA_SUB = 16
GLA_HEADS_PER_STEP = 2
MOE_ROWS = 1536
MOE_SUB = 512
MOE_UP_TILE = 256
MOE_DOWN_TILE = 256
MOE_GATHER_ROWS = 512
MOE_CAST_CHUNK = 512


def _cparams(*sem):
    return pltpu.CompilerParams(dimension_semantics=sem,
                                vmem_limit_bytes=V7X_VMEM_LIMIT_BYTES)


def _dot(a, b):
    return jnp.dot(a, b, preferred_element_type=F32)


def _dot_nt(a, b):
    return lax.dot_general(a, b, (((1,), (1,)), ((), ())), preferred_element_type=F32)


def _dot_cast(a_ref, rows, w_ref, chunk):
    acc = None
    for k0 in range(0, w_ref.shape[0], chunk):
        part = _dot(a_ref[rows, k0:k0 + chunk], w_ref[k0:k0 + chunk, :].astype(BF))
        acc = part if acc is None else acc + part
    return acc


def _silu(x):
    return x * jax.nn.sigmoid(x)


def _latent_mod_row(tm):
    return lambda t: (t * tm) // SEQ


def _ctx_mod_row(n_batch):
    return lambda t: n_batch


def _ada_kernel(c_ref, w_ref, b_ref, o_ref):
    a = _silu(c_ref[...])
    o_ref[...] = _dot(a.astype(BF), w_ref[...].astype(BF)) + b_ref[...]


def ada_modulation(cvec, w_ada, b_ada):
    tn = 1024
    n6 = 6 * D_MODEL
    out = pl.pallas_call(
        _ada_kernel,
        grid=(DEPTH, n6 // tn),
        in_specs=[
            pl.BlockSpec((8, D_MODEL), lambda l, j: (0, 0)),
            pl.BlockSpec((None, D_MODEL, tn), lambda l, j: (l, 0, j)),
            pl.BlockSpec((None, 1, tn), lambda l, j: (l, 0, j)),
        ],
        out_specs=pl.BlockSpec((None, 8, tn), lambda l, j: (l, 0, j)),
        out_shape=jax.ShapeDtypeStruct((DEPTH, 8, n6), F32),
        compiler_params=_cparams("parallel", "parallel"),
        name="ada_modulation",
    )(cvec, w_ada, b_ada.reshape(DEPTH, 1, n6))
    return out.reshape(DEPTH, 8, 6, D_MODEL)


def _modulated_norm(x, g, mod, shift_idx, scale_idx):
    ms = jnp.mean(x * x, axis=-1, keepdims=True)
    y = x * lax.rsqrt(ms + NORM_EPS) * g
    return y * (1.0 + mod[scale_idx:scale_idx + 1]) + mod[shift_idx:shift_idx + 1]


def _rmsnorm_kernel(x_ref, g_ref, o_ref):
    x = x_ref[...]
    ms = jnp.mean(x * x, axis=-1, keepdims=True)
    o_ref[...] = x * lax.rsqrt(ms + NORM_EPS) * g_ref[...]


def final_rmsnorm(x, g, tm=512):
    m = x.shape[0]
    return pl.pallas_call(
        _rmsnorm_kernel,
        grid=(m // tm,),
        in_specs=[pl.BlockSpec((tm, D_MODEL), lambda t: (t, 0)),
                  pl.BlockSpec((1, D_MODEL), lambda t: (0, 0))],
        out_specs=pl.BlockSpec((tm, D_MODEL), lambda t: (t, 0)),
        out_shape=jax.ShapeDtypeStruct((m, D_MODEL), F32),
        compiler_params=_cparams("parallel"),
        name="final_rmsnorm",
    )(x, g.reshape(1, D_MODEL))


def _in_proj_kernel(x_ref, g_ref, mod_ref, wh_ref, wt_ref, oh_ref, ot_ref, h_s, *, n_head):
    j = pl.program_id(1)

    @pl.when(j == 0)
    def _():
        h_s[...] = _modulated_norm(x_ref[...], g_ref[...], mod_ref[...], 0, 1).astype(h_s.dtype)

    @pl.when(j < n_head)
    def _():
        oh_ref[...] = _dot(h_s[...], wh_ref[...]).astype(oh_ref.dtype)

    @pl.when(j >= n_head)
    def _():
        ot_ref[...] = _dot(h_s[...], wt_ref[...]).astype(ot_ref.dtype)


def in_proj(x, g, mod, mod_row, w_head, w_tail, layer, tm=1024, tn=512):
    m = x.shape[0]
    tm = min(tm, m)
    n_head, n_tail = D_HEAD // tn, D_TAIL // tn
    head_col = lambda j: jnp.minimum(j, n_head - 1)
    tail_col = lambda j: jnp.maximum(j - n_head, 0)
    return pl.pallas_call(
        functools.partial(_in_proj_kernel, n_head=n_head),
        grid=(m // tm, n_head + n_tail),
        in_specs=[pl.BlockSpec((tm, D_MODEL), lambda i, j: (i, 0)),
                  pl.BlockSpec((1, D_MODEL), lambda i, j: (0, 0)),
                  pl.BlockSpec((None, 6, D_MODEL), lambda i, j: (mod_row(i), 0, 0)),
                  pl.BlockSpec((None, D_MODEL, tn), lambda i, j: (layer, 0, head_col(j))),
                  pl.BlockSpec((None, D_MODEL, tn), lambda i, j: (layer, 0, tail_col(j)))],
        out_specs=[pl.BlockSpec((tm, tn), lambda i, j: (i, head_col(j))),
                   pl.BlockSpec((tm, tn), lambda i, j: (i, tail_col(j)))],
        out_shape=[jax.ShapeDtypeStruct((m, D_HEAD), BF), jax.ShapeDtypeStruct((m, D_TAIL), BF)],
        scratch_shapes=[pltpu.VMEM((tm, D_MODEL), BF)],
        compiler_params=_cparams("parallel", "arbitrary"),
        name="in_proj",
    )(x, g.reshape(1, D_MODEL), mod, w_head, w_tail)


def _mm_res_kernel(a_ref, w_ref, x_ref, mod_ref, o_ref, *, gate_idx):
    y = _dot(a_ref[...], w_ref[...].astype(BF))
    o_ref[...] = x_ref[...] + mod_ref[gate_idx:gate_idx + 1, :] * y


def matmul_residual(a, w, layer, x, mod, mod_row, gate_idx, tm=1024, tn=256):
    m, k = a.shape
    n = w.shape[2]
    tm = min(tm, m)
    return pl.pallas_call(
        functools.partial(_mm_res_kernel, gate_idx=gate_idx),
        grid=(m // tm, n // tn),
        in_specs=[pl.BlockSpec((tm, k), lambda i, j: (i, 0)),
                  pl.BlockSpec((None, k, tn), lambda i, j: (layer, 0, j)),
                  pl.BlockSpec((tm, tn), lambda i, j: (i, j)),
                  pl.BlockSpec((None, 6, tn), lambda i, j: (mod_row(i), 0, j))],
        out_specs=pl.BlockSpec((tm, tn), lambda i, j: (i, j)),
        out_shape=jax.ShapeDtypeStruct((m, n), F32),
        compiler_params=_cparams("parallel", "parallel"),
        name="matmul_residual",
    )(a, w, x, mod)


def _merge_kernel(a_ref, b_ref, c_ref, ga_ref, gb_ref, gc_ref, wa_ref, wb_ref, wc_ref, o_ref):
    def branch(x_ref, g_ref, w_ref):
        return jax.nn.sigmoid(g_ref[...].astype(F32)) * _dot(x_ref[...], w_ref[...].astype(BF))

    o_ref[...] = (branch(a_ref, ga_ref, wa_ref) + branch(b_ref, gb_ref, wb_ref)
                  + branch(c_ref, gc_ref, wc_ref)).astype(o_ref.dtype)


def merge_branches(a, b, c, proj, w_pa, w_pb, w_pc, layer, tm=1024, tn=512):
    m = a.shape[0]
    tm = min(tm, m)
    x_spec = pl.BlockSpec((tm, BRANCH_W), lambda i, j: (i, 0))
    w_spec = pl.BlockSpec((None, BRANCH_W, tn), lambda i, j: (layer, 0, j))

    def gate_spec(col):
        return pl.BlockSpec((tm, tn), lambda i, j: (i, col // tn + j))

    return pl.pallas_call(
        _merge_kernel,
        grid=(m // tm, D_MODEL // tn),
        in_specs=[x_spec, x_spec, x_spec,
                  gate_spec(C_GATE_A), gate_spec(C_GATE_B), gate_spec(C_GATE_C),
                  w_spec, w_spec, w_spec],
        out_specs=pl.BlockSpec((tm, tn), lambda i, j: (i, j)),
        out_shape=jax.ShapeDtypeStruct((m, D_MODEL), BF),
        compiler_params=_cparams("parallel", "parallel"),
        name="merge_branches",
    )(a, b, c, proj, proj, proj, w_pa, w_pb, w_pc)


def _ffn_up_kernel(x_ref, g_ref, mod_ref, wg_ref, wu_ref, o_ref, h_s):
    @pl.when(pl.program_id(1) == 0)
    def _():
        h_s[...] = _modulated_norm(x_ref[...], g_ref[...], mod_ref[...], 3, 4).astype(h_s.dtype)

    g = _dot_cast(h_s, slice(None), wg_ref, 512)
    u = _dot_cast(h_s, slice(None), wu_ref, 512)
    o_ref[...] = (_silu(g) * u).astype(o_ref.dtype)


def ffn_up(x, norm_g, mod, mod_row, w_gate, w_up, tm=1024, tn=512):
    m = x.shape[0]
    tm = min(tm, m)
    return pl.pallas_call(
        _ffn_up_kernel,
        grid=(m // tm, D_FF // tn),
        in_specs=[pl.BlockSpec((tm, D_MODEL), lambda i, j: (i, 0)),
                  pl.BlockSpec((1, D_MODEL), lambda i, j: (0, 0)),
                  pl.BlockSpec((None, 6, D_MODEL), lambda i, j: (mod_row(i), 0, 0)),
                  pl.BlockSpec((D_MODEL, tn), lambda i, j: (0, j)),
                  pl.BlockSpec((D_MODEL, tn), lambda i, j: (0, j))],
        out_specs=pl.BlockSpec((tm, tn), lambda i, j: (i, j)),
        out_shape=jax.ShapeDtypeStruct((m, D_FF), BF),
        scratch_shapes=[pltpu.VMEM((tm, D_MODEL), BF)],
        compiler_params=_cparams("parallel", "arbitrary"),
        name="ffn_up",
    )(x, norm_g.reshape(1, D_MODEL), mod, w_gate, w_up)


def _gqa_prep_kernel(q_ref, k_ref, cos_ref, sin_ref, qg_ref, kg_ref, qo_ref, ko_ref):
    cos = cos_ref[...]
    sin = sin_ref[...]
    lane = lax.broadcasted_iota(jnp.int32, cos.shape, 1)
    even = (lane & 1) == 0

    def prep(x_ref, g_ref, o_ref, heads):
        for h in range(heads):
            sl = slice(h * GQA_HEAD_DIM, (h + 1) * GQA_HEAD_DIM)
            x = x_ref[:, sl].astype(F32)
            ms = jnp.mean(x * x, axis=-1, keepdims=True)
            y = x * lax.rsqrt(ms + NORM_EPS) * g_ref[...]
            swapped = jnp.where(even, pltpu.roll(y, GQA_HEAD_DIM - 1, 1), pltpu.roll(y, 1, 1))
            o_ref[:, sl] = (y * cos + swapped * sin).astype(o_ref.dtype)

    prep(q_ref, qg_ref, qo_ref, GQA_HEADS)
    prep(k_ref, kg_ref, ko_ref, GQA_KV_HEADS)


def gqa_prep(proj, cos_t, sin_t, qn_g, kn_g, table_block, tm=256):
    m = proj.shape[0]
    qw = GQA_HEADS * GQA_HEAD_DIM
    kw = GQA_KV_HEADS * GQA_HEAD_DIM
    return pl.pallas_call(
        _gqa_prep_kernel,
        grid=(m // tm,),
        in_specs=[pl.BlockSpec((tm, qw), lambda t: (t, C_GQA_Q // qw)),
                  pl.BlockSpec((tm, kw), lambda t: (t, C_GQA_K // kw)),
                  pl.BlockSpec((tm, GQA_HEAD_DIM), lambda t: (table_block(t), 0)),
                  pl.BlockSpec((tm, GQA_HEAD_DIM), lambda t: (table_block(t), 0)),
                  pl.BlockSpec((1, GQA_HEAD_DIM), lambda t: (0, 0)),
                  pl.BlockSpec((1, GQA_HEAD_DIM), lambda t: (0, 0))],
        out_specs=[pl.BlockSpec((tm, qw), lambda t: (t, 0)),
                   pl.BlockSpec((tm, kw), lambda t: (t, 0))],
        out_shape=[jax.ShapeDtypeStruct((m, qw), BF), jax.ShapeDtypeStruct((m, kw), BF)],
        compiler_params=_cparams("parallel"),
        name="gqa_prep",
    )(proj, proj, cos_t, sin_t, qn_g.reshape(1, -1), kn_g.reshape(1, -1))


def rope_tables():
    half = GQA_HEAD_DIM // 2
    freqs = ROPE_THETA ** (-jnp.arange(0, half, 2, dtype=F32) / half)
    t = jnp.arange(SEQ)
    row = (t // GRID_W).astype(F32)
    col = (t % GRID_W).astype(F32)
    ang = jnp.concatenate([row[:, None] * freqs, col[:, None] * freqs], axis=-1)
    cos = jnp.repeat(jnp.cos(ang), 2, axis=-1)
    sin = jnp.repeat(jnp.sin(ang), 2, axis=-1)
    sign = jnp.tile(jnp.array([-1.0, 1.0], F32), half)
    cos = jnp.concatenate([cos, jnp.ones((256, GQA_HEAD_DIM), F32)], axis=0)
    sin = jnp.concatenate([sin * sign, jnp.zeros((256, GQA_HEAD_DIM), F32)], axis=0)
    return cos, sin


def _attend(q, kv_list, scale):
    scores = []
    for k, _, bias in kv_list:
        s = _dot_nt(q, k)
        if bias is not None:
            s = s + bias
        scores.append(s)

    def lane_chunks(xs):
        return [x[:, c:c + V7X_LANES] for x in xs for c in range(0, x.shape[1], V7X_LANES)]

    m = jnp.max(functools.reduce(jnp.maximum, lane_chunks(scores)), axis=-1, keepdims=True)
    ps = [jnp.exp2((s - m) * (scale * LOG2_E)) for s in scores]
    den = jnp.sum(functools.reduce(jnp.add, lane_chunks(ps)), axis=-1, keepdims=True)
    o = functools.reduce(jnp.add, [_dot(p.astype(BF), v) for p, (_, v, _) in zip(ps, kv_list)])
    return o / den


def _head_pair_rows(q2):
    lane = lax.broadcasted_iota(jnp.int32, q2.shape, 1)
    first = lane < NA_HEAD_DIM
    zero = jnp.zeros_like(q2)
    return jnp.concatenate([jnp.where(first, q2, zero), jnp.where(first, zero, q2)], axis=0)


def _head_pair_merge(o, m):
    lane = lax.broadcasted_iota(jnp.int32, (m, 2 * NA_HEAD_DIM), 1)
    return jnp.where(lane < NA_HEAD_DIM, o[:m], o[m:])


def _na_kernel(q_ref, kl_ref, vl_ref, kc_ref, vc_ref, *rest):
    bias_refs, o_ref = rest[:NA_ROWS_PER_STEP], rest[NA_ROWS_PER_STEP]
    n_loc = NA_WIN_ROWS * GRID_W
    scale = NA_HEAD_DIM ** -0.5
    for rr in range(NA_ROWS_PER_STEP):
        r = pl.program_id(1) * NA_ROWS_PER_STEP + rr
        start = jnp.clip(r - NA_WIN_ROWS // 2, 0, GRID_ROWS - NA_WIN_ROWS) * GRID_W
        start = pl.multiple_of(start, GRID_W)
        q_rows = slice(rr * GRID_W, (rr + 1) * GRID_W)
        for hp in range(NA_HEADS // 2):
            sl = slice(hp * 128, (hp + 1) * 128)
            qq = _head_pair_rows(q_ref[q_rows, sl])
            bias = jnp.concatenate([bias_refs[rr][2 * hp], bias_refs[rr][2 * hp + 1]], axis=0)
            o = _attend(qq, [(kl_ref[pl.ds(start, n_loc), sl], vl_ref[pl.ds(start, n_loc), sl], bias),
                             (kc_ref[:, sl], vc_ref[:, sl], None)], scale)
            o_ref[q_rows, sl] = _head_pair_merge(o, GRID_W).astype(o_ref.dtype)


def na_latent(proj_l, proj_c, bias_tab, n_batch):
    w = NA_HEADS * NA_HEAD_DIM
    rps = NA_ROWS_PER_STEP
    steps = GRID_ROWS // rps

    def pattern(r):
        return jnp.where(r < 4, r, jnp.where(r > GRID_ROWS - 4, r - (GRID_ROWS - NA_WIN_ROWS), 4))

    def bias_spec(rr):
        return pl.BlockSpec((None, NA_HEADS, GRID_W, NA_WIN_ROWS * GRID_W),
                            lambda b, i: (pattern(i * rps + rr), 0, 0, 0))

    return pl.pallas_call(
        _na_kernel,
        grid=(n_batch, steps),
        in_specs=[pl.BlockSpec((rps * GRID_W, w), lambda b, i: (b * steps + i, C_NA_Q // w)),
                  pl.BlockSpec((SEQ, w), lambda b, i: (b, C_NA_K // w)),
                  pl.BlockSpec((SEQ, w), lambda b, i: (b, C_NA_V // w)),
                  pl.BlockSpec((CTX_LEN, w), lambda b, i: (b, C_NA_K // w)),
                  pl.BlockSpec((CTX_LEN, w), lambda b, i: (b, C_NA_V // w))]
                 + [bias_spec(rr) for rr in range(rps)],
        out_specs=pl.BlockSpec((rps * GRID_W, w), lambda b, i: (b * steps + i, 0)),
        out_shape=jax.ShapeDtypeStruct((n_batch * SEQ, w), BF),
        compiler_params=_cparams("parallel", "arbitrary"),
        name="na_latent",
    )(proj_l, proj_l, proj_l, proj_c, proj_c, *([bias_tab] * rps))


def na_bias_table(rpb):
    cols = jnp.arange(GRID_W)
    col_start = jnp.clip(cols - NA_WIN_COLS // 2, 0, GRID_W - NA_WIN_COLS)
    in_win = (cols[None, :] >= col_start[:, None]) & (cols[None, :] < col_start[:, None] + NA_WIN_COLS)
    dc = jnp.clip(cols[None, :] - cols[:, None] + NA_WIN_COLS - 1, 0, 2 * NA_WIN_COLS - 2)
    onehot = (dc[None] == jnp.arange(2 * NA_WIN_COLS - 1)[:, None, None]).astype(F32)
    by_col = jnp.einsum('hdc,cqk->hdqk', rpb.astype(F32), onehot, precision=lax.Precision.HIGHEST)
    by_col = jnp.where(in_win[None, None], by_col * (NA_HEAD_DIM ** 0.5), -jnp.inf)
    last = NA_WIN_ROWS - 1
    bias = jnp.stack([by_col[:, last - p:last - p + NA_WIN_ROWS] for p in range(NA_WIN_ROWS)])
    bias = bias.transpose(0, 1, 3, 2, 4)
    return bias.reshape(NA_WIN_ROWS, NA_HEADS, GRID_W, NA_WIN_ROWS * GRID_W)


def _na_ctx_kernel(q_ref, k_ref, v_ref, o_ref):
    scale = NA_HEAD_DIM ** -0.5
    for hp in range(NA_HEADS // 2):
        sl = slice(hp * 128, (hp + 1) * 128)
        o = _attend(_head_pair_rows(q_ref[:, sl]), [(k_ref[:, sl], v_ref[:, sl], None)], scale)
        o_ref[:, sl] = _head_pair_merge(o, CTX_LEN).astype(o_ref.dtype)


def na_context(proj_c, n_batch):
    w = NA_HEADS * NA_HEAD_DIM
    return pl.pallas_call(
        _na_ctx_kernel,
        grid=(n_batch,),
        in_specs=[pl.BlockSpec((CTX_LEN, w), lambda b: (b, C_NA_Q // w)),
                  pl.BlockSpec((CTX_LEN, w), lambda b: (b, C_NA_K // w)),
                  pl.BlockSpec((CTX_LEN, w), lambda b: (b, C_NA_V // w))],
        out_specs=pl.BlockSpec((CTX_LEN, w), lambda b: (b, 0)),
        out_shape=jax.ShapeDtypeStruct((n_batch * CTX_LEN, w), BF),
        compiler_params=_cparams("parallel"),
        name="na_context",
    )(proj_c, proj_c, proj_c)


def _gqa_rows(q_ref):
    return jnp.concatenate([q_ref[:, g * GQA_HEAD_DIM:(g + 1) * GQA_HEAD_DIM]
                            for g in range(GQA_GROUP)], axis=0)


def _gqa_store(o, o_ref):
    tq = o_ref.shape[0]
    for g in range(GQA_GROUP):
        o_ref[:, g * GQA_HEAD_DIM:(g + 1) * GQA_HEAD_DIM] = o[g * tq:(g + 1) * tq].astype(o_ref.dtype)


def _gqa_kernel(q_ref, kl_ref, vl_ref, kc_ref, vc_ref, o_ref):
    for g in range(GQA_GROUP):
        sl = slice(g * GQA_HEAD_DIM, (g + 1) * GQA_HEAD_DIM)
        o = _attend(q_ref[:, sl], [(kl_ref[...], vl_ref[...], None), (kc_ref[...], vc_ref[...], None)],
                    GQA_HEAD_DIM ** -0.5)
        o_ref[:, sl] = o.astype(o_ref.dtype)


def _gqa_ctx_kernel(q_ref, kc_ref, vc_ref, o_ref):
    o = _attend(_gqa_rows(q_ref), [(kc_ref[...], vc_ref[...], None)], GQA_HEAD_DIM ** -0.5)
    _gqa_store(o, o_ref)


def gqa_latent(q_l, k_l, k_c, proj_l, proj_c, n_batch, tq=512):
    gw = GQA_GROUP * GQA_HEAD_DIM
    nq = SEQ // tq
    dh = GQA_HEAD_DIM
    return pl.pallas_call(
        _gqa_kernel,
        grid=(n_batch, GQA_KV_HEADS, nq),
        in_specs=[pl.BlockSpec((tq, gw), lambda b, h, i: (b * nq + i, h)),
                  pl.BlockSpec((SEQ, dh), lambda b, h, i: (b, h)),
                  pl.BlockSpec((SEQ, dh), lambda b, h, i: (b, C_GQA_V // dh + h)),
                  pl.BlockSpec((CTX_LEN, dh), lambda b, h, i: (b, h)),
                  pl.BlockSpec((CTX_LEN, dh), lambda b, h, i: (b, C_GQA_V // dh + h))],
        out_specs=pl.BlockSpec((tq, gw), lambda b, h, i: (b * nq + i, h)),
        out_shape=jax.ShapeDtypeStruct((n_batch * SEQ, GQA_HEADS * dh), BF),
        compiler_params=_cparams("parallel", "parallel", "arbitrary"),
        name="gqa_latent",
    )(q_l, k_l, proj_l, k_c, proj_c)


def gqa_context(q_c, k_c, proj_c, n_batch):
    gw = GQA_GROUP * GQA_HEAD_DIM
    dh = GQA_HEAD_DIM
    return pl.pallas_call(
        _gqa_ctx_kernel,
        grid=(n_batch, GQA_KV_HEADS),
        in_specs=[pl.BlockSpec((CTX_LEN, gw), lambda b, h: (b, h)),
                  pl.BlockSpec((CTX_LEN, dh), lambda b, h: (b, h)),
                  pl.BlockSpec((CTX_LEN, dh), lambda b, h: (b, C_GQA_V // dh + h))],
        out_specs=pl.BlockSpec((CTX_LEN, gw), lambda b, h: (b, h)),
        out_shape=jax.ShapeDtypeStruct((n_batch * CTX_LEN, GQA_HEADS * dh), BF),
        compiler_params=_cparams("parallel", "parallel"),
        name="gqa_context",
    )(q_c, k_c, proj_c)


def _log_sigmoid(z):
    return jnp.minimum(z, 0.0) - jnp.log1p(jnp.exp(-jnp.abs(z)))


def _gla_decay(c, direction, head, low_s, wa_ref, ba_ref):
    n = GLA_STEP
    rows = pl.ds(pl.multiple_of(c * n, n), n)
    z = _dot(low_s[rows, :], wa_ref[direction, head].astype(BF)) + ba_ref[direction, head]
    g = _log_sigmoid(z) * (1.0 / GLA_TAU)
    ri = lax.broadcasted_iota(jnp.int32, (n, n), 0)
    ci = lax.broadcasted_iota(jnp.int32, (n, n), 1)
    tri = (ci <= ri) if direction == 0 else (ci >= ri)
    tri = jnp.where(tri, 1.0, 0.0).astype(BF)
    g_hi = g.astype(BF)
    g_lo = (g - g_hi.astype(F32)).astype(BF)
    return (_dot(tri, g_hi) + _dot(tri, g_lo)) * LOG2_E


def _gla_step(c, direction, head, bc, q_s, k_s, v_s, st_ref, o_s):
    n = GLA_STEP
    r0 = pl.multiple_of(c * n, n)
    rows = pl.ds(r0, n)
    k_lanes = slice(head * GLA_DK, (head + 1) * GLA_DK)
    v_lanes = slice(head * GLA_DV, (head + 1) * GLA_DV)
    qf = q_s[rows, k_lanes].astype(F32) * (GLA_DK ** -0.5)
    k_bf = k_s[rows, k_lanes]
    kf = k_bf.astype(F32)
    v = v_s[rows, v_lanes]
    btot = bc[n - 1:n] if direction == 0 else bc[0:1]

    state = st_ref[...]
    q_in = (qf * jnp.exp2(bc)).astype(BF)
    o = _dot_nt(q_in, state.astype(BF))
    k_out = (kf * jnp.exp2(btot - bc)).astype(BF)
    v_t = v.astype(F32).T.astype(BF)
    st_ref[...] = jnp.exp2(btot) * state + _dot(v_t, k_out)

    across = None
    size = n // 2
    while size >= GLA_SUB:
        pieces = []
        for base in range(0, n, 2 * size):
            mid = base + size
            if direction == 0:
                early, late, ref = slice(base, mid), slice(mid, mid + size), bc[mid - 1:mid]
            else:
                early, late, ref = slice(mid, mid + size), slice(base, mid), bc[mid:mid + 1]
            k_dec = (kf[early] * jnp.exp2(ref - bc[early])).astype(BF)
            k_rows = [jnp.zeros((early.start, GLA_DK), BF)] if early.start else []
            k_rows.append(k_dec)
            if early.stop < n:
                k_rows.append(jnp.zeros((n - early.stop, GLA_DK), BF))
            part = _dot_nt((qf[late] * jnp.exp2(bc[late] - ref)).astype(BF),
                           jnp.concatenate(k_rows, axis=0))
            none = jnp.zeros((size, n), F32)
            pieces += [none, part] if direction == 0 else [part, none]
        level = jnp.concatenate(pieces, axis=0)
        across = level if across is None else across + level
        size //= 2

    lane = lax.broadcasted_iota(jnp.int32, (GLA_SUB, n), 1)
    sub_row = lax.broadcasted_iota(jnp.int32, (GLA_SUB, 1), 0)
    neg_inf = jnp.float32(-jnp.inf)
    blocks = []
    for blk in range(n // GLA_SUB):
        lo, hi = blk * GLA_SUB, (blk + 1) * GLA_SUB
        q_b, b_b, a_b = qf[lo:hi], bc[lo:hi], across[lo:hi]
        decayed = []
        for jl in range(GLA_SUB):
            keep = (sub_row >= jl) if direction == 0 else (sub_row <= jl)
            dec = jnp.exp2(jnp.where(keep, b_b - bc[lo + jl:lo + jl + 1], neg_inf))
            decayed.append((q_b * dec).astype(BF))
        pair = _dot_nt(jnp.concatenate(decayed, axis=0), k_bf)
        for jl in range(GLA_SUB):
            a_b = jnp.where(lane == lo + jl, pair[jl * GLA_SUB:(jl + 1) * GLA_SUB], a_b)
        blocks.append(a_b)
    attn = jnp.concatenate(blocks, axis=0).astype(BF)
    o_s[rows, :] = o + _dot(attn, v)


def _gla_kernel(ql, kl, vl, ogl, lowl, qc, kc, vc, ogc, lowc, wa_ref, ba_ref, ng_ref,
                yl_ref, yc_ref, q_s, k_s, v_s, low_s, *per_head):
    hps = GLA_HEADS_PER_STEP
    of_s, ob_s, stf, stb, bcf, bcb = (per_head[g * hps:(g + 1) * hps] for g in range(6))
    nc = CTX_LEN
    q_s[0:nc, :] = qc[...]
    q_s[nc:, :] = ql[...]
    k_s[0:nc, :] = kc[...]
    k_s[nc:, :] = kl[...]
    v_s[0:nc, :] = vc[...]
    v_s[nc:, :] = vl[...]
    low_s[0:nc, :] = lowc[...]
    low_s[nc:, :] = lowl[...]
    for st in (*stf, *stb):
        st[...] = jnp.zeros_like(st)
    n_ctx = CTX_LEN // GLA_STEP
    n_all = (CTX_LEN + SEQ) // GLA_STEP

    def backward_step(i):
        return jnp.where(i < n_ctx, n_ctx - 1 - i, n_all + n_ctx - 1 - i)

    def store_decays(i):
        for head in range(hps):
            bcf[head][...] = _gla_decay(i, 0, head, low_s, wa_ref, ba_ref)
            bcb[head][...] = _gla_decay(backward_step(i), 1, head, low_s, wa_ref, ba_ref)

    store_decays(0)

    def body(i, carry):
        for head in range(hps):
            _gla_step(i, 0, head, bcf[head][...], q_s, k_s, v_s, stf[head], of_s[head])
            _gla_step(backward_step(i), 1, head, bcb[head][...], q_s, k_s, v_s, stb[head], ob_s[head])
        store_decays(jnp.minimum(i + 1, n_all - 1))
        return carry

    lax.fori_loop(0, n_all, body, 0)

    def finish(lo, hi, og_ref, y_ref):
        for head in range(GLA_HEADS_PER_STEP):
            lanes = slice(head * GLA_DV, (head + 1) * GLA_DV)
            o = of_s[head][lo:hi, :] + ob_s[head][lo:hi, :]
            ms = jnp.mean(o * o, axis=-1, keepdims=True)
            y = o * lax.rsqrt(ms + NORM_EPS) * ng_ref[...]
            y_ref[:, lanes] = (y * _silu(og_ref[:, lanes].astype(F32))).astype(y_ref.dtype)

    finish(0, nc, ogc, yc_ref)
    finish(nc, nc + SEQ, ogl, yl_ref)


def gla_bidirectional(head_l, head_c, tail_l, tail_c, w_a2, b_a, norm_g, n_batch):
    hps = GLA_HEADS_PER_STEP
    dk, dv = GLA_DK, GLA_DV
    kw, vw = hps * dk, hps * dv
    wa = jnp.zeros((2, GLA_HEADS, V7X_LANES, dk), F32)
    for d in range(2):
        wa = wa.at[d, :, d * GLA_GATE_RANK:(d + 1) * GLA_GATE_RANK, :].set(
            w_a2[d].reshape(GLA_GATE_RANK, GLA_HEADS, dk).transpose(1, 0, 2))
    ba = b_a.reshape(2, GLA_HEADS, 1, dk)

    def specs(rows):
        return [pl.BlockSpec((rows, kw), lambda b, h: (b, C_GLA_Q // kw + h)),
                pl.BlockSpec((rows, kw), lambda b, h: (b, C_GLA_K // kw + h)),
                pl.BlockSpec((rows, vw), lambda b, h: (b, C_GLA_V // vw + h)),
                pl.BlockSpec((rows, vw), lambda b, h: (b, C_GLA_OG // vw + h)),
                pl.BlockSpec((rows, V7X_LANES), lambda b, h: (b, C_LOW // V7X_LANES))]

    n_rows = CTX_LEN + SEQ
    return pl.pallas_call(
        _gla_kernel,
        grid=(n_batch, GLA_HEADS // hps),
        in_specs=specs(SEQ) + specs(CTX_LEN) + [
            pl.BlockSpec((2, hps, V7X_LANES, dk), lambda b, h: (0, h, 0, 0)),
            pl.BlockSpec((2, hps, 1, dk), lambda b, h: (0, h, 0, 0)),
            pl.BlockSpec((1, dv), lambda b, h: (0, 0))],
        out_specs=[pl.BlockSpec((SEQ, vw), lambda b, h: (b, h)),
                   pl.BlockSpec((CTX_LEN, vw), lambda b, h: (b, h))],
        out_shape=[jax.ShapeDtypeStruct((n_batch * SEQ, GLA_HEADS * dv), BF),
                   jax.ShapeDtypeStruct((n_batch * CTX_LEN, GLA_HEADS * dv), BF)],
        scratch_shapes=[pltpu.VMEM((n_rows, kw), BF), pltpu.VMEM((n_rows, kw), BF),
                        pltpu.VMEM((n_rows, vw), BF), pltpu.VMEM((n_rows, V7X_LANES), BF)]
                       + [pltpu.VMEM((n_rows, dv), F32)] * (2 * hps)
                       + [pltpu.VMEM((dv, dk), F32)] * (2 * hps)
                       + [pltpu.VMEM((GLA_STEP, dk), F32)] * (2 * hps),
        compiler_params=_cparams("parallel", "parallel"),
        name="gla_bidirectional",
    )(*([head_l] * 4 + [tail_l] + [head_c] * 4 + [tail_c]), wa, ba, norm_g.reshape(1, dv))


def _router_kernel(x_ref, g_ref, mod_ref, rw_ref, rb_ref, h_ref, idx_ref, wt_ref):
    h = _modulated_norm(x_ref[...], g_ref[...], mod_ref[...], 3, 4)
    h_ref[...] = h.astype(h_ref.dtype)
    logits = jnp.dot(h, rw_ref[...], preferred_element_type=F32,
                     precision=lax.Precision.HIGHEST) + rb_ref[...]
    lane = lax.broadcasted_iota(jnp.int32, logits.shape, 1)
    neg_inf = jnp.float32(-jnp.inf)
    logits = jnp.where(lane < N_EXPERTS, logits, neg_inf)
    m1 = jnp.max(logits, axis=-1, keepdims=True)
    lane_f = lane.astype(F32)
    i1 = jnp.min(jnp.where(logits == m1, lane_f, float(V7X_LANES)), axis=-1, keepdims=True)
    rest = jnp.where(lane_f == i1, neg_inf, logits)
    m2 = jnp.max(rest, axis=-1, keepdims=True)
    i2 = jnp.min(jnp.where(rest == m2, lane_f, float(V7X_LANES)), axis=-1, keepdims=True)
    e = jnp.exp(m2 - m1)
    w1 = 1.0 / (1.0 + e)
    idx_ref[...] = jnp.where(lane == 0, i1, jnp.where(lane == 1, i2, 0.0)).astype(jnp.int32)
    wt_ref[...] = jnp.where(lane == 0, w1, jnp.where(lane == 1, e * w1, 0.0))


def moe_router(x, g, mod, mod_row, router_w, router_b, tm=512):
    m = x.shape[0]
    rw = jnp.zeros((D_MODEL, V7X_LANES), F32).at[:, :N_EXPERTS].set(router_w)
    rb = jnp.zeros((1, V7X_LANES), F32).at[0, :N_EXPERTS].set(router_b)
    return pl.pallas_call(
        _router_kernel,
        grid=(m // tm,),
        in_specs=[pl.BlockSpec((tm, D_MODEL), lambda t: (t, 0)),
                  pl.BlockSpec((1, D_MODEL), lambda t: (0, 0)),
                  pl.BlockSpec((None, 6, D_MODEL), lambda t: (mod_row(t), 0, 0)),
                  pl.BlockSpec((D_MODEL, V7X_LANES), lambda t: (0, 0)),
                  pl.BlockSpec((1, V7X_LANES), lambda t: (0, 0))],
        out_specs=[pl.BlockSpec((tm, D_MODEL), lambda t: (t, 0)),
                   pl.BlockSpec((tm, V7X_LANES), lambda t: (t, 0)),
                   pl.BlockSpec((tm, V7X_LANES), lambda t: (t, 0))],
        out_shape=[jax.ShapeDtypeStruct((m, D_MODEL), F32),
                   jax.ShapeDtypeStruct((m, V7X_LANES), jnp.int32),
                   jax.ShapeDtypeStruct((m, V7X_LANES), F32)],
        compiler_params=_cparams("parallel"),
        name="moe_router",
    )(x, g.reshape(1, D_MODEL), mod, rw, rb)


def _row_copy(src_hbm, src_row, dst, dst_row, sem):
    return pltpu.make_async_copy(src_hbm.at[pl.ds(src_row, 1)], dst.at[pl.ds(dst_row, 1)], sem)


def _gather_tile(src_hbm, idx_ref, n_rows, per_row, buf, sem, action):
    def body(i, carry):
        for k in range(per_row):
            cp = _row_copy(src_hbm, idx_ref[0, per_row * i + k], buf.at[k], i, sem)
            cp.start() if action == "start" else cp.wait()
        return carry

    lax.fori_loop(0, n_rows, body, 0)


def _dispatch_kernel(nv_ref, idx_ref, idx_next_ref, h_hbm, o_ref, buf, sem):
    t = pl.program_id(0)
    rows = o_ref.shape[0]
    slot = t % 2

    def start_tile(tile_idx_ref, tile, s):
        nv = nv_ref[tile]

        @pl.when(nv < rows)
        def _():
            buf[s] = jnp.zeros(buf.shape[1:], buf.dtype)

        _gather_tile(h_hbm, tile_idx_ref, nv, 1, buf.at[s], sem.at[s], "start")

    @pl.when(t == 0)
    def _():
        start_tile(idx_ref, 0, 0)

    @pl.when(t + 1 < pl.num_programs(0))
    def _():
        start_tile(idx_next_ref, t + 1, 1 - slot)

    _gather_tile(h_hbm, idx_ref, nv_ref[t], 1, buf.at[slot], sem.at[slot], "wait")
    o_ref[...] = buf[slot, 0].astype(o_ref.dtype)


def moe_dispatch(h, slot_t, tile_valid):
    n_slots = slot_t.shape[0]
    rows = MOE_GATHER_ROWS
    n_tiles = n_slots // rows
    idx = slot_t.reshape(n_tiles, 1, rows)
    grid_spec = pltpu.PrefetchScalarGridSpec(
        num_scalar_prefetch=1,
        grid=(n_tiles,),
        in_specs=[pl.BlockSpec((None, 1, rows), lambda t, nv: (t, 0, 0), memory_space=pltpu.SMEM),
                  pl.BlockSpec((None, 1, rows), lambda t, nv: (jnp.minimum(t + 1, n_tiles - 1), 0, 0),
                               memory_space=pltpu.SMEM),
                  pl.BlockSpec(memory_space=pl.ANY)],
        out_specs=pl.BlockSpec((rows, D_MODEL), lambda t, nv: (t, 0)),
        scratch_shapes=[pltpu.VMEM((2, 1, rows, D_MODEL), F32), pltpu.SemaphoreType.DMA((2,))],
    )
    return pl.pallas_call(
        _dispatch_kernel,
        grid_spec=grid_spec,
        out_shape=jax.ShapeDtypeStruct((n_slots, D_MODEL), BF),
        compiler_params=_cparams("arbitrary"),
        name="moe_dispatch",
    )(tile_valid, idx, idx, h)


def _moe_ffn_kernel(be_ref, nv_ref, nu_ref, x_ref, wg_ref, wu_ref, wd_ref, o_ref, act):
    s = pl.program_id(1)
    nv = nv_ref[pl.program_id(0)]
    n_up = D_FF // MOE_UP_TILE
    n_sub = (nv + MOE_SUB - 1) // MOE_SUB

    @pl.when(s < n_up)
    def _():
        cols = pl.ds(pl.multiple_of(s * MOE_UP_TILE, MOE_UP_TILE), MOE_UP_TILE)

        def up(n_rows):
            rows = slice(0, n_rows)
            g = _dot_cast(x_ref, rows, wg_ref, MOE_CAST_CHUNK)
            u = _dot_cast(x_ref, rows, wu_ref, MOE_CAST_CHUNK)
            act[rows, cols] = (_silu(g) * u).astype(BF)

        for parts in range(1, MOE_ROWS // MOE_SUB + 1):
            pl.when(n_sub == parts)(functools.partial(up, parts * MOE_SUB))

    @pl.when(s >= n_up)
    def _():
        def down(n_rows):
            if n_rows:
                o_ref[0:n_rows, :] = _dot_cast(act, slice(0, n_rows), wd_ref, MOE_CAST_CHUNK)
            if n_rows < MOE_ROWS:
                o_ref[n_rows:, :] = jnp.zeros((MOE_ROWS - n_rows, MOE_DOWN_TILE), F32)

        for parts in range(0, MOE_ROWS // MOE_SUB + 1):
            pl.when(n_sub == parts)(functools.partial(down, parts * MOE_SUB))


def moe_expert_ffn(xb, block_e, n_valid, n_used, w_gate, w_up, w_down):
    n_blk = xb.shape[0] // MOE_ROWS
    n_up = D_FF // MOE_UP_TILE
    n_down = D_MODEL // MOE_DOWN_TILE

    def last_used(b, nu):
        return jnp.minimum(b, nu[0] - 1)

    def up_idx(b, s, nu):
        return jnp.where(b < nu[0], jnp.minimum(s, n_up - 1), n_up - 1)

    def down_idx(b, s, nu):
        return jnp.where(b < nu[0], jnp.maximum(s - n_up, 0), n_down - 1)

    grid_spec = pltpu.PrefetchScalarGridSpec(
        num_scalar_prefetch=3,
        grid=(n_blk, n_up + n_down),
        in_specs=[
            pl.BlockSpec((MOE_ROWS, D_MODEL), lambda b, s, be, nv, nu: (last_used(b, nu), 0),
                         pipeline_mode=pl.Buffered(1)),
            pl.BlockSpec((None, D_MODEL, MOE_UP_TILE),
                         lambda b, s, be, nv, nu: (be[last_used(b, nu)], 0, up_idx(b, s, nu))),
            pl.BlockSpec((None, D_MODEL, MOE_UP_TILE),
                         lambda b, s, be, nv, nu: (be[last_used(b, nu)], 0, up_idx(b, s, nu))),
            pl.BlockSpec((None, D_FF, MOE_DOWN_TILE),
                         lambda b, s, be, nv, nu: (be[last_used(b, nu)], 0, down_idx(b, s, nu))),
        ],
        out_specs=pl.BlockSpec((MOE_ROWS, MOE_DOWN_TILE),
                               lambda b, s, be, nv, nu: (b, jnp.maximum(s - n_up, 0))),
        scratch_shapes=[pltpu.VMEM((MOE_ROWS, D_FF), BF)],
    )
    return pl.pallas_call(
        _moe_ffn_kernel,
        grid_spec=grid_spec,
        out_shape=jax.ShapeDtypeStruct((n_blk * MOE_ROWS, D_MODEL), F32),
        compiler_params=_cparams("arbitrary", "arbitrary"),
        name="moe_expert_ffn",
    )(block_e, n_valid, n_used, xb, w_gate, w_up, w_down)


def _moe_combine_kernel(idx_ref, idx_next_ref, x_ref, wt_ref, mod_ref, g_ref, yb_hbm, o_ref, buf, sem,
                        *, final_norm):
    t = pl.program_id(0)
    tm = x_ref.shape[0]
    slot = t % 2

    @pl.when(t == 0)
    def _():
        _gather_tile(yb_hbm, idx_ref, tm, TOP_K, buf.at[0], sem.at[0], "start")

    @pl.when(t + 1 < pl.num_programs(0))
    def _():
        _gather_tile(yb_hbm, idx_next_ref, tm, TOP_K, buf.at[1 - slot], sem.at[1 - slot], "start")

    _gather_tile(yb_hbm, idx_ref, tm, TOP_K, buf.at[slot], sem.at[slot], "wait")
    w = wt_ref[...]
    y = w[:, 0:1] * buf[slot, 0] + w[:, 1:2] * buf[slot, 1]
    out = x_ref[...] + mod_ref[5:6, :] * y
    if final_norm:
        ms = jnp.mean(out * out, axis=-1, keepdims=True)
        out = out * lax.rsqrt(ms + NORM_EPS) * g_ref[...]
    o_ref[...] = out


def moe_combine(x, yb, dest, wt, mod, mod_row, final_g, tm=256):
    m = x.shape[0]
    tm = min(tm, m)
    n_tiles = m // tm
    row = pl.BlockSpec((tm, D_MODEL), lambda t: (t, 0))
    final_norm = final_g is not None
    g = final_g if final_norm else jnp.ones((D_MODEL,), F32)
    idx = dest.reshape(n_tiles, 1, TOP_K * tm)
    return pl.pallas_call(
        functools.partial(_moe_combine_kernel, final_norm=final_norm),
        grid=(n_tiles,),
        in_specs=[pl.BlockSpec((None, 1, TOP_K * tm), lambda t: (t, 0, 0), memory_space=pltpu.SMEM),
                  pl.BlockSpec((None, 1, TOP_K * tm), lambda t: (jnp.minimum(t + 1, n_tiles - 1), 0, 0),
                               memory_space=pltpu.SMEM),
                  row,
                  pl.BlockSpec((tm, V7X_LANES), lambda t: (t, 0)),
                  pl.BlockSpec((None, 6, D_MODEL), lambda t: (mod_row(t), 0, 0)),
                  pl.BlockSpec((1, D_MODEL), lambda t: (0, 0)),
                  pl.BlockSpec(memory_space=pl.ANY)],
        out_specs=row,
        out_shape=jax.ShapeDtypeStruct((m, D_MODEL), F32),
        scratch_shapes=[pltpu.VMEM((2, TOP_K, tm, D_MODEL), F32), pltpu.SemaphoreType.DMA((2,))],
        compiler_params=_cparams("arbitrary"),
        name="moe_combine",
    )(idx, idx, x, wt, mod, g.reshape(1, D_MODEL), yb)


def moe_layout(top_i):
    t = top_i.shape[0]
    n_assign = t * TOP_K
    flat_e = top_i.reshape(-1)
    onehot = (flat_e[:, None] == jnp.arange(N_EXPERTS, dtype=jnp.int32)[None, :]).astype(jnp.int32)
    csum = jnp.cumsum(onehot, axis=0)
    rank = jnp.take_along_axis(csum, flat_e[:, None], axis=1)[:, 0] - 1
    counts = csum[-1]
    padded = (counts + MOE_ROWS - 1) // MOE_ROWS * MOE_ROWS
    pends = jnp.cumsum(padded)
    pstarts = pends - padded
    dest = pstarts[flat_e] + rank
    n_blk = n_assign // MOE_ROWS + N_EXPERTS
    flat_t = jnp.repeat(jnp.arange(t, dtype=jnp.int32), TOP_K)
    slot_t = jnp.zeros((n_blk * MOE_ROWS,), jnp.int32).at[dest].set(flat_t)
    blk = jnp.arange(n_blk, dtype=jnp.int32)
    blk_start = blk * MOE_ROWS
    block_e = jnp.minimum(jnp.sum((pends[None, :] <= blk_start[:, None]).astype(jnp.int32), axis=1),
                          N_EXPERTS - 1)
    n_used = pends[-1] // MOE_ROWS
    n_valid = jnp.clip(counts[block_e] - (blk_start - pstarts[block_e]), 0, MOE_ROWS)
    n_valid = jnp.where(blk < n_used, n_valid, 0)
    tiles_per_blk = MOE_ROWS // MOE_GATHER_ROWS
    tile_off = jnp.arange(tiles_per_blk, dtype=jnp.int32) * MOE_GATHER_ROWS
    tile_valid = jnp.clip(n_valid[:, None] - tile_off[None, :], 0, MOE_GATHER_ROWS).reshape(-1)
    return (slot_t, dest.reshape(t, TOP_K).astype(jnp.int32), block_e.astype(jnp.int32),
            n_valid.astype(jnp.int32), n_used.astype(jnp.int32).reshape(1), tile_valid.astype(jnp.int32))


def moe_layer(x, norm_g, mod, mod_row_tm, router_w, router_b, w_gate, w_up, w_down, final_g):
    router_tm, combine_tm = 512, 512
    h, idx, wt = moe_router(x, norm_g, mod, mod_row_tm(router_tm), router_w, router_b, router_tm)
    slot_t, dest, block_e, n_valid, n_used, tile_valid = moe_layout(idx[:, :TOP_K])
    xb = moe_dispatch(h, slot_t, tile_valid)
    yb = moe_expert_ffn(xb, block_e, n_valid, n_used, w_gate, w_up, w_down)
    return moe_combine(x, yb, dest, wt, mod, mod_row_tm(combine_tm), final_g, combine_tm)


def split_w_in(w_in):
    head = w_in[:, :, :D_HEAD].astype(BF)
    body = jnp.pad(w_in[:, :, R_GQA_Q:R_END], ((0, 0), (0, 0), (0, D_TAIL - C_LOW)))
    low = jnp.pad(w_in[:, :, R_LOW:R_GQA_Q], ((0, 0), (0, 0), (C_LOW, D_TAIL - C_LOW - (R_GQA_Q - R_LOW))))
    return head, (body + low).astype(BF)


def kernel(x, c, ctx, c_ctx, w_ada, b_ada, norm1_g, norm2_g, w_in, na_rpb, gla_w_a2, gla_b_a, gla_norm_g, gqa_qn_g, gqa_kn_g, w_pa, w_pb, w_pc, w_out, dense_w_gate, dense_w_up, dense_w_down, router_w, router_b, moe_w_gate, moe_w_up, moe_w_down, final_norm_g):
    n_batch = x.shape[0]
    xl = x.reshape(n_batch * SEQ, D_MODEL)
    xc = ctx.reshape(n_batch * CTX_LEN, D_MODEL)
    cvec = jnp.zeros((8, D_MODEL), F32).at[:n_batch].set(c).at[n_batch].set(c_ctx)
    mods = ada_modulation(cvec, w_ada, b_ada)
    cos_t, sin_t = rope_tables()
    lat_row = _latent_mod_row
    ctx_row = _ctx_mod_row(n_batch)
    prep_tm = 256
    lat_table = lambda t: t % (SEQ // prep_tm)
    ctx_table = lambda t: SEQ // prep_tm

    w_head, w_tail = split_w_in(w_in)
    w_pa, w_pb, w_pc, w_out = (w.astype(BF) for w in (w_pa, w_pb, w_pc, w_out))
    dense_w_down = dense_w_down.astype(BF)

    for i in range(DEPTH):
        last = i == DEPTH - 1
        mod = mods[i]
        head_l, tail_l = in_proj(xl, norm1_g[i], mod, lat_row(1024), w_head, w_tail, i)
        head_c, tail_c = in_proj(xc, norm1_g[i], mod, ctx_row, w_head, w_tail, i)

        a_l = na_latent(head_l, head_c, na_bias_table(na_rpb[i]), n_batch)
        b_l, b_c = gla_bidirectional(head_l, head_c, tail_l, tail_c, gla_w_a2[i], gla_b_a[i],
                                     gla_norm_g[i], n_batch)
        q_l, k_l = gqa_prep(tail_l, cos_t, sin_t, gqa_qn_g[i], gqa_kn_g[i], lat_table, prep_tm)
        q_c, k_c = gqa_prep(tail_c, cos_t, sin_t, gqa_qn_g[i], gqa_kn_g[i], ctx_table, prep_tm)
        c_l = gqa_latent(q_l, k_l, k_c, tail_l, tail_c, n_batch)

        m_l = merge_branches(a_l, b_l, c_l, tail_l, w_pa, w_pb, w_pc, i)
        xl = matmul_residual(m_l, w_out, i, xl, mod, lat_row(1024), 2, tn=512)
        if not last:
            a_c = na_context(head_c, n_batch)
            c_c = gqa_context(q_c, k_c, tail_c, n_batch)
            m_c = merge_branches(a_c, b_c, c_c, tail_c, w_pa, w_pb, w_pc, i)
            xc = matmul_residual(m_c, w_out, i, xc, mod, ctx_row, 2, tn=512)

        j = i // 2
        if i % 2 == 0:
            def ffn(xs, mod_row_tm):
                u = ffn_up(xs, norm2_g[i], mod, mod_row_tm(1024), dense_w_gate[j], dense_w_up[j])
                return matmul_residual(u, dense_w_down, j, xs, mod, mod_row_tm(1024), 5)
            xl = ffn(xl, lat_row)
            if not last:
                xc = ffn(xc, lambda tm: ctx_row)
        else:
            fin = final_norm_g if last else None
            xl = moe_layer(xl, norm2_g[i], mod, lat_row, router_w[j], router_b[j],
                           moe_w_gate[j], moe_w_up[j], moe_w_down[j], fin)
            if not last:
                xc = moe_layer(xc, norm2_g[i], mod, lambda tm: ctx_row, router_w[j], router_b[j],
                               moe_w_gate[j], moe_w_up[j], moe_w_down[j], None)
    if (DEPTH - 1) % 2 == 0:
        xl = final_rmsnorm(xl, final_norm_g)
    return xl.reshape(n_batch, SEQ, D_MODEL)
```
